```python
import jax, jax.numpy as jnp
from jax import lax
import numpy as np

D_MODEL = 1024
BATCH = 8
SEQ = 2048
DEPTH = 1

MLA_HEADS = 8
MLA_NOPE = 64
MLA_ROPE = 32
MLA_V = 64
MLA_Q_LORA = 512
MLA_KV_LORA = 256
DIL_HEADS = 8
DIL_HEAD_DIM = 64
DIL_PATTERNS = ((128, 1), (512, 4), (2048, 16))
DIL_WIDTH = DIL_HEADS * DIL_HEAD_DIM
MIX_WIDTH = MLA_HEADS * MLA_V + DIL_WIDTH
IN_COLS = MLA_Q_LORA + MLA_KV_LORA + MLA_ROPE + 3 * DIL_WIDTH
SPLITS = (MLA_Q_LORA,
          MLA_Q_LORA + MLA_KV_LORA,
          MLA_Q_LORA + MLA_KV_LORA + MLA_ROPE,
          MLA_Q_LORA + MLA_KV_LORA + MLA_ROPE + DIL_WIDTH,
          MLA_Q_LORA + MLA_KV_LORA + MLA_ROPE + 2 * DIL_WIDTH)
D_FF = 2816
CONV_WIDTH = 3
ROPE_THETA = 10000.0
EPS = 1e-6
Q_BLOCK = 128
NEG_INF = -1e30

kernel_name = "hybrid_mla_dilated_convffn_adaln"


def rms_norm(x, g):
    xf = x.astype(jnp.float32)
    y = xf * lax.rsqrt(jnp.mean(xf * xf, axis=-1, keepdims=True) + EPS)
    return (y * g.astype(jnp.float32)).astype(x.dtype)


def rope(x, positions):
    d = x.shape[-1]
    half = d // 2
    inv_freq = jnp.power(ROPE_THETA, -2.0 * jnp.arange(half, dtype=jnp.float32) / d)
    ang = positions.astype(jnp.float32)[:, :, None, None] * inv_freq
    cos, sin = jnp.cos(ang), jnp.sin(ang)
    xf = x.astype(jnp.float32)
    x1, x2 = xf[..., :half], xf[..., half:]
    return jnp.concatenate([x1 * cos - x2 * sin, x1 * sin + x2 * cos], axis=-1).astype(x.dtype)


def causal_dense_attention(q, k, v, scale):
    B, S, H, Dk = q.shape
    Dv = v.shape[-1]
    nb = S // Q_BLOCK
    qb = q.reshape(B, nb, Q_BLOCK, H, Dk).transpose(1, 0, 2, 3, 4)
    kpos = jnp.arange(S)

    def one_block(args):
        qi, i = args
        s = jnp.einsum('bqhd,bkhd->bhqk', qi, k).astype(jnp.float32) * scale
        qpos = i * Q_BLOCK + jnp.arange(Q_BLOCK)
        mask = kpos[None, :] <= qpos[:, None]
        s = jnp.where(mask, s, NEG_INF)
        p = jax.nn.softmax(s, axis=-1).astype(v.dtype)
        return jnp.einsum('bhqk,bkhd->bqhd', p, v)

    out = lax.map(one_block, (qb, jnp.arange(nb)))
    return out.transpose(1, 0, 2, 3, 4).reshape(B, S, H, Dv)


def banded_causal_attention(q, k, v, span):
    N, L, H, D = q.shape
    blk = span
    nb = -(-L // blk)
    Lp = nb * blk
    pad = ((0, 0), (0, Lp - L), (0, 0), (0, 0))
    q, k, v = jnp.pad(q, pad), jnp.pad(k, pad), jnp.pad(v, pad)
    qb = q.reshape(N, nb, blk, H, D)

    def two_blocks(t):
        tb = jnp.pad(t, ((0, 0), (blk, 0), (0, 0), (0, 0))).reshape(N, nb + 1, blk, H, D)
        return jnp.concatenate([tb[:, :-1], tb[:, 1:]], axis=2)

    kb, vb = two_blocks(k), two_blocks(v)
    s = jnp.einsum('nbqhd,nbkhd->nbhqk', qb, kb).astype(jnp.float32) * (D ** -0.5)
    blk_idx = jnp.arange(nb)[:, None, None]
    qry_pos = blk_idx * blk + jnp.arange(blk)[None, :, None]
    key_pos = (blk_idx - 1) * blk + jnp.arange(2 * blk)[None, None, :]
    dist = qry_pos - key_pos
    mask = (dist >= 0) & (dist <= span) & (key_pos >= 0)
    s = jnp.where(mask[None, :, None], s, NEG_INF)
    m = jnp.max(s, axis=-1, keepdims=True)
    e = jnp.exp(s - m)
    denom = jnp.sum(e, axis=-1, keepdims=True)
    p = (e / denom).astype(v.dtype)
    o = jnp.einsum('nbhqk,nbkhd->nbqhd', p, vb).reshape(N, Lp, H, D)[:, :L]
    lse = (m + jnp.log(denom))[..., 0]
    lse = lse.transpose(0, 1, 3, 2).reshape(N, Lp, H)[:, :L]
    return o, lse


def to_strided(t, dil):
    B, S, H, D = t.shape
    return t.reshape(B, S // dil, dil, H, D).transpose(0, 2, 1, 3, 4).reshape(B * dil, S // dil, H, D)


def dilated_attention(q, k, v):
    B, S, H, D = q.shape
    outs, lses = [], []
    for window, dil in DIL_PATTERNS:
        L = S // dil
        o, lse = banded_causal_attention(to_strided(q, dil), to_strided(k, dil),
                                         to_strided(v, dil), window // dil)
        outs.append(o.reshape(B, dil, L, H, D).transpose(0, 2, 1, 3, 4).reshape(B, S, H, D))
        lses.append(lse.reshape(B, dil, L, H).transpose(0, 2, 1, 3).reshape(B, S, H))
    w = jax.nn.softmax(jnp.stack(lses, axis=0), axis=0)
    out = jnp.sum(w[..., None] * jnp.stack(outs, axis=0).astype(jnp.float32), axis=0)
    return out.astype(q.dtype)


def causal_depthwise_conv(u, w, b):
    K = w.shape[0]
    S = u.shape[1]
    up = jnp.pad(u, ((0, 0), (K - 1, 0), (0, 0)))
    y = b
    for kk in range(K):
        y = y + up[:, kk:kk + S] * w[kk]
    return y


def setup_inputs(seed: int = 0) -> dict:
    key = jax.random.key(seed)
    ks = jax.random.split(key, 24)
    nrm = jax.random.normal
    L = DEPTH

    def gain(k, n):
        return 1.0 + 0.05 * nrm(k, (L, n), jnp.float32)

    x = nrm(ks[0], (BATCH, SEQ, D_MODEL), jnp.float32)
    c = nrm(ks[1], (BATCH, D_MODEL), jnp.float32)
    positions = (jnp.arange(SEQ, dtype=jnp.int32)[None, :]
                 + jax.random.randint(ks[2], (BATCH, 1), 0, 4096, dtype=jnp.int32))
    return {
        "x": x,
        "c": c,
        "positions": positions,
        "w_ada": nrm(ks[3], (L, D_MODEL, 6 * D_MODEL), jnp.float32) * (0.5 * D_MODEL ** -0.5),
        "b_ada": 0.02 * nrm(ks[4], (L, 6 * D_MODEL), jnp.float32),
        "g_mix_norm": gain(ks[5], D_MODEL),
        "w_in": nrm(ks[6], (L, D_MODEL, IN_COLS), jnp.float32) * D_MODEL ** -0.5,
        "g_q_lat": gain(ks[7], MLA_Q_LORA),
        "w_q_b": nrm(ks[8], (L, MLA_Q_LORA, MLA_HEADS * (MLA_NOPE + MLA_ROPE)), jnp.float32) * MLA_Q_LORA ** -0.5,
        "g_kv_lat": gain(ks[9], MLA_KV_LORA),
        "w_kv_b": nrm(ks[10], (L, MLA_KV_LORA, MLA_HEADS * (MLA_NOPE + MLA_V)), jnp.float32) * MLA_KV_LORA ** -0.5,
        "g_mla_q_nope": gain(ks[11], MLA_NOPE),
        "g_mla_q_pe": gain(ks[12], MLA_ROPE),
        "g_mla_k_nope": gain(ks[13], MLA_NOPE),
        "g_mla_k_pe": gain(ks[14], MLA_ROPE),
        "g_dil_q": gain(ks[15], DIL_HEAD_DIM),
        "g_dil_k": gain(ks[16], DIL_HEAD_DIM),
        "w_o": nrm(ks[17], (L, MIX_WIDTH, D_MODEL), jnp.float32) * MIX_WIDTH ** -0.5,
        "g_ffn_norm": gain(ks[18], D_MODEL),
        "w_up": nrm(ks[19], (L, D_MODEL, 2 * D_FF), jnp.float32) * D_MODEL ** -0.5,
        "w_conv": nrm(ks[20], (L, CONV_WIDTH, 2 * D_FF), jnp.float32) * CONV_WIDTH ** -0.5,
        "b_conv": 0.02 * nrm(ks[21], (L, 2 * D_FF), jnp.float32),
        "w_down": nrm(ks[22], (L, D_FF, D_MODEL), jnp.float32) * D_FF ** -0.5,
    }


def reference(x, c, positions, w_ada, b_ada, g_mix_norm, w_in, g_q_lat, w_q_b, g_kv_lat, w_kv_b,
              g_mla_q_nope, g_mla_q_pe, g_mla_k_nope, g_mla_k_pe, g_dil_q, g_dil_k, w_o,
              g_ffn_norm, w_up, w_conv, b_conv, w_down):
    B, S, _ = x.shape
    for l in range(DEPTH):
        mod = jax.nn.silu(c) @ w_ada[l] + b_ada[l]
        sh1, sc1, g1, sh2, sc2, g2 = jnp.split(mod, 6, axis=-1)

        h = rms_norm(x, g_mix_norm[l]) * (1.0 + sc1[:, None]) + sh1[:, None]
        proj = h @ w_in[l]
        q_lat, kv_lat, k_pe, qd, kd, vd = jnp.split(proj, SPLITS, axis=-1)

        q = (rms_norm(q_lat, g_q_lat[l]) @ w_q_b[l]).reshape(B, S, MLA_HEADS, MLA_NOPE + MLA_ROPE)
        q_nope = rms_norm(q[..., :MLA_NOPE], g_mla_q_nope[l])
        q_pe = rope(rms_norm(q[..., MLA_NOPE:], g_mla_q_pe[l]), positions)
        kv = (rms_norm(kv_lat, g_kv_lat[l]) @ w_kv_b[l]).reshape(B, S, MLA_HEADS, MLA_NOPE + MLA_V)
        k_nope = rms_norm(kv[..., :MLA_NOPE], g_mla_k_nope[l])
        v_mla = kv[..., MLA_NOPE:]
        k_pe = rope(rms_norm(k_pe, g_mla_k_pe[l])[:, :, None, :], positions)
        k_mla = jnp.concatenate([k_nope, jnp.broadcast_to(k_pe, (B, S, MLA_HEADS, MLA_ROPE))], axis=-1)
        q_mla = jnp.concatenate([q_nope, q_pe], axis=-1)
        o_mla = causal_dense_attention(q_mla, k_mla, v_mla, (MLA_NOPE + MLA_ROPE) ** -0.5)

        qd = rope(rms_norm(qd.reshape(B, S, DIL_HEADS, DIL_HEAD_DIM), g_dil_q[l]), positions)
        kd = rope(rms_norm(kd.reshape(B, S, DIL_HEADS, DIL_HEAD_DIM), g_dil_k[l]), positions)
        vd = vd.reshape(B, S, DIL_HEADS, DIL_HEAD_DIM)
        o_dil = dilated_attention(qd, kd, vd)

        mix = jnp.concatenate([o_mla.reshape(B, S, MLA_HEADS * MLA_V),
                               o_dil.reshape(B, S, DIL_WIDTH)], axis=-1) @ w_o[l]
        x = x + g1[:, None] * mix

        h2 = rms_norm(x, g_ffn_norm[l]) * (1.0 + sc2[:, None]) + sh2[:, None]
        u = causal_depthwise_conv(h2 @ w_up[l], w_conv[l], b_conv[l])
        gate, val = jnp.split(u, 2, axis=-1)
        x = x + g2[:, None] * ((jax.nn.silu(gate) * val) @ w_down[l])
    return x
```

```python
import functools

import numpy as np
import jax
import jax.numpy as jnp
from jax import lax
from jax.experimental import pallas as pl
from jax.experimental.pallas import tpu as pltpu

F32 = jnp.float32
BF16 = jnp.bfloat16

D_MODEL = 1024
N_HEADS = 8
MLA_NOPE = 64
MLA_ROPE = 32
MLA_V = 64
MLA_Q_LORA = 512
MLA_KV_LORA = 256
DIL_HEAD_DIM = 64
DIL_WIDTH = N_HEADS * DIL_HEAD_DIM
DIL_PATTERNS = ((128, 1), (512, 4), (2048, 16))
D_FF = 2816
ROPE_THETA = 10000.0
EPS = 1e-6
MASK_BIAS = -1e30
M_INIT = -5e29
LOG2E = 1.4426950408889634

LANES = 128
MXU_DIM = 256
VMEM_LIMIT = 56 * 1024 * 1024
ADA_TN = 1024
PROJ_TM = 512
ATT_TQ = 256
ATT_TK = 256
FFN_TM = 512
FFN_TF = 256
N_FT = D_FF // FFN_TF

C_QLAT = 0
C_KVLAT = C_QLAT + MLA_Q_LORA
C_KPE = C_KVLAT + MLA_KV_LORA
C_QD = C_KPE + LANES
C_KD = C_QD + DIL_WIDTH
C_VD = C_KD + DIL_WIDTH
IN_COLS_P = C_VD + DIL_WIDTH


def _dot(a, b):
    return jnp.dot(a, b, preferred_element_type=F32)


def _dot_nt(a, b):
    return lax.dot_general(a, b, (((1,), (1,)), ((), ())), preferred_element_type=F32)


def _const_spec(shape):
    nd = len(shape)
    return pl.BlockSpec(shape, lambda *_: (0,) * nd, pipeline_mode=pl.Buffered(1))


def _split_bf16(v):
    hi = v.astype(BF16)
    lo = (v - hi.astype(F32)).astype(BF16)
    return hi, lo


def _ada_kernel(c_ref, w_ref, b_ref, o_ref):
    c = c_ref[...]
    a = c * (1.0 / (1.0 + jnp.exp(-c)))
    a_hi, a_lo = _split_bf16(a)
    w_hi, w_lo = _split_bf16(w_ref[...])
    o_ref[...] = _dot(a_hi, w_hi) + _dot(a_lo, w_hi) + _dot(a_hi, w_lo) + b_ref[...]


def _ada(c, w_ada, b_ada):
    B = c.shape[0]
    n = w_ada.shape[1]
    return pl.pallas_call(
        _ada_kernel,
        out_shape=jax.ShapeDtypeStruct((B, n), F32),
        grid=(n // ADA_TN,),
        in_specs=[pl.BlockSpec((B, D_MODEL), lambda j: (0, 0)),
                  pl.BlockSpec((D_MODEL, ADA_TN), lambda j: (0, j)),
                  pl.BlockSpec((1, ADA_TN), lambda j: (0, j))],
        out_specs=pl.BlockSpec((B, ADA_TN), lambda j: (0, j)),
        compiler_params=pltpu.CompilerParams(dimension_semantics=("arbitrary",),
                                             vmem_limit_bytes=VMEM_LIMIT),
        name="ada",
    )(c, w_ada, b_ada)


def _rms_rows(v, width):
    ms = jnp.sum(v * v, axis=-1, keepdims=True) * (1.0 / width)
    return v * lax.rsqrt(ms + EPS)


def _seg_rms(v, seg_ref, seg_width):
    outs = []
    for c0 in range(0, v.shape[1], MXU_DIM):
        blk = v[:, c0:c0 + MXU_DIM]
        ss = _dot((blk * blk).astype(BF16), seg_ref[...])
        outs.append(blk * lax.rsqrt(ss * (1.0 / seg_width) + EPS))
    return outs[0] if len(outs) == 1 else jnp.concatenate(outs, axis=1)


def _rope(v, cos, sin_signed):
    outs = []
    for c0 in range(0, v.shape[1], LANES):
        blk = v[:, c0:c0 + LANES]
        outs.append(blk * cos + pltpu.roll(blk, LANES // 2, 1) * sin_signed)
    return outs[0] if len(outs) == 1 else jnp.concatenate(outs, axis=1)


def _proj_kernel(x_ref, pos_ref, mod_ref, gmix_ref, win_ref, gql_ref, wqb_ref, gkvl_ref, wkvb_ref,
                 gqn_ref, gqpe_ref, gkn_ref, gkpe_ref, gdq_ref, gdk_ref,
                 seg64_ref, segpe_ref, segpair_ref, freq_ref, expand_ref,
                 qn_ref, qpe_ref, kn_ref, kpe_ref, vm_ref, qd_ref, kd_ref, vd_ref):
    x = x_ref[0]
    sh1 = mod_ref[0, 0:1, :]
    sc1 = mod_ref[0, 1:2, :]
    h = (_rms_rows(x, D_MODEL) * (gmix_ref[...] * (1.0 + sc1)) + sh1).astype(BF16)

    pos = pos_ref[0].astype(F32)
    ang = freq_ref[...] * pos
    cs = jnp.concatenate([jnp.cos(ang), jnp.sin(ang)], axis=0).T
    cs_hi, cs_lo = _split_bf16(cs)
    tabs = _dot(cs_hi, expand_ref[...]) + _dot(cs_lo, expand_ref[...])
    cos_d, sin_d = tabs[:, 0:LANES], tabs[:, LANES:2 * LANES]
    cos_p, sin_p = tabs[:, 2 * LANES:3 * LANES], tabs[:, 3 * LANES:4 * LANES]

    q_lat = _dot(h, win_ref[:, C_QLAT:C_QLAT + MLA_Q_LORA])
    q_in = (_rms_rows(q_lat, MLA_Q_LORA) * gql_ref[...]).astype(BF16)
    q = _dot(q_in, wqb_ref[...])
    n_nope = N_HEADS * MLA_NOPE
    qn_ref[0] = (_seg_rms(q[:, :n_nope], seg64_ref, MLA_NOPE) * gqn_ref[...]).astype(BF16)
    q_pe = _seg_rms(q[:, n_nope:], segpe_ref, MLA_ROPE) * gqpe_ref[...]
    qpe_ref[0] = _rope(q_pe, cos_p, sin_p).astype(BF16)

    kv_lat = _dot(h, win_ref[:, C_KVLAT:C_KVLAT + MLA_KV_LORA])
    kv_in = (_rms_rows(kv_lat, MLA_KV_LORA) * gkvl_ref[...]).astype(BF16)
    kv = _dot(kv_in, wkvb_ref[...])
    kn_ref[0] = (_seg_rms(kv[:, :n_nope], seg64_ref, MLA_NOPE) * gkn_ref[...]).astype(BF16)
    vm_ref[0] = kv[:, n_nope:].astype(BF16)
    k_pe = _dot(h, win_ref[:, C_KPE:C_KPE + LANES])
    kpe_ref[0] = _rope(_rms_rows(k_pe, LANES) * gkpe_ref[...], cos_p, sin_p).astype(BF16)

    qd = _seg_rms(_dot(h, win_ref[:, C_QD:C_QD + DIL_WIDTH]), segpair_ref, DIL_HEAD_DIM)
    qd_ref[0] = _rope(qd * gdq_ref[...], cos_d, sin_d).astype(BF16)
    kd = _seg_rms(_dot(h, win_ref[:, C_KD:C_KD + DIL_WIDTH]), segpair_ref, DIL_HEAD_DIM)
    kd_ref[0] = _rope(kd * gdk_ref[...], cos_d, sin_d).astype(BF16)
    vd_ref[0] = _dot(h, win_ref[:, C_VD:C_VD + DIL_WIDTH]).astype(BF16)


def _proj(x, pos3, mod3, consts):
    B, S, _ = x.shape
    tm = PROJ_TM
    row = lambda w: pl.BlockSpec((1, tm, w), lambda b, i: (b, i, 0))
    out_widths = (512, 256, 512, LANES, 512, 512, 512, 512)
    in_specs = [row(D_MODEL),
                pl.BlockSpec((1, 1, tm), lambda b, i: (b, 0, i)),
                pl.BlockSpec((1, 6, D_MODEL), lambda b, i: (b, 0, 0))]
    in_specs += [_const_spec(a.shape) for a in consts]
    return pl.pallas_call(
        _proj_kernel,
        out_shape=tuple(jax.ShapeDtypeStruct((B, S, w), BF16) for w in out_widths),
        grid=(B, S // tm),
        in_specs=in_specs,
        out_specs=tuple(row(w) for w in out_widths),
        compiler_params=pltpu.CompilerParams(dimension_semantics=("arbitrary", "arbitrary"),
                                             vmem_limit_bytes=VMEM_LIMIT),
        name="proj",
    )(x, pos3, mod3, *consts)


def _lane_iota(shape):
    return lax.broadcasted_iota(jnp.int32, shape, len(shape) - 1)


def _pair_attention(load_q2, load_k, load_v, bias_ref, far_bias, near, o_ref, m_scr, l_scr, acc_scr):
    tq, tk = ATT_TQ, ATT_TK
    assert tq == tk
    n_q = o_ref.shape[1] // tq
    lane = _lane_iota((tq, LANES))
    first = lane < LANES // 2

    def q_body(i, carry):
        q_rows = pl.ds(pl.multiple_of(i * tq, tq), tq)
        q2 = load_q2(q_rows)
        m_scr[...] = jnp.full(m_scr.shape, M_INIT, F32)
        l_scr[...] = jnp.zeros(l_scr.shape, F32)
        acc_scr[...] = jnp.zeros(acc_scr.shape, F32)

        def step(j, bias_idx):
            k_rows = pl.ds(pl.multiple_of(j * tk, tk), tk)
            s = _dot_nt(q2, load_k(k_rows))
            if bias_idx is not None:
                s = s + bias_ref[bias_idx]
            m_old = m_scr[...]
            m_new = jnp.maximum(m_old, jnp.max(s, axis=-1, keepdims=True))
            alpha = jnp.exp2(m_old - m_new)
            p = jnp.exp2(s - m_new)
            l_scr[...] = alpha * l_scr[...] + jnp.sum(p, axis=-1, keepdims=True)
            m_scr[...] = m_new
            v = load_v(k_rows)
            v2 = jnp.concatenate([jnp.where(first, v, jnp.zeros_like(v)),
                                  jnp.where(first, jnp.zeros_like(v), v)], axis=0)
            p2 = jnp.concatenate([p[:tq], p[tq:]], axis=1).astype(BF16)
            a_l = jnp.where(first, alpha[:tq], alpha[tq:])
            acc_scr[...] = acc_scr[...] * a_l + _dot(p2, v2)

        def far_body(j, c):
            step(j, far_bias)
            return c

        lax.fori_loop(0, i - (near - 1), far_body, 0)
        for d in range(near - 1, 0, -1):
            @pl.when(i >= d)
            def _(d=d):
                step(i - d, d)
        step(i, 0)

        inv_l = 1.0 / l_scr[...]
        o = acc_scr[...] * jnp.where(first, inv_l[:tq], inv_l[tq:])
        o_ref[0, q_rows, :] = o.astype(o_ref.dtype)
        return carry

    lax.fori_loop(0, n_q, q_body, 0)


def _mla_kernel(qn_ref, qpe_ref, kn_ref, kpe_ref, v_ref, bias_ref, o_ref, m_scr, l_scr, acc_scr):
    pair = pl.program_id(1)
    lane = _lane_iota((ATT_TQ, 2 * LANES))
    pe_head = (lane % (LANES // 2)) // (MLA_ROPE // 2)
    pe_base = 2 * (pair % 2)
    is_pe = lane >= LANES
    mask_a = (lane < LANES // 2) | (is_pe & (pe_head == pe_base))
    mask_b = ((lane >= LANES // 2) & (lane < LANES)) | (is_pe & (pe_head == pe_base + 1))

    def load_q2(rows):
        q = jnp.concatenate([qn_ref[0, rows, :], qpe_ref[0, rows, :]], axis=1)
        z = jnp.zeros_like(q)
        return jnp.concatenate([jnp.where(mask_a, q, z), jnp.where(mask_b, q, z)], axis=0)

    def load_k(rows):
        return jnp.concatenate([kn_ref[0, rows, :], kpe_ref[0, rows, :]], axis=1)

    def load_v(rows):
        return v_ref[0, rows, :]

    _pair_attention(load_q2, load_k, load_v, bias_ref, None, 1, o_ref, m_scr, l_scr, acc_scr)


def _dil_kernel(q_ref, k_ref, v_ref, bias_ref, o_ref, m_scr, l_scr, acc_scr):
    lane = _lane_iota((ATT_TQ, LANES))
    is_a = (lane % (LANES // 2)) < DIL_HEAD_DIM // 2

    def load_q2(rows):
        q = q_ref[0, rows, :]
        z = jnp.zeros_like(q)
        return jnp.concatenate([jnp.where(is_a, q, z), jnp.where(is_a, z, q)], axis=0)

    def load_k(rows):
        return k_ref[0, rows, :]

    def load_v(rows):
        return v_ref[0, rows, :]

    n_bias = bias_ref.shape[0]
    _pair_attention(load_q2, load_k, load_v, bias_ref, n_bias - 1, n_bias - 1,
                    o_ref, m_scr, l_scr, acc_scr)


def _attn_call(kernel, name, arrays, specs, bias, B, S):
    col = lambda fn: pl.BlockSpec((1, S, LANES), fn)
    in_specs = [col(fn) for fn in specs] + [_const_spec(bias.shape)]
    return pl.pallas_call(
        kernel,
        out_shape=jax.ShapeDtypeStruct((B, S, N_HEADS * MLA_V), BF16),
        grid=(B, N_HEADS // 2),
        in_specs=in_specs,
        out_specs=col(lambda b, p: (b, 0, p)),
        scratch_shapes=[pltpu.VMEM((2 * ATT_TQ, 1), F32), pltpu.VMEM((2 * ATT_TQ, 1), F32),
                        pltpu.VMEM((ATT_TQ, LANES), F32)],
        compiler_params=pltpu.CompilerParams(dimension_semantics=("arbitrary", "arbitrary"),
                                             vmem_limit_bytes=VMEM_LIMIT),
        name=name,
    )(*arrays, bias)


def _mla_bias():
    d = np.arange(ATT_TQ)[:, None] - np.arange(ATT_TK)[None, :]
    tile = np.where(d >= 0, 0.0, MASK_BIAS).astype(np.float32)
    return jnp.asarray(np.concatenate([tile, tile], axis=0)[None])


def _dil_bias():
    (far_window, far_dil), = [(w, d) for w, d in DIL_PATTERNS if w == max(p[0] for p in DIL_PATTERNS)]
    assert ATT_TK % far_dil == 0
    max_near = max(w for w, d in DIL_PATTERNS if w < far_window)
    n_tiles = -(-max_near // ATT_TK) + 2
    tiles = []
    for blk in range(n_tiles):
        delta = blk * ATT_TK + np.arange(ATT_TQ)[:, None] - np.arange(ATT_TK)[None, :]
        mult = np.zeros(delta.shape, np.int64)
        for window, dil in DIL_PATTERNS:
            mult += (delta >= 0) & (delta % dil == 0) & (delta <= window)
        tile = np.where(mult > 0, np.log2(np.maximum(mult, 1)), MASK_BIAS).astype(np.float32)
        tiles.append(np.concatenate([tile, tile], axis=0))
    return jnp.asarray(np.stack(tiles))


def _shift_rows(y, prev, k):
    head = pltpu.roll(jnp.concatenate([prev, y[:8]], axis=0), k, 0)[8:]
    return jnp.concatenate([head, pltpu.roll(y, k, 0)[8:]], axis=0)


def _ffn_kernel(x_ref, om_ref, od_ref, mod_ref, gffn_ref, wo_ref, wup_ref, wconv_ref, bconv_ref,
                wdown_ref, out_ref, carry_scr, acc_scr, h2_scr):
    g1 = mod_ref[0, 2:3, :]
    sh2 = mod_ref[0, 3:4, :]
    sc2 = mod_ref[0, 4:5, :]
    g2 = mod_ref[0, 5:6, :]
    half = wo_ref.shape[0] // 2
    mix = _dot(om_ref[0], wo_ref[:half, :]) + _dot(od_ref[0], wo_ref[half:, :])
    x1 = x_ref[0] + g1 * mix
    h2_scr[...] = (_rms_rows(x1, D_MODEL) * (gffn_ref[...] * (1.0 + sc2)) + sh2).astype(BF16)
    out_ref[0] = x1
    acc_scr[...] = jnp.zeros(acc_scr.shape, F32)

    @pl.when(pl.program_id(1) == 0)
    def _():
        carry_scr[...] = jnp.zeros(carry_scr.shape, F32)

    def conv_tile(t):
        y = _dot(h2_scr[...], wup_ref[t])
        prev = carry_scr[t]
        carry_scr[t] = y[FFN_TM - 8:, :]
        w = wconv_ref[t]
        return (w[2:3, :] * y + w[1:2, :] * _shift_rows(y, prev, 1)
                + w[0:1, :] * _shift_rows(y, prev, 2) + bconv_ref[t])

    def f_body(f, c):
        gate = conv_tile(f)
        val = conv_tile(f + N_FT)
        a = (gate * (1.0 / (1.0 + jnp.exp(-gate))) * val).astype(BF16)
        acc_scr[...] += _dot(a, wdown_ref[f])
        return c

    lax.fori_loop(0, N_FT, f_body, 0)
    out_ref[0] = out_ref[0] + g2 * acc_scr[...]


def _ffn(x, o_mla, o_dil, mod3, g_ffn, w_o, w_up_t, w_conv_t, b_conv_t, w_down_t):
    B, S, _ = x.shape
    tm = FFN_TM
    row = lambda w: pl.BlockSpec((1, tm, w), lambda b, i: (b, i, 0))
    consts = (g_ffn, w_o, w_up_t, w_conv_t, b_conv_t, w_down_t)
    return pl.pallas_call(
        _ffn_kernel,
        out_shape=jax.ShapeDtypeStruct(x.shape, F32),
        grid=(B, S // tm),
        in_specs=[row(D_MODEL), row(o_mla.shape[2]), row(o_dil.shape[2]),
                  pl.BlockSpec((1, 6, D_MODEL), lambda b, i: (b, 0, 0))]
                 + [_const_spec(a.shape) for a in consts],
        out_specs=row(D_MODEL),
        scratch_shapes=[pltpu.VMEM((2 * N_FT, 8, FFN_TF), F32), pltpu.VMEM((tm, D_MODEL), F32),
                        pltpu.VMEM((tm, D_MODEL), BF16)],
        compiler_params=pltpu.CompilerParams(dimension_semantics=("arbitrary", "arbitrary"),
                                             vmem_limit_bytes=VMEM_LIMIT),
        name="ffn",
    )(x, o_mla, o_dil, mod3, *consts)


def _layout_indices():
    lane = np.arange(LANES)
    dil_head = (lane % 64) // 32
    dil_feat = (lane // 64) * 32 + lane % 32
    dil_cols = np.concatenate([(2 * g + dil_head) * DIL_HEAD_DIM + dil_feat for g in range(4)])
    pe_head = (lane % 64) // 16
    pe_feat = (lane // 64) * 16 + lane % 16
    qb_nope = np.concatenate([h * (MLA_NOPE + MLA_ROPE) + np.arange(MLA_NOPE) for h in range(N_HEADS)])
    qb_pe = np.concatenate([(4 * g + pe_head) * (MLA_NOPE + MLA_ROPE) + MLA_NOPE + pe_feat
                            for g in range(2)])
    kvb_k = np.concatenate([h * (MLA_NOPE + MLA_V) + np.arange(MLA_NOPE) for h in range(N_HEADS)])
    kvb_v = np.concatenate([h * (MLA_NOPE + MLA_V) + MLA_NOPE + np.arange(MLA_V) for h in range(N_HEADS)])
    return dict(dil_cols=dil_cols, dil_feat=dil_feat, pe_feat=pe_feat, pe_head=pe_head,
                dil_head=dil_head, qb_cols=np.concatenate([qb_nope, qb_pe]),
                kvb_cols=np.concatenate([kvb_k, kvb_v]))


def _segment_matrices(ix):
    i = np.arange(MXU_DIM)
    lane = i % LANES
    seg64 = i // 64
    segpe = (i // LANES) * 4 + ix["pe_head"][lane]
    segpair = (i // LANES) * 2 + ix["dil_head"][lane]
    same = lambda s: jnp.asarray((s[:, None] == s[None, :]).astype(np.float32), dtype=BF16)
    return same(seg64), same(segpe), same(segpair)


def _rope_constants(ix):
    def inv_freq(d):
        half = d // 2
        return jnp.power(ROPE_THETA, -2.0 * jnp.arange(half, dtype=F32) / d)
    fd, fp = inv_freq(DIL_HEAD_DIM), inv_freq(MLA_ROPE)
    nd, npe = fd.shape[0], fp.shape[0]
    freq = jnp.concatenate([fd, fp, jnp.zeros((64 - nd - npe,), F32)])[:, None]
    lane = np.arange(LANES)
    sign = np.where(lane < 64, -1.0, 1.0)
    e = np.zeros((2 * 64, 4 * LANES), np.float32)
    e[lane % 32, lane] = 1.0
    e[64 + lane % 32, LANES + lane] = sign
    e[nd + lane % 16, 2 * LANES + lane] = 1.0
    e[64 + nd + lane % 16, 3 * LANES + lane] = sign
    return freq, jnp.asarray(e, dtype=BF16)


def kernel(x, c, positions, w_ada, b_ada, g_mix_norm, w_in, g_q_lat, w_q_b, g_kv_lat, w_kv_b,
           g_mla_q_nope, g_mla_q_pe, g_mla_k_nope, g_mla_k_pe, g_dil_q, g_dil_k, w_o,
           g_ffn_norm, w_up, w_conv, b_conv, w_down):
    B, S, D = x.shape
    assert D == D_MODEL and S % PROJ_TM == 0 and S % ATT_TQ == 0 and S % FFN_TM == 0
    assert S <= max(w for w, _ in DIL_PATTERNS)
    ix = _layout_indices()
    seg64, segpe, segpair = _segment_matrices(ix)
    freq, expand = _rope_constants(ix)
    mla_scale = (MLA_NOPE + MLA_ROPE) ** -0.5 * LOG2E
    dil_scale = DIL_HEAD_DIM ** -0.5 * LOG2E
    pos3 = positions.reshape(B, 1, S)

    for l in range(w_ada.shape[0]):
        mod3 = _ada(c, w_ada[l], b_ada[l][None, :]).reshape(B, 6, D)

        wi = w_in[l]
        kpe_cols = MLA_Q_LORA + MLA_KV_LORA + ix["pe_feat"]
        qd0 = MLA_Q_LORA + MLA_KV_LORA + MLA_ROPE
        w_in_p = jnp.concatenate([
            wi[:, :MLA_Q_LORA + MLA_KV_LORA], wi[:, kpe_cols],
            wi[:, qd0 + ix["dil_cols"]], wi[:, qd0 + DIL_WIDTH + ix["dil_cols"]],
            wi[:, qd0 + 2 * DIL_WIDTH:]], axis=1).astype(BF16)
        consts = (
            g_mix_norm[l][None, :], w_in_p,
            g_q_lat[l][None, :], w_q_b[l][:, ix["qb_cols"]].astype(BF16),
            g_kv_lat[l][None, :], w_kv_b[l][:, ix["kvb_cols"]].astype(BF16),
            (jnp.tile(g_mla_q_nope[l], N_HEADS) * mla_scale)[None, :],
            (jnp.tile(g_mla_q_pe[l][ix["pe_feat"]], 2) * mla_scale)[None, :],
            jnp.tile(g_mla_k_nope[l], N_HEADS)[None, :],
            g_mla_k_pe[l][ix["pe_feat"]][None, :],
            (jnp.tile(g_dil_q[l][ix["dil_feat"]], 4) * dil_scale)[None, :],
            jnp.tile(g_dil_k[l][ix["dil_feat"]], 4)[None, :],
            seg64, segpe, segpair, freq, expand)
        qn, qpe, kn, kpe, vm, qd, kd, vd = _proj(x, pos3, mod3, consts)

        o_mla = _attn_call(
            _mla_kernel, "mla", (qn, qpe, kn, kpe, vm),
            (lambda b, p: (b, 0, p), lambda b, p: (b, 0, p // 2), lambda b, p: (b, 0, p),
             lambda b, p: (b, 0, 0), lambda b, p: (b, 0, p)), _mla_bias(), B, S)
        pair = lambda b, p: (b, 0, p)
        o_dil = _attn_call(_dil_kernel, "dil", (qd, kd, vd), (pair, pair, pair), _dil_bias(), B, S)

        w_up_t = w_up[l].astype(BF16).reshape(D, 2 * N_FT, FFN_TF).transpose(1, 0, 2)
        w_conv_t = w_conv[l].reshape(3, 2 * N_FT, FFN_TF).transpose(1, 0, 2)
        b_conv_t = b_conv[l].reshape(2 * N_FT, 1, FFN_TF)
        w_down_t = w_down[l].astype(BF16).reshape(N_FT, FFN_TF, D)
        x = _ffn(x, o_mla, o_dil, mod3, g_ffn_norm[l][None, :], w_o[l].astype(BF16),
                 w_up_t, w_conv_t, b_conv_t, w_down_t)
    return x
```

```python
import functools

import numpy as np
import jax
import jax.numpy as jnp
from jax import lax
from jax.experimental import pallas as pl
from jax.experimental.pallas import tpu as pltpu

F32 = jnp.float32
BF16 = jnp.bfloat16

D_MODEL = 1024
N_HEADS = 8
MLA_NOPE = 64
MLA_ROPE = 32
MLA_V = 64
MLA_Q_LORA = 512
MLA_KV_LORA = 256
DIL_HEAD_DIM = 64
DIL_WIDTH = N_HEADS * DIL_HEAD_DIM
DIL_PATTERNS = ((128, 1), (512, 4), (2048, 16))
D_FF = 2816
ROPE_THETA = 10000.0
EPS = 1e-6
MASK_BIAS = -1e30
M_INIT = -5e29
LOG2E = 1.4426950408889634

LANES = 128
MXU_DIM = 256
VMEM_LIMIT = 56 * 1024 * 1024
ADA_TN = 1024
PROJ_TM = 512
ATT_TQ = 256
ATT_TK = 256
ATT_PAIRS = 2
FFN_TM = 512
FFN_TF = 256
N_FT = D_FF // FFN_TF

C_QLAT = 0
C_KVLAT = C_QLAT + MLA_Q_LORA
C_KPE = C_KVLAT + MLA_KV_LORA
C_QD = C_KPE + LANES
C_KD = C_QD + DIL_WIDTH
C_VD = C_KD + DIL_WIDTH
IN_COLS_P = C_VD + DIL_WIDTH


def _dot(a, b):
    return jnp.dot(a, b, preferred_element_type=F32)


def _const_spec(shape):
    nd = len(shape)
    return pl.BlockSpec(shape, lambda *_: (0,) * nd, pipeline_mode=pl.Buffered(1))


def _split_bf16(v):
    hi = v.astype(BF16)
    lo = (v - hi.astype(F32)).astype(BF16)
    return hi, lo


def _ada_kernel(c_ref, w_ref, b_ref, o_ref):
    c = c_ref[...]
    a = c * (1.0 / (1.0 + jnp.exp(-c)))
    a_hi, a_lo = _split_bf16(a)
    w_hi, w_lo = _split_bf16(w_ref[...])
    o_ref[...] = _dot(a_hi, w_hi) + _dot(a_lo, w_hi) + _dot(a_hi, w_lo) + b_ref[...]


def _ada(c, w_ada, b_ada):
    B = c.shape[0]
    n = w_ada.shape[1]
    return pl.pallas_call(
        _ada_kernel,
        out_shape=jax.ShapeDtypeStruct((B, n), F32),
        grid=(n // ADA_TN,),
        in_specs=[pl.BlockSpec((B, D_MODEL), lambda j: (0, 0)),
                  pl.BlockSpec((D_MODEL, ADA_TN), lambda j: (0, j)),
                  pl.BlockSpec((1, ADA_TN), lambda j: (0, j))],
        out_specs=pl.BlockSpec((B, ADA_TN), lambda j: (0, j)),
        compiler_params=pltpu.CompilerParams(dimension_semantics=("arbitrary",),
                                             vmem_limit_bytes=VMEM_LIMIT),
        name="ada",
    )(c, w_ada, b_ada)


def _rms_rows(v, width):
    ms = jnp.sum(v * v, axis=-1, keepdims=True) * (1.0 / width)
    return v * lax.rsqrt(ms + EPS)


def _seg_rms(v, seg_ref, seg_width):
    outs = []
    for c0 in range(0, v.shape[1], MXU_DIM):
        blk = v[:, c0:c0 + MXU_DIM]
        ss = _dot((blk * blk).astype(BF16), seg_ref[...])
        outs.append(blk * lax.rsqrt(ss * (1.0 / seg_width) + EPS))
    return outs[0] if len(outs) == 1 else jnp.concatenate(outs, axis=1)


def _rope(v, cos, sin_signed):
    outs = []
    for c0 in range(0, v.shape[1], LANES):
        blk = v[:, c0:c0 + LANES]
        outs.append(blk * cos + pltpu.roll(blk, LANES // 2, 1) * sin_signed)
    return outs[0] if len(outs) == 1 else jnp.concatenate(outs, axis=1)


def _proj_kernel(x_ref, pos_ref, mod_ref, gmix_ref, win_ref, gql_ref, wqb_ref, gkvl_ref, wkvb_ref,
                 gqn_ref, gqpe_ref, gkn_ref, gkpe_ref, gdq_ref, gdk_ref,
                 seg64_ref, segpe_ref, segpair_ref, freq_ref, expand_ref,
                 qn_ref, qpe_ref, kn_ref, kpe_ref, vm_ref, qd_ref, kd_ref, vd_ref):
    x = x_ref[0]
    sh1 = mod_ref[0, 0:1, :]
    sc1 = mod_ref[0, 1:2, :]
    h = (_rms_rows(x, D_MODEL) * (gmix_ref[...] * (1.0 + sc1)) + sh1).astype(BF16)

    pos = pos_ref[0].astype(F32)
    ang = freq_ref[...] * pos
    cs = jnp.concatenate([jnp.cos(ang), jnp.sin(ang)], axis=0).T
    cs_hi, cs_lo = _split_bf16(cs)
    tabs = _dot(cs_hi, expand_ref[...]) + _dot(cs_lo, expand_ref[...])
    cos_d, sin_d = tabs[:, 0:LANES], tabs[:, LANES:2 * LANES]
    cos_p, sin_p = tabs[:, 2 * LANES:3 * LANES], tabs[:, 3 * LANES:4 * LANES]

    q_lat = _dot(h, win_ref[:, C_QLAT:C_QLAT + MLA_Q_LORA])
    q_in = (_rms_rows(q_lat, MLA_Q_LORA) * gql_ref[...]).astype(BF16)
    q = _dot(q_in, wqb_ref[...])
    n_nope = N_HEADS * MLA_NOPE
    qn_ref[0] = (_seg_rms(q[:, :n_nope], seg64_ref, MLA_NOPE) * gqn_ref[...]).astype(BF16)
    q_pe = _seg_rms(q[:, n_nope:], segpe_ref, MLA_ROPE) * gqpe_ref[...]
    qpe_ref[0] = _rope(q_pe, cos_p, sin_p).astype(BF16)

    kv_lat = _dot(h, win_ref[:, C_KVLAT:C_KVLAT + MLA_KV_LORA])
    kv_in = (_rms_rows(kv_lat, MLA_KV_LORA) * gkvl_ref[...]).astype(BF16)
    kv = _dot(kv_in, wkvb_ref[...])
    kn_ref[0] = (_seg_rms(kv[:, :n_nope], seg64_ref, MLA_NOPE) * gkn_ref[...]).astype(BF16)
    vm_ref[0] = kv[:, n_nope:].astype(BF16)
    k_pe = _dot(h, win_ref[:, C_KPE:C_KPE + LANES])
    kpe_ref[0] = _rope(_rms_rows(k_pe, LANES) * gkpe_ref[...], cos_p, sin_p).astype(BF16)

    qd = _seg_rms(_dot(h, win_ref[:, C_QD:C_QD + DIL_WIDTH]), segpair_ref, DIL_HEAD_DIM)
    qd_ref[0] = _rope(qd * gdq_ref[...], cos_d, sin_d).astype(BF16)
    kd = _seg_rms(_dot(h, win_ref[:, C_KD:C_KD + DIL_WIDTH]), segpair_ref, DIL_HEAD_DIM)
    kd_ref[0] = _rope(kd * gdk_ref[...], cos_d, sin_d).astype(BF16)
    vd_ref[0] = _dot(h, win_ref[:, C_VD:C_VD + DIL_WIDTH]).astype(BF16)


def _proj(x, pos3, mod3, consts):
    B, S, _ = x.shape
    tm = PROJ_TM
    row = lambda w: pl.BlockSpec((1, tm, w), lambda b, i: (b, i, 0))
    out_widths = (512, 256, 512, LANES, 512, 512, 512, 512)
    in_specs = [row(D_MODEL),
                pl.BlockSpec((1, 1, tm), lambda b, i: (b, 0, i)),
                pl.BlockSpec((1, 6, D_MODEL), lambda b, i: (b, 0, 0))]
    in_specs += [_const_spec(a.shape) for a in consts]
    return pl.pallas_call(
        _proj_kernel,
        out_shape=tuple(jax.ShapeDtypeStruct((B, S, w), BF16) for w in out_widths),
        grid=(B, S // tm),
        in_specs=in_specs,
        out_specs=tuple(row(w) for w in out_widths),
        compiler_params=pltpu.CompilerParams(dimension_semantics=("arbitrary", "arbitrary"),
                                             vmem_limit_bytes=VMEM_LIMIT),
        name="proj",
    )(x, pos3, mod3, *consts)


def _row_iota(shape):
    return lax.broadcasted_iota(jnp.int32, shape, 0)


def _sublane_allreduce(v, op):
    for shift in (4, 2, 1):
        v = op(v, pltpu.roll(v, shift, 0))
    return v


def _transpose_bf16(v):
    return v.astype(F32).T.astype(BF16)


class _Stream:
    def __init__(self, load_qt2, load_k, load_v, store_o):
        self.load_qt2, self.load_k, self.load_v, self.store_o = load_qt2, load_k, load_v, store_o


def _pair_attention(streams, n_rows, bias_ref, far_bias, vt_scr, qt_scr, m_scr, l_scr, acc_scr):
    tq, tk = ATT_TQ, ATT_TK
    assert tq == tk
    n_q = n_rows // tq
    hd = LANES // 2
    ids = range(len(streams))

    def put_queries(i):
        rows = pl.ds(pl.multiple_of(i * tq, tq), tq)
        for s in ids:
            qt_scr[s] = streams[s].load_qt2(rows)

    def scores(j):
        k_rows = pl.ds(pl.multiple_of(j * tk, tk), tk)
        return tuple(_dot(streams[s].load_k(k_rows), qt_scr[s]) for s in ids)

    def consume(j, sc, bias_idx):
        for s in ids:
            st = sc[s]
            if bias_idx is not None:
                st = st + bias_ref[bias_idx]
            s3 = st.reshape(tk // 8, 8, 2 * tq)
            m_old = m_scr[s]
            m_new = jnp.maximum(m_old, _sublane_allreduce(jnp.max(s3, axis=0), jnp.maximum))
            alpha = jnp.exp2(m_old - m_new)
            p3 = jnp.exp2(s3 - m_new[None])
            l_scr[s] = alpha * l_scr[s] + _sublane_allreduce(jnp.sum(p3, axis=0), jnp.add)
            m_scr[s] = m_new
            pt = p3.reshape(tk, 2 * tq).astype(BF16)
            vt = vt_scr[s, j]
            for h in range(2):
                pv = _dot(vt[h * hd:(h + 1) * hd, :], pt[:, h * tq:(h + 1) * tq])
                a3 = alpha[:, h * tq:(h + 1) * tq][None]
                acc_scr[s, h] = (acc_scr[s, h].reshape(hd // 8, 8, tq) * a3).reshape(hd, tq) + pv

    for s in ids:
        for j in range(n_rows // tk):
            vt_scr[s, j] = _transpose_bf16(streams[s].load_v(pl.ds(j * tk, tk)))
    put_queries(0)

    def q_body(i, sc):
        q_rows = pl.ds(pl.multiple_of(i * tq, tq), tq)
        m_scr[...] = jnp.full(m_scr.shape, M_INIT, F32)
        l_scr[...] = jnp.zeros(l_scr.shape, F32)
        acc_scr[...] = jnp.zeros(acc_scr.shape, F32)

        def off_diagonal(j, sc):
            nxt = scores(j + 1)
            consume(j, sc, None if far_bias is None else jnp.minimum(i - j, far_bias))
            return nxt

        sc = lax.fori_loop(0, i, off_diagonal, sc)
        put_queries(jnp.minimum(i + 1, n_q - 1))
        nxt = scores(0)
        consume(i, sc, 0)

        for s in ids:
            inv_l = 1.0 / l_scr[s]
            ot = jnp.concatenate(
                [(acc_scr[s, h].reshape(hd // 8, 8, tq) * inv_l[:, h * tq:(h + 1) * tq][None]).reshape(hd, tq)
                 for h in range(2)], axis=0)
            streams[s].store_o(q_rows, ot.T)
        return nxt

    lax.fori_loop(0, n_q, q_body, scores(0))


def _lane_block(ref, s):
    cols = slice(s * LANES, (s + 1) * LANES)
    return lambda rows: ref[0, rows, cols]


def _mla_kernel(qn_ref, qpe_ref, kn_ref, kpe_ref, v_ref, bias_ref, o_ref, *scratch):
    feat = _row_iota((2 * LANES, ATT_TQ))
    pe_head = (feat % (LANES // 2)) // (MLA_ROPE // 2)
    is_pe = feat >= LANES
    n_streams = o_ref.shape[2] // LANES
    assert n_streams == 2

    def make(s):
        mask_a = (feat < LANES // 2) | (is_pe & (pe_head == 2 * s))
        mask_b = ((feat >= LANES // 2) & (feat < LANES)) | (is_pe & (pe_head == 2 * s + 1))
        qn, kn = _lane_block(qn_ref, s), _lane_block(kn_ref, s)

        def load_qt2(rows):
            qt = jnp.concatenate([qn(rows), qpe_ref[0, rows, :]], axis=1).astype(F32).T
            z = jnp.zeros_like(qt)
            return jnp.concatenate([jnp.where(mask_a, qt, z), jnp.where(mask_b, qt, z)],
                                   axis=1).astype(BF16)

        def load_k(rows):
            return jnp.concatenate([kn(rows), kpe_ref[0, rows, :]], axis=1)

        def store_o(rows, val):
            o_ref[0, rows, s * LANES:(s + 1) * LANES] = val.astype(o_ref.dtype)

        return _Stream(load_qt2, load_k, _lane_block(v_ref, s), store_o)

    _pair_attention([make(s) for s in range(n_streams)], o_ref.shape[1], bias_ref, None, *scratch)


def _dil_kernel(q_ref, k_ref, v_ref, bias_ref, o_ref, *scratch):
    feat = _row_iota((LANES, ATT_TQ))
    is_a = (feat % (LANES // 2)) < DIL_HEAD_DIM // 2

    def make(s):
        q = _lane_block(q_ref, s)

        def load_qt2(rows):
            qt = q(rows).astype(F32).T
            z = jnp.zeros_like(qt)
            return jnp.concatenate([jnp.where(is_a, qt, z), jnp.where(is_a, z, qt)],
                                   axis=1).astype(BF16)

        def store_o(rows, val):
            o_ref[0, rows, s * LANES:(s + 1) * LANES] = val.astype(o_ref.dtype)

        return _Stream(load_qt2, _lane_block(k_ref, s), _lane_block(v_ref, s), store_o)

    _pair_attention([make(s) for s in range(o_ref.shape[2] // LANES)], o_ref.shape[1], bias_ref,
                    bias_ref.shape[0] - 1, *scratch)


def _attn_call(kernel, name, arrays, widths, qk_width, bias, B, S):
    n_s = ATT_PAIRS
    col = lambda w: pl.BlockSpec((1, S, w), lambda b, g: (b, 0, g))
    shared = pl.BlockSpec((1, S, LANES), lambda b, g: (b, 0, 0))
    in_specs = [shared if w is None else col(w) for w in widths] + [_const_spec(bias.shape)]
    return pl.pallas_call(
        kernel,
        out_shape=jax.ShapeDtypeStruct((B, S, N_HEADS * MLA_V), BF16),
        grid=(B, N_HEADS // 2 // n_s),
        in_specs=in_specs,
        out_specs=col(n_s * LANES),
        scratch_shapes=[pltpu.VMEM((n_s, S // ATT_TK, LANES, ATT_TK), BF16),
                        pltpu.VMEM((n_s, qk_width, 2 * ATT_TQ), BF16),
                        pltpu.VMEM((n_s, 8, 2 * ATT_TQ), F32),
                        pltpu.VMEM((n_s, 8, 2 * ATT_TQ), F32),
                        pltpu.VMEM((n_s, 2, LANES // 2, ATT_TQ), F32)],
        compiler_params=pltpu.CompilerParams(dimension_semantics=("arbitrary", "arbitrary"),
                                             vmem_limit_bytes=VMEM_LIMIT),
        name=name,
    )(*arrays, bias)


def _two_heads(tile_qk):
    return np.concatenate([tile_qk.T, tile_qk.T], axis=1)


def _mla_bias():
    d = np.arange(ATT_TQ)[:, None] - np.arange(ATT_TK)[None, :]
    tile = np.where(d >= 0, 0.0, MASK_BIAS).astype(np.float32)
    return jnp.asarray(_two_heads(tile)[None])


def _dil_bias():
    (far_window, far_dil), = [(w, d) for w, d in DIL_PATTERNS if w == max(p[0] for p in DIL_PATTERNS)]
    assert ATT_TK % far_dil == 0
    max_near = max(w for w, d in DIL_PATTERNS if w < far_window)
    n_tiles = -(-max_near // ATT_TK) + 2
    tiles = []
    for blk in range(n_tiles):
        delta = blk * ATT_TK + np.arange(ATT_TQ)[:, None] - np.arange(ATT_TK)[None, :]
        mult = np.zeros(delta.shape, np.int64)
        for window, dil in DIL_PATTERNS:
            mult += (delta >= 0) & (delta % dil == 0) & (delta <= window)
        tile = np.where(mult > 0, np.log2(np.maximum(mult, 1)), MASK_BIAS).astype(np.float32)
        tiles.append(_two_heads(tile))
    return jnp.asarray(np.stack(tiles))


def _shift_rows(y, prev, k):
    head = pltpu.roll(jnp.concatenate([prev, y[:8]], axis=0), k, 0)[8:]
    return jnp.concatenate([head, pltpu.roll(y, k, 0)[8:]], axis=0)


def _ffn_kernel(x_ref, om_ref, od_ref, mod_ref, gffn_ref, wo_ref, wup_ref, wconv_ref, bconv_ref,
                wdown_ref, out_ref, carry_scr, acc_scr, h2_scr):
    g1 = mod_ref[0, 2:3, :]
    sh2 = mod_ref[0, 3:4, :]
    sc2 = mod_ref[0, 4:5, :]
    g2 = mod_ref[0, 5:6, :]
    half = wo_ref.shape[0] // 2
    mix = _dot(om_ref[0], wo_ref[:half, :]) + _dot(od_ref[0], wo_ref[half:, :])
    x1 = x_ref[0] + g1 * mix
    h2_scr[...] = (_rms_rows(x1, D_MODEL) * (gffn_ref[...] * (1.0 + sc2)) + sh2).astype(BF16)
    out_ref[0] = x1
    acc_scr[...] = jnp.zeros(acc_scr.shape, F32)

    @pl.when(pl.program_id(1) == 0)
    def _():
        carry_scr[...] = jnp.zeros(carry_scr.shape, F32)

    def conv_tile(t):
        y = _dot(h2_scr[...], wup_ref[t])
        prev = carry_scr[t]
        carry_scr[t] = y[FFN_TM - 8:, :]
        w = wconv_ref[t]
        return (w[2:3, :] * y + w[1:2, :] * _shift_rows(y, prev, 1)
                + w[0:1, :] * _shift_rows(y, prev, 2) + bconv_ref[t])

    def f_body(f, c):
        gate = conv_tile(f)
        val = conv_tile(f + N_FT)
        a = (gate * (1.0 / (1.0 + jnp.exp(-gate))) * val).astype(BF16)
        acc_scr[...] += _dot(a, wdown_ref[f])
        return c

    lax.fori_loop(0, N_FT, f_body, 0)
    out_ref[0] = out_ref[0] + g2 * acc_scr[...]


def _ffn(x, o_mla, o_dil, mod3, g_ffn, w_o, w_up_t, w_conv_t, b_conv_t, w_down_t):
    B, S, _ = x.shape
    tm = FFN_TM
    row = lambda w: pl.BlockSpec((1, tm, w), lambda b, i: (b, i, 0))
    consts = (g_ffn, w_o, w_up_t, w_conv_t, b_conv_t, w_down_t)
    return pl.pallas_call(
        _ffn_kernel,
        out_shape=jax.ShapeDtypeStruct(x.shape, F32),
        grid=(B, S // tm),
        in_specs=[row(D_MODEL), row(o_mla.shape[2]), row(o_dil.shape[2]),
                  pl.BlockSpec((1, 6, D_MODEL), lambda b, i: (b, 0, 0))]
                 + [_const_spec(a.shape) for a in consts],
        out_specs=row(D_MODEL),
        scratch_shapes=[pltpu.VMEM((2 * N_FT, 8, FFN_TF), F32), pltpu.VMEM((tm, D_MODEL), F32),
                        pltpu.VMEM((tm, D_MODEL), BF16)],
        compiler_params=pltpu.CompilerParams(dimension_semantics=("arbitrary", "arbitrary"),
                                             vmem_limit_bytes=VMEM_LIMIT),
        name="ffn",
    )(x, o_mla, o_dil, mod3, *consts)


def _layout_indices():
    lane = np.arange(LANES)
    dil_head = (lane % 64) // 32
    dil_feat = (lane // 64) * 32 + lane % 32
    dil_cols = np.concatenate([(2 * g + dil_head) * DIL_HEAD_DIM + dil_feat for g in range(4)])
    pe_head = (lane % 64) // 16
    pe_feat = (lane // 64) * 16 + lane % 16
    qb_nope = np.concatenate([h * (MLA_NOPE + MLA_ROPE) + np.arange(MLA_NOPE) for h in range(N_HEADS)])
    qb_pe = np.concatenate([(4 * g + pe_head) * (MLA_NOPE + MLA_ROPE) + MLA_NOPE + pe_feat
                            for g in range(2)])
    kvb_k = np.concatenate([h * (MLA_NOPE + MLA_V) + np.arange(MLA_NOPE) for h in range(N_HEADS)])
    kvb_v = np.concatenate([h * (MLA_NOPE + MLA_V) + MLA_NOPE + np.arange(MLA_V) for h in range(N_HEADS)])
    return dict(dil_cols=dil_cols, dil_feat=dil_feat, pe_feat=pe_feat, pe_head=pe_head,
                dil_head=dil_head, qb_cols=np.concatenate([qb_nope, qb_pe]),
                kvb_cols=np.concatenate([kvb_k, kvb_v]))


def _segment_matrices(ix):
    i = np.arange(MXU_DIM)
    lane = i % LANES
    seg64 = i // 64
    segpe = (i // LANES) * 4 + ix["pe_head"][lane]
    segpair = (i // LANES) * 2 + ix["dil_head"][lane]
    same = lambda s: jnp.asarray((s[:, None] == s[None, :]).astype(np.float32), dtype=BF16)
    return same(seg64), same(segpe), same(segpair)


def _rope_constants(ix):
    def inv_freq(d):
        half = d // 2
        return jnp.power(ROPE_THETA, -2.0 * jnp.arange(half, dtype=F32) / d)
    fd, fp = inv_freq(DIL_HEAD_DIM), inv_freq(MLA_ROPE)
    nd, npe = fd.shape[0], fp.shape[0]
    freq = jnp.concatenate([fd, fp, jnp.zeros((64 - nd - npe,), F32)])[:, None]
    lane = np.arange(LANES)
    sign = np.where(lane < 64, -1.0, 1.0)
    e = np.zeros((2 * 64, 4 * LANES), np.float32)
    e[lane % 32, lane] = 1.0
    e[64 + lane % 32, LANES + lane] = sign
    e[nd + lane % 16, 2 * LANES + lane] = 1.0
    e[64 + nd + lane % 16, 3 * LANES + lane] = sign
    return freq, jnp.asarray(e, dtype=BF16)


def kernel(x, c, positions, w_ada, b_ada, g_mix_norm, w_in, g_q_lat, w_q_b, g_kv_lat, w_kv_b,
           g_mla_q_nope, g_mla_q_pe, g_mla_k_nope, g_mla_k_pe, g_dil_q, g_dil_k, w_o,
           g_ffn_norm, w_up, w_conv, b_conv, w_down):
    B, S, D = x.shape
    assert D == D_MODEL and S % PROJ_TM == 0 and S % ATT_TQ == 0 and S % FFN_TM == 0
    assert S <= max(w for w, _ in DIL_PATTERNS)
    ix = _layout_indices()
    seg64, segpe, segpair = _segment_matrices(ix)
    freq, expand = _rope_constants(ix)
    mla_scale = (MLA_NOPE + MLA_ROPE) ** -0.5 * LOG2E
    dil_scale = DIL_HEAD_DIM ** -0.5 * LOG2E
    pos3 = positions.reshape(B, 1, S)

    for l in range(w_ada.shape[0]):
        mod3 = _ada(c, w_ada[l], b_ada[l][None, :]).reshape(B, 6, D)

        wi = w_in[l]
        kpe_cols = MLA_Q_LORA + MLA_KV_LORA + ix["pe_feat"]
        qd0 = MLA_Q_LORA + MLA_KV_LORA + MLA_ROPE
        w_in_p = jnp.concatenate([
            wi[:, :MLA_Q_LORA + MLA_KV_LORA], wi[:, kpe_cols],
            wi[:, qd0 + ix["dil_cols"]], wi[:, qd0 + DIL_WIDTH + ix["dil_cols"]],
            wi[:, qd0 + 2 * DIL_WIDTH:]], axis=1).astype(BF16)
        consts = (
            g_mix_norm[l][None, :], w_in_p,
            g_q_lat[l][None, :], w_q_b[l][:, ix["qb_cols"]].astype(BF16),
            g_kv_lat[l][None, :], w_kv_b[l][:, ix["kvb_cols"]].astype(BF16),
            (jnp.tile(g_mla_q_nope[l], N_HEADS) * mla_scale)[None, :],
            (jnp.tile(g_mla_q_pe[l][ix["pe_feat"]], 2) * mla_scale)[None, :],
            jnp.tile(g_mla_k_nope[l], N_HEADS)[None, :],
            g_mla_k_pe[l][ix["pe_feat"]][None, :],
            (jnp.tile(g_dil_q[l][ix["dil_feat"]], 4) * dil_scale)[None, :],
            jnp.tile(g_dil_k[l][ix["dil_feat"]], 4)[None, :],
            seg64, segpe, segpair, freq, expand)
        qn, qpe, kn, kpe, vm, qd, kd, vd = _proj(x, pos3, mod3, consts)

        two = ATT_PAIRS * LANES
        o_mla = _attn_call(_mla_kernel, "mla", (qn, qpe, kn, kpe, vm), (two, LANES, two, None, two),
                           2 * LANES, _mla_bias(), B, S)
        o_dil = _attn_call(_dil_kernel, "dil", (qd, kd, vd), (two, two, two), LANES, _dil_bias(), B, S)

        w_up_t = w_up[l].astype(BF16).reshape(D, 2 * N_FT, FFN_TF).transpose(1, 0, 2)
        w_conv_t = w_conv[l].reshape(3, 2 * N_FT, FFN_TF).transpose(1, 0, 2)
        b_conv_t = b_conv[l].reshape(2 * N_FT, 1, FFN_TF)
        w_down_t = w_down[l].astype(BF16).reshape(N_FT, FFN_TF, D)
        x = _ffn(x, o_mla, o_dil, mod3, g_ffn_norm[l][None, :], w_o[l].astype(BF16),
                 w_up_t, w_conv_t, b_conv_t, w_down_t)
    return x
```

```python
import functools

import numpy as np
import jax
import jax.numpy as jnp
from jax import lax
from jax.experimental import pallas as pl
from jax.experimental.pallas import tpu as pltpu

F32 = jnp.float32
BF16 = jnp.bfloat16

D_MODEL = 1024
N_HEADS = 8
MLA_NOPE = 64
MLA_ROPE = 32
MLA_V = 64
MLA_Q_LORA = 512
MLA_KV_LORA = 256
DIL_HEAD_DIM = 64
DIL_WIDTH = N_HEADS * DIL_HEAD_DIM
DIL_PATTERNS = ((128, 1), (512, 4), (2048, 16))
D_FF = 2816
ROPE_THETA = 10000.0
EPS = 1e-6
MASK_BIAS = -1e30
M_INIT = -5e29
LOG2E = 1.4426950408889634

LANES = 128
MXU_DIM = 256
VMEM_LIMIT = 56 * 1024 * 1024
ADA_TN = 1024
PROJ_TM = 512
ATT_TQ = 256
ATT_TK = 256
ATT_PAIRS = 2
FFN_TM = 512
FFN_TF = 256
N_FT = D_FF // FFN_TF

C_QLAT = 0
C_KVLAT = C_QLAT + MLA_Q_LORA
C_KPE = C_KVLAT + MLA_KV_LORA
C_QD = C_KPE + LANES
C_KD = C_QD + DIL_WIDTH
C_VD = C_KD + DIL_WIDTH
IN_COLS_P = C_VD + DIL_WIDTH


def _dot(a, b):
    return jnp.dot(a, b, preferred_element_type=F32)


def _const_spec(shape):
    nd = len(shape)
    return pl.BlockSpec(shape, lambda *_: (0,) * nd, pipeline_mode=pl.Buffered(1))


def _split_bf16(v):
    hi = v.astype(BF16)
    lo = (v - hi.astype(F32)).astype(BF16)
    return hi, lo


def _ada_kernel(c_ref, w_ref, b_ref, o_ref):
    c = c_ref[...]
    a = c * (1.0 / (1.0 + jnp.exp(-c)))
    a_hi, a_lo = _split_bf16(a)
    w_hi, w_lo = _split_bf16(w_ref[...])
    o_ref[...] = _dot(a_hi, w_hi) + _dot(a_lo, w_hi) + _dot(a_hi, w_lo) + b_ref[...]


def _ada(c, w_ada, b_ada):
    B = c.shape[0]
    n = w_ada.shape[1]
    return pl.pallas_call(
        _ada_kernel,
        out_shape=jax.ShapeDtypeStruct((B, n), F32),
        grid=(n // ADA_TN,),
        in_specs=[pl.BlockSpec((B, D_MODEL), lambda j: (0, 0)),
                  pl.BlockSpec((D_MODEL, ADA_TN), lambda j: (0, j)),
                  pl.BlockSpec((1, ADA_TN), lambda j: (0, j))],
        out_specs=pl.BlockSpec((B, ADA_TN), lambda j: (0, j)),
        compiler_params=pltpu.CompilerParams(dimension_semantics=("arbitrary",),
                                             vmem_limit_bytes=VMEM_LIMIT),
        name="ada",
    )(c, w_ada, b_ada)


def _rms_rows(v, width):
    ms = jnp.sum(v * v, axis=-1, keepdims=True) * (1.0 / width)
    return v * lax.rsqrt(ms + EPS)


def _seg_rms(v, seg_ref, seg_width):
    outs = []
    for c0 in range(0, v.shape[1], MXU_DIM):
        blk = v[:, c0:c0 + MXU_DIM]
        ss = _dot((blk * blk).astype(BF16), seg_ref[...])
        outs.append(blk * lax.rsqrt(ss * (1.0 / seg_width) + EPS))
    return outs[0] if len(outs) == 1 else jnp.concatenate(outs, axis=1)


def _rope(v, cos, sin_signed):
    outs = []
    for c0 in range(0, v.shape[1], LANES):
        blk = v[:, c0:c0 + LANES]
        outs.append(blk * cos + pltpu.roll(blk, LANES // 2, 1) * sin_signed)
    return outs[0] if len(outs) == 1 else jnp.concatenate(outs, axis=1)


def _proj_kernel(x_ref, pos_ref, mod_ref, gmix_ref, win_ref, gql_ref, wqb_ref, gkvl_ref, wkvb_ref,
                 gqn_ref, gqpe_ref, gkn_ref, gkpe_ref, gdq_ref, gdk_ref,
                 seg64_ref, segpe_ref, segpair_ref, freq_ref, expand_ref,
                 qn_ref, qpe_ref, kn_ref, kpe_ref, vm_ref, qd_ref, kd_ref, vd_ref):
    x = x_ref[0]
    sh1 = mod_ref[0, 0:1, :]
    sc1 = mod_ref[0, 1:2, :]
    h = (_rms_rows(x, D_MODEL) * (gmix_ref[...] * (1.0 + sc1)) + sh1).astype(BF16)

    pos = pos_ref[0].astype(F32)
    ang = freq_ref[...] * pos
    cs = jnp.concatenate([jnp.cos(ang), jnp.sin(ang)], axis=0).T
    cs_hi, cs_lo = _split_bf16(cs)
    tabs = _dot(cs_hi, expand_ref[...]) + _dot(cs_lo, expand_ref[...])
    cos_d, sin_d = tabs[:, 0:LANES], tabs[:, LANES:2 * LANES]
    cos_p, sin_p = tabs[:, 2 * LANES:3 * LANES], tabs[:, 3 * LANES:4 * LANES]

    q_lat = _dot(h, win_ref[:, C_QLAT:C_QLAT + MLA_Q_LORA])
    q_in = (_rms_rows(q_lat, MLA_Q_LORA) * gql_ref[...]).astype(BF16)
    q = _dot(q_in, wqb_ref[...])
    n_nope = N_HEADS * MLA_NOPE
    qn_ref[0] = (_seg_rms(q[:, :n_nope], seg64_ref, MLA_NOPE) * gqn_ref[...]).astype(BF16)
    q_pe = _seg_rms(q[:, n_nope:], segpe_ref, MLA_ROPE) * gqpe_ref[...]
    qpe_ref[0] = _rope(q_pe, cos_p, sin_p).astype(BF16)

    kv_lat = _dot(h, win_ref[:, C_KVLAT:C_KVLAT + MLA_KV_LORA])
    kv_in = (_rms_rows(kv_lat, MLA_KV_LORA) * gkvl_ref[...]).astype(BF16)
    kv = _dot(kv_in, wkvb_ref[...])
    kn_ref[0] = (_seg_rms(kv[:, :n_nope], seg64_ref, MLA_NOPE) * gkn_ref[...]).astype(BF16)
    vm_ref[0] = kv[:, n_nope:].astype(BF16)
    k_pe = _dot(h, win_ref[:, C_KPE:C_KPE + LANES])
    kpe_ref[0] = _rope(_rms_rows(k_pe, LANES) * gkpe_ref[...], cos_p, sin_p).astype(BF16)

    qd = _seg_rms(_dot(h, win_ref[:, C_QD:C_QD + DIL_WIDTH]), segpair_ref, DIL_HEAD_DIM)
    qd_ref[0] = _rope(qd * gdq_ref[...], cos_d, sin_d).astype(BF16)
    kd = _seg_rms(_dot(h, win_ref[:, C_KD:C_KD + DIL_WIDTH]), segpair_ref, DIL_HEAD_DIM)
    kd_ref[0] = _rope(kd * gdk_ref[...], cos_d, sin_d).astype(BF16)
    vd_ref[0] = _dot(h, win_ref[:, C_VD:C_VD + DIL_WIDTH]).astype(BF16)


def _proj(x, pos3, mod3, consts):
    B, S, _ = x.shape
    tm = PROJ_TM
    row = lambda w: pl.BlockSpec((1, tm, w), lambda b, i: (b, i, 0))
    out_widths = (512, 256, 512, LANES, 512, 512, 512, 512)
    in_specs = [row(D_MODEL),
                pl.BlockSpec((1, 1, tm), lambda b, i: (b, 0, i)),
                pl.BlockSpec((1, 6, D_MODEL), lambda b, i: (b, 0, 0))]
    in_specs += [_const_spec(a.shape) for a in consts]
    return pl.pallas_call(
        _proj_kernel,
        out_shape=tuple(jax.ShapeDtypeStruct((B, S, w), BF16) for w in out_widths),
        grid=(B, S // tm),
        in_specs=in_specs,
        out_specs=tuple(row(w) for w in out_widths),
        compiler_params=pltpu.CompilerParams(dimension_semantics=("arbitrary", "arbitrary"),
                                             vmem_limit_bytes=VMEM_LIMIT),
        name="proj",
    )(x, pos3, mod3, *consts)


def _row_iota(shape):
    return lax.broadcasted_iota(jnp.int32, shape, 0)


def _sublane_allreduce(v, op):
    for shift in (4, 2, 1):
        v = op(v, pltpu.roll(v, shift, 0))
    return v


def _transpose_bf16(v):
    return v.astype(F32).T.astype(BF16)


class _Stream:
    def __init__(self, load_qt2, load_k, load_v, store_o):
        self.load_qt2, self.load_k, self.load_v, self.store_o = load_qt2, load_k, load_v, store_o


def _pair_attention(streams, n_rows, bias_ref, far_bias, vt_scr, qt_scr, s_scr, m_scr, l_scr, acc_scr):
    tq, tk = ATT_TQ, ATT_TK
    assert tq == tk
    n_q = n_rows // tq
    hd = LANES // 2
    ids = range(len(streams))

    def put_queries(i):
        rows = pl.ds(pl.multiple_of(i * tq, tq), tq)
        for s in ids:
            qt_scr[s] = streams[s].load_qt2(rows)

    def put_scores(j, slot):
        k_rows = pl.ds(pl.multiple_of(j * tk, tk), tk)
        for s in ids:
            s_scr[s, slot] = _dot(streams[s].load_k(k_rows), qt_scr[s])

    def consume(j, slot, bias_idx):
        for s in ids:
            st = s_scr[s, slot]
            if bias_idx is not None:
                st = st + bias_ref[bias_idx]
            s3 = st.reshape(tk // 8, 8, 2 * tq)
            m_old = m_scr[s]
            m_new = jnp.maximum(m_old, _sublane_allreduce(jnp.max(s3, axis=0), jnp.maximum))
            alpha = jnp.exp2(m_old - m_new)
            p3 = jnp.exp2(s3 - m_new[None])
            l_scr[s] = alpha * l_scr[s] + _sublane_allreduce(jnp.sum(p3, axis=0), jnp.add)
            m_scr[s] = m_new
            pt = p3.reshape(tk, 2 * tq).astype(BF16)
            vt = vt_scr[s, j]
            for h in range(2):
                pv = _dot(vt[h * hd:(h + 1) * hd, :], pt[:, h * tq:(h + 1) * tq])
                a3 = alpha[:, h * tq:(h + 1) * tq][None]
                acc_scr[s, h] = (acc_scr[s, h].reshape(hd // 8, 8, tq) * a3).reshape(hd, tq) + pv

    def step(j, slot, bias_idx):
        put_scores(j + 1, 1 - slot)
        consume(j, slot, bias_idx)

    for s in ids:
        for j in range(n_rows // tk):
            vt_scr[s, j] = _transpose_bf16(streams[s].load_v(pl.ds(j * tk, tk)))
    put_queries(0)
    put_scores(0, 0)

    first_slot = (0, 1, 1, 0)
    assert n_q % len(first_slot) == 0

    def query_block(i, r):
        q_rows = pl.ds(pl.multiple_of(i * tq, tq), tq)
        m_scr[...] = jnp.full(m_scr.shape, M_INIT, F32)
        l_scr[...] = jnp.zeros(l_scr.shape, F32)
        acc_scr[...] = jnp.zeros(acc_scr.shape, F32)
        first = first_slot[r]

        def bias_of(d):
            return None if far_bias is None else jnp.minimum(d, far_bias)

        def key_pair(jj, c):
            j = 2 * jj
            step(j, first, bias_of(i - j))
            step(j + 1, 1 - first, bias_of(i - j - 1))
            return c

        lax.fori_loop(0, i // 2, key_pair, 0)
        cur = first
        if r % 2 == 1:
            step(i - 1, cur, None if far_bias is None else min(1, far_bias))
            cur = 1 - cur
        put_queries(jnp.minimum(i + 1, n_q - 1))
        put_scores(0, 1 - cur)
        consume(i, cur, 0)

        for s in ids:
            inv_l = 1.0 / l_scr[s]
            ot = jnp.concatenate(
                [(acc_scr[s, h].reshape(hd // 8, 8, tq) * inv_l[:, h * tq:(h + 1) * tq][None]).reshape(hd, tq)
                 for h in range(2)], axis=0)
            streams[s].store_o(q_rows, ot.T)

    def four_blocks(ii, c):
        for r in range(len(first_slot)):
            query_block(len(first_slot) * ii + r, r)
        return c

    lax.fori_loop(0, n_q // len(first_slot), four_blocks, 0)


def _lane_block(ref, s):
    cols = slice(s * LANES, (s + 1) * LANES)
    return lambda rows: ref[0, rows, cols]


def _mla_kernel(qn_ref, qpe_ref, kn_ref, kpe_ref, v_ref, bias_ref, o_ref, *scratch):
    feat = _row_iota((2 * LANES, ATT_TQ))
    pe_head = (feat % (LANES // 2)) // (MLA_ROPE // 2)
    is_pe = feat >= LANES
    n_streams = o_ref.shape[2] // LANES
    assert n_streams == 2

    def make(s):
        mask_a = (feat < LANES // 2) | (is_pe & (pe_head == 2 * s))
        mask_b = ((feat >= LANES // 2) & (feat < LANES)) | (is_pe & (pe_head == 2 * s + 1))
        qn, kn = _lane_block(qn_ref, s), _lane_block(kn_ref, s)

        def load_qt2(rows):
            qt = jnp.concatenate([qn(rows), qpe_ref[0, rows, :]], axis=1).astype(F32).T
            z = jnp.zeros_like(qt)
            return jnp.concatenate([jnp.where(mask_a, qt, z), jnp.where(mask_b, qt, z)],
                                   axis=1).astype(BF16)

        def load_k(rows):
            return jnp.concatenate([kn(rows), kpe_ref[0, rows, :]], axis=1)

        def store_o(rows, val):
            o_ref[0, rows, s * LANES:(s + 1) * LANES] = val.astype(o_ref.dtype)

        return _Stream(load_qt2, load_k, _lane_block(v_ref, s), store_o)

    _pair_attention([make(s) for s in range(n_streams)], o_ref.shape[1], bias_ref, None, *scratch)


def _dil_kernel(q_ref, k_ref, v_ref, bias_ref, o_ref, *scratch):
    feat = _row_iota((LANES, ATT_TQ))
    is_a = (feat % (LANES // 2)) < DIL_HEAD_DIM // 2

    def make(s):
        q = _lane_block(q_ref, s)

        def load_qt2(rows):
            qt = q(rows).astype(F32).T
            z = jnp.zeros_like(qt)
            return jnp.concatenate([jnp.where(is_a, qt, z), jnp.where(is_a, z, qt)],
                                   axis=1).astype(BF16)

        def store_o(rows, val):
            o_ref[0, rows, s * LANES:(s + 1) * LANES] = val.astype(o_ref.dtype)

        return _Stream(load_qt2, _lane_block(k_ref, s), _lane_block(v_ref, s), store_o)

    _pair_attention([make(s) for s in range(o_ref.shape[2] // LANES)], o_ref.shape[1], bias_ref,
                    bias_ref.shape[0] - 1, *scratch)


def _attn_call(kernel, name, arrays, widths, qk_width, bias, B, S):
    n_s = ATT_PAIRS
    col = lambda w: pl.BlockSpec((1, S, w), lambda b, g: (b, 0, g))
    shared = pl.BlockSpec((1, S, LANES), lambda b, g: (b, 0, 0))
    in_specs = [shared if w is None else col(w) for w in widths] + [_const_spec(bias.shape)]
    return pl.pallas_call(
        kernel,
        out_shape=jax.ShapeDtypeStruct((B, S, N_HEADS * MLA_V), BF16),
        grid=(B, N_HEADS // 2 // n_s),
        in_specs=in_specs,
        out_specs=col(n_s * LANES),
        scratch_shapes=[pltpu.VMEM((n_s, S // ATT_TK, LANES, ATT_TK), BF16),
                        pltpu.VMEM((n_s, qk_width, 2 * ATT_TQ), BF16),
                        pltpu.VMEM((n_s, 2, ATT_TK, 2 * ATT_TQ), F32),
                        pltpu.VMEM((n_s, 8, 2 * ATT_TQ), F32),
                        pltpu.VMEM((n_s, 8, 2 * ATT_TQ), F32),
                        pltpu.VMEM((n_s, 2, LANES // 2, ATT_TQ), F32)],
        compiler_params=pltpu.CompilerParams(dimension_semantics=("arbitrary", "arbitrary"),
                                             vmem_limit_bytes=VMEM_LIMIT),
        name=name,
    )(*arrays, bias)


def _two_heads(tile_qk):
    return np.concatenate([tile_qk.T, tile_qk.T], axis=1)


def _mla_bias():
    d = np.arange(ATT_TQ)[:, None] - np.arange(ATT_TK)[None, :]
    tile = np.where(d >= 0, 0.0, MASK_BIAS).astype(np.float32)
    return jnp.asarray(_two_heads(tile)[None])


def _dil_bias():
    (far_window, far_dil), = [(w, d) for w, d in DIL_PATTERNS if w == max(p[0] for p in DIL_PATTERNS)]
    assert ATT_TK % far_dil == 0
    max_near = max(w for w, d in DIL_PATTERNS if w < far_window)
    n_tiles = -(-max_near // ATT_TK) + 2
    tiles = []
    for blk in range(n_tiles):
        delta = blk * ATT_TK + np.arange(ATT_TQ)[:, None] - np.arange(ATT_TK)[None, :]
        mult = np.zeros(delta.shape, np.int64)
        for window, dil in DIL_PATTERNS:
            mult += (delta >= 0) & (delta % dil == 0) & (delta <= window)
        tile = np.where(mult > 0, np.log2(np.maximum(mult, 1)), MASK_BIAS).astype(np.float32)
        tiles.append(_two_heads(tile))
    return jnp.asarray(np.stack(tiles))


def _ffn_kernel(x_ref, om_ref, od_ref, mod_ref, gffn_ref, wo_ref, wup_ref, wconv_ref, bconv_ref,
                wdown_ref, out_ref, carry_scr, acc_scr, h2_scr, y_scr):
    g1 = mod_ref[0, 2:3, :]
    sh2 = mod_ref[0, 3:4, :]
    sc2 = mod_ref[0, 4:5, :]
    g2 = mod_ref[0, 5:6, :]
    half = wo_ref.shape[0] // 2
    mix = _dot(om_ref[0], wo_ref[:half, :]) + _dot(od_ref[0], wo_ref[half:, :])
    x1 = x_ref[0] + g1 * mix
    h2_scr[...] = (_rms_rows(x1, D_MODEL) * (gffn_ref[...] * (1.0 + sc2)) + sh2).astype(BF16)
    out_ref[0] = x1

    @pl.when(pl.program_id(1) == 0)
    def _():
        carry_scr[...] = jnp.zeros(carry_scr.shape, F32)

    def up(f):
        for half_idx, t in enumerate((f, f + N_FT)):
            y = _dot(h2_scr[...], wup_ref[:, t * FFN_TF:(t + 1) * FFN_TF])
            y_scr[f % 2, half_idx, 0:8, :] = carry_scr[t]
            y_scr[f % 2, half_idx, 8:, :] = y
            carry_scr[t] = y[FFN_TM - 8:, :]

    def conv(f, half_idx):
        t = f + half_idx * N_FT
        yb = y_scr.at[f % 2, half_idx]
        cols = slice(t * FFN_TF, (t + 1) * FFN_TF)
        w = wconv_ref[:, cols]
        return (w[2:3, :] * yb[8:8 + FFN_TM, :] + w[1:2, :] * yb[7:7 + FFN_TM, :]
                + w[0:1, :] * yb[6:6 + FFN_TM, :] + bconv_ref[:, cols])

    up(0)
    for f in range(N_FT):
        if f + 1 < N_FT:
            up(f + 1)
        gate = conv(f, 0)
        val = conv(f, 1)
        a = (gate * (1.0 / (1.0 + jnp.exp(-gate))) * val).astype(BF16)
        d = _dot(a, wdown_ref[f * FFN_TF:(f + 1) * FFN_TF, :])
        if f == 0:
            acc_scr[...] = d
        else:
            acc_scr[...] += d
    out_ref[0] = out_ref[0] + g2 * acc_scr[...]


def _ffn(x, o_mla, o_dil, mod3, g_ffn, w_o, w_up, w_conv, b_conv, w_down):
    B, S, _ = x.shape
    tm = FFN_TM
    row = lambda w: pl.BlockSpec((1, tm, w), lambda b, i: (b, i, 0))
    consts = (g_ffn, w_o, w_up, w_conv, b_conv, w_down)
    return pl.pallas_call(
        _ffn_kernel,
        out_shape=jax.ShapeDtypeStruct(x.shape, F32),
        grid=(B, S // tm),
        in_specs=[row(D_MODEL), row(o_mla.shape[2]), row(o_dil.shape[2]),
                  pl.BlockSpec((1, 6, D_MODEL), lambda b, i: (b, 0, 0))]
                 + [_const_spec(a.shape) for a in consts],
        out_specs=row(D_MODEL),
        scratch_shapes=[pltpu.VMEM((2 * N_FT, 8, FFN_TF), F32), pltpu.VMEM((tm, D_MODEL), F32),
                        pltpu.VMEM((tm, D_MODEL), BF16),
                        pltpu.VMEM((2, 2, tm + 8, FFN_TF), F32)],
        compiler_params=pltpu.CompilerParams(dimension_semantics=("arbitrary", "arbitrary"),
                                             vmem_limit_bytes=VMEM_LIMIT),
        name="ffn",
    )(x, o_mla, o_dil, mod3, *consts)


def _layout_indices():
    lane = np.arange(LANES)
    dil_head = (lane % 64) // 32
    dil_feat = (lane // 64) * 32 + lane % 32
    dil_cols = np.concatenate([(2 * g + dil_head) * DIL_HEAD_DIM + dil_feat for g in range(4)])
    pe_head = (lane % 64) // 16
    pe_feat = (lane // 64) * 16 + lane % 16
    qb_nope = np.concatenate([h * (MLA_NOPE + MLA_ROPE) + np.arange(MLA_NOPE) for h in range(N_HEADS)])
    qb_pe = np.concatenate([(4 * g + pe_head) * (MLA_NOPE + MLA_ROPE) + MLA_NOPE + pe_feat
                            for g in range(2)])
    kvb_k = np.concatenate([h * (MLA_NOPE + MLA_V) + np.arange(MLA_NOPE) for h in range(N_HEADS)])
    kvb_v = np.concatenate([h * (MLA_NOPE + MLA_V) + MLA_NOPE + np.arange(MLA_V) for h in range(N_HEADS)])
    return dict(dil_cols=dil_cols, dil_feat=dil_feat, pe_feat=pe_feat, pe_head=pe_head,
                dil_head=dil_head, qb_cols=np.concatenate([qb_nope, qb_pe]),
                kvb_cols=np.concatenate([kvb_k, kvb_v]))


def _segment_matrices(ix):
    i = np.arange(MXU_DIM)
    lane = i % LANES
    seg64 = i // 64
    segpe = (i // LANES) * 4 + ix["pe_head"][lane]
    segpair = (i // LANES) * 2 + ix["dil_head"][lane]
    same = lambda s: jnp.asarray((s[:, None] == s[None, :]).astype(np.float32), dtype=BF16)
    return same(seg64), same(segpe), same(segpair)


def _rope_constants(ix):
    def inv_freq(d):
        half = d // 2
        return jnp.power(ROPE_THETA, -2.0 * jnp.arange(half, dtype=F32) / d)
    fd, fp = inv_freq(DIL_HEAD_DIM), inv_freq(MLA_ROPE)
    nd, npe = fd.shape[0], fp.shape[0]
    freq = jnp.concatenate([fd, fp, jnp.zeros((64 - nd - npe,), F32)])[:, None]
    lane = np.arange(LANES)
    sign = np.where(lane < 64, -1.0, 1.0)
    e = np.zeros((2 * 64, 4 * LANES), np.float32)
    e[lane % 32, lane] = 1.0
    e[64 + lane % 32, LANES + lane] = sign
    e[nd + lane % 16, 2 * LANES + lane] = 1.0
    e[64 + nd + lane % 16, 3 * LANES + lane] = sign
    return freq, jnp.asarray(e, dtype=BF16)


def kernel(x, c, positions, w_ada, b_ada, g_mix_norm, w_in, g_q_lat, w_q_b, g_kv_lat, w_kv_b,
           g_mla_q_nope, g_mla_q_pe, g_mla_k_nope, g_mla_k_pe, g_dil_q, g_dil_k, w_o,
           g_ffn_norm, w_up, w_conv, b_conv, w_down):
    B, S, D = x.shape
    assert D == D_MODEL and S % PROJ_TM == 0 and S % ATT_TQ == 0 and S % FFN_TM == 0
    assert S <= max(w for w, _ in DIL_PATTERNS)
    ix = _layout_indices()
    seg64, segpe, segpair = _segment_matrices(ix)
    freq, expand = _rope_constants(ix)
    mla_scale = (MLA_NOPE + MLA_ROPE) ** -0.5 * LOG2E
    dil_scale = DIL_HEAD_DIM ** -0.5 * LOG2E
    pos3 = positions.reshape(B, 1, S)

    for l in range(w_ada.shape[0]):
        mod3 = _ada(c, w_ada[l], b_ada[l][None, :]).reshape(B, 6, D)

        wi = w_in[l]
        kpe_cols = MLA_Q_LORA + MLA_KV_LORA + ix["pe_feat"]
        qd0 = MLA_Q_LORA + MLA_KV_LORA + MLA_ROPE
        w_in_p = jnp.concatenate([
            wi[:, :MLA_Q_LORA + MLA_KV_LORA], wi[:, kpe_cols],
            wi[:, qd0 + ix["dil_cols"]], wi[:, qd0 + DIL_WIDTH + ix["dil_cols"]],
            wi[:, qd0 + 2 * DIL_WIDTH:]], axis=1).astype(BF16)
        consts = (
            g_mix_norm[l][None, :], w_in_p,
            g_q_lat[l][None, :], w_q_b[l][:, ix["qb_cols"]].astype(BF16),
            g_kv_lat[l][None, :], w_kv_b[l][:, ix["kvb_cols"]].astype(BF16),
            (jnp.tile(g_mla_q_nope[l], N_HEADS) * mla_scale)[None, :],
            (jnp.tile(g_mla_q_pe[l][ix["pe_feat"]], 2) * mla_scale)[None, :],
            jnp.tile(g_mla_k_nope[l], N_HEADS)[None, :],
            g_mla_k_pe[l][ix["pe_feat"]][None, :],
            (jnp.tile(g_dil_q[l][ix["dil_feat"]], 4) * dil_scale)[None, :],
            jnp.tile(g_dil_k[l][ix["dil_feat"]], 4)[None, :],
            seg64, segpe, segpair, freq, expand)
        qn, qpe, kn, kpe, vm, qd, kd, vd = _proj(x, pos3, mod3, consts)

        two = ATT_PAIRS * LANES
        o_mla = _attn_call(_mla_kernel, "mla", (qn, qpe, kn, kpe, vm), (two, LANES, two, None, two),
                           2 * LANES, _mla_bias(), B, S)
        o_dil = _attn_call(_dil_kernel, "dil", (qd, kd, vd), (two, two, two), LANES, _dil_bias(), B, S)

        x = _ffn(x, o_mla, o_dil, mod3, g_ffn_norm[l][None, :], w_o[l].astype(BF16),
                 w_up[l].astype(BF16), w_conv[l], b_conv[l][None, :], w_down[l].astype(BF16))
    return x
```

```python
import functools

import numpy as np
import jax
import jax.numpy as jnp
from jax import lax
from jax.experimental import pallas as pl
from jax.experimental.pallas import tpu as pltpu

F32 = jnp.float32
BF16 = jnp.bfloat16

D_MODEL = 1024
N_HEADS = 8
MLA_NOPE = 64
MLA_ROPE = 32
MLA_V = 64
MLA_Q_LORA = 512
MLA_KV_LORA = 256
DIL_HEAD_DIM = 64
DIL_WIDTH = N_HEADS * DIL_HEAD_DIM
DIL_PATTERNS = ((128, 1), (512, 4), (2048, 16))
D_FF = 2816
ROPE_THETA = 10000.0
EPS = 1e-6
MASK_BIAS = -1e30
LOG2E = 1.4426950408889634

LANES = 128
MXU_DIM = 256
VMEM_LIMIT = 56 * 1024 * 1024
ADA_TN = 1024
PROJ_TM = 512
ATT_TQ = 256
ATT_TK = 256
ATT_PAIRS = 2
FFN_TM = 512
FFN_TF = 256
N_FT = D_FF // FFN_TF

C_QLAT = 0
C_KVLAT = C_QLAT + MLA_Q_LORA
C_KPE = C_KVLAT + MLA_KV_LORA
C_QD = C_KPE + LANES
C_KD = C_QD + DIL_WIDTH
C_VD = C_KD + DIL_WIDTH
IN_COLS_P = C_VD + DIL_WIDTH


def _dot(a, b):
    return jnp.dot(a, b, preferred_element_type=F32)


def _const_spec(shape):
    nd = len(shape)
    return pl.BlockSpec(shape, lambda *_: (0,) * nd, pipeline_mode=pl.Buffered(1))


def _split_bf16(v):
    hi = v.astype(BF16)
    lo = (v - hi.astype(F32)).astype(BF16)
    return hi, lo


def _ada_kernel(c_ref, w_ref, b_ref, o_ref):
    c = c_ref[...]
    a = c * (1.0 / (1.0 + jnp.exp(-c)))
    a_hi, a_lo = _split_bf16(a)
    w_hi, w_lo = _split_bf16(w_ref[...])
    o_ref[...] = _dot(a_hi, w_hi) + _dot(a_lo, w_hi) + _dot(a_hi, w_lo) + b_ref[...]


def _ada(c, w_ada, b_ada):
    B = c.shape[0]
    n = w_ada.shape[1]
    return pl.pallas_call(
        _ada_kernel,
        out_shape=jax.ShapeDtypeStruct((B, n), F32),
        grid=(n // ADA_TN,),
        in_specs=[pl.BlockSpec((B, D_MODEL), lambda j: (0, 0)),
                  pl.BlockSpec((D_MODEL, ADA_TN), lambda j: (0, j)),
                  pl.BlockSpec((1, ADA_TN), lambda j: (0, j))],
        out_specs=pl.BlockSpec((B, ADA_TN), lambda j: (0, j)),
        compiler_params=pltpu.CompilerParams(dimension_semantics=("arbitrary",),
                                             vmem_limit_bytes=VMEM_LIMIT),
        name="ada",
    )(c, w_ada, b_ada)


def _rms_rows(v, width):
    ms = jnp.sum(v * v, axis=-1, keepdims=True) * (1.0 / width)
    return v * lax.rsqrt(ms + EPS)


def _seg_rms(v, seg_ref, seg_width):
    outs = []
    for c0 in range(0, v.shape[1], MXU_DIM):
        blk = v[:, c0:c0 + MXU_DIM]
        ss = _dot((blk * blk).astype(BF16), seg_ref[...])
        outs.append(blk * lax.rsqrt(ss * (1.0 / seg_width) + EPS))
    return outs[0] if len(outs) == 1 else jnp.concatenate(outs, axis=1)


def _rope(v, cos, sin_signed):
    outs = []
    for c0 in range(0, v.shape[1], LANES):
        blk = v[:, c0:c0 + LANES]
        outs.append(blk * cos + pltpu.roll(blk, LANES // 2, 1) * sin_signed)
    return outs[0] if len(outs) == 1 else jnp.concatenate(outs, axis=1)


def _proj_kernel(x_ref, pos_ref, mod_ref, gmix_ref, win_ref, gql_ref, wqb_ref, gkvl_ref, wkvb_ref,
                 gqn_ref, gqpe_ref, gkn_ref, gkpe_ref, gdq_ref, gdk_ref,
                 seg64_ref, segpe_ref, segpair_ref, freq_ref, expand_ref,
                 qn_ref, qpe_ref, kn_ref, kpe_ref, vm_ref, qd_ref, kd_ref, vd_ref):
    x = x_ref[0]
    sh1 = mod_ref[0, 0:1, :]
    sc1 = mod_ref[0, 1:2, :]
    h = (_rms_rows(x, D_MODEL) * (gmix_ref[...] * (1.0 + sc1)) + sh1).astype(BF16)

    pos = pos_ref[0].astype(F32)
    ang = freq_ref[...] * pos
    cs = jnp.concatenate([jnp.cos(ang), jnp.sin(ang)], axis=0).T
    cs_hi, cs_lo = _split_bf16(cs)
    tabs = _dot(cs_hi, expand_ref[...]) + _dot(cs_lo, expand_ref[...])
    cos_d, sin_d = tabs[:, 0:LANES], tabs[:, LANES:2 * LANES]
    cos_p, sin_p = tabs[:, 2 * LANES:3 * LANES], tabs[:, 3 * LANES:4 * LANES]

    q_lat = _dot(h, win_ref[:, C_QLAT:C_QLAT + MLA_Q_LORA])
    q_in = (_rms_rows(q_lat, MLA_Q_LORA) * gql_ref[...]).astype(BF16)
    q = _dot(q_in, wqb_ref[...])
    n_nope = N_HEADS * MLA_NOPE
    qn_ref[0] = (_seg_rms(q[:, :n_nope], seg64_ref, MLA_NOPE) * gqn_ref[...]).astype(BF16)
    q_pe = _seg_rms(q[:, n_nope:], segpe_ref, MLA_ROPE) * gqpe_ref[...]
    qpe_ref[0] = _rope(q_pe, cos_p, sin_p).astype(BF16)

    kv_lat = _dot(h, win_ref[:, C_KVLAT:C_KVLAT + MLA_KV_LORA])
    kv_in = (_rms_rows(kv_lat, MLA_KV_LORA) * gkvl_ref[...]).astype(BF16)
    kv = _dot(kv_in, wkvb_ref[...])
    kn_ref[0] = (_seg_rms(kv[:, :n_nope], seg64_ref, MLA_NOPE) * gkn_ref[...]).astype(BF16)
    vm_ref[0] = kv[:, n_nope:].astype(BF16)
    k_pe = _dot(h, win_ref[:, C_KPE:C_KPE + LANES])
    kpe_ref[0] = _rope(_rms_rows(k_pe, LANES) * gkpe_ref[...], cos_p, sin_p).astype(BF16)

    qd = _seg_rms(_dot(h, win_ref[:, C_QD:C_QD + DIL_WIDTH]), segpair_ref, DIL_HEAD_DIM)
    qd_ref[0] = _rope(qd * gdq_ref[...], cos_d, sin_d).astype(BF16)
    kd = _seg_rms(_dot(h, win_ref[:, C_KD:C_KD + DIL_WIDTH]), segpair_ref, DIL_HEAD_DIM)
    kd_ref[0] = _rope(kd * gdk_ref[...], cos_d, sin_d).astype(BF16)
    vd_ref[0] = _dot(h, win_ref[:, C_VD:C_VD + DIL_WIDTH]).astype(BF16)


def _proj(x, pos3, mod3, consts):
    B, S, _ = x.shape
    tm = PROJ_TM
    row = lambda w: pl.BlockSpec((1, tm, w), lambda b, i: (b, i, 0))
    out_widths = (512, 256, 512, LANES, 512, 512, 512, 512)
    in_specs = [row(D_MODEL),
                pl.BlockSpec((1, 1, tm), lambda b, i: (b, 0, i)),
                pl.BlockSpec((1, 6, D_MODEL), lambda b, i: (b, 0, 0))]
    in_specs += [_const_spec(a.shape) for a in consts]
    return pl.pallas_call(
        _proj_kernel,
        out_shape=tuple(jax.ShapeDtypeStruct((B, S, w), BF16) for w in out_widths),
        grid=(B, S // tm),
        in_specs=in_specs,
        out_specs=tuple(row(w) for w in out_widths),
        compiler_params=pltpu.CompilerParams(dimension_semantics=("arbitrary", "arbitrary"),
                                             vmem_limit_bytes=VMEM_LIMIT),
        name="proj",
    )(x, pos3, mod3, *consts)


def _row_iota(shape):
    return lax.broadcasted_iota(jnp.int32, shape, 0)


def _sublane_allreduce(v, op):
    for shift in (4, 2, 1):
        v = op(v, pltpu.roll(v, shift, 0))
    return v


def _transpose_bf16(v):
    return v.astype(F32).T.astype(BF16)


class _Stream:
    def __init__(self, load_qt2, load_k, load_v, store_o):
        self.load_qt2, self.load_k, self.load_v, self.store_o = load_qt2, load_k, load_v, store_o


def _pair_attention(streams, n_rows, bias_ref, far_bias, vt_scr, qt_scr, s_scr, m_scr, l_scr, acc_scr):
    tq, tk = ATT_TQ, ATT_TK
    assert tq == tk
    n_q = n_rows // tq
    hd = LANES // 2
    ids = range(len(streams))

    def put_queries(i):
        for s in ids:
            qt_scr[s] = streams[s].load_qt2(pl.ds(i * tq, tq))

    def put_scores(j, slot):
        k_rows = pl.ds(j * tk, tk)
        for s in ids:
            s_scr[s, slot] = _dot(streams[s].load_k(k_rows), qt_scr[s])

    def consume(j, slot, bias_idx):
        for s in ids:
            st = s_scr[s, slot]
            if bias_idx is not None:
                st = st + bias_ref[bias_idx]
            s3 = st.reshape(tk // 8, 8, 2 * tq)
            m_new = _sublane_allreduce(jnp.max(s3, axis=0), jnp.maximum)
            if j > 0:
                m_old = m_scr[s]
                m_new = jnp.maximum(m_old, m_new)
                alpha = jnp.exp2(m_old - m_new)
            p3 = jnp.exp2(s3 - m_new[None])
            l_new = _sublane_allreduce(jnp.sum(p3, axis=0), jnp.add)
            l_scr[s] = l_new if j == 0 else alpha * l_scr[s] + l_new
            m_scr[s] = m_new
            pt = p3.reshape(tk, 2 * tq).astype(BF16)
            vt = vt_scr[s, j]
            for h in range(2):
                pv = _dot(vt[h * hd:(h + 1) * hd, :], pt[:, h * tq:(h + 1) * tq])
                if j > 0:
                    a3 = alpha[:, h * tq:(h + 1) * tq][None]
                    pv = (acc_scr[s, h].reshape(hd // 8, 8, tq) * a3).reshape(hd, tq) + pv
                acc_scr[s, h] = pv

    def step(j, slot, bias_idx):
        put_scores(j + 1, 1 - slot)
        consume(j, slot, bias_idx)

    for s in ids:
        for j in range(n_rows // tk):
            vt_scr[s, j] = _transpose_bf16(streams[s].load_v(pl.ds(j * tk, tk)))
    put_queries(0)
    put_scores(0, 0)

    t = 0
    for i in range(n_q):
        for j in range(i):
            step(j, t % 2, None if far_bias is None else min(i - j, far_bias))
            t += 1
        put_queries(min(i + 1, n_q - 1))
        put_scores(0, 1 - t % 2)
        consume(i, t % 2, 0)
        t += 1

        for s in ids:
            inv_l = 1.0 / l_scr[s]
            ot = jnp.concatenate(
                [(acc_scr[s, h].reshape(hd // 8, 8, tq) * inv_l[:, h * tq:(h + 1) * tq][None]).reshape(hd, tq)
                 for h in range(2)], axis=0)
            streams[s].store_o(pl.ds(i * tq, tq), ot.T)


def _lane_block(ref, s):
    cols = slice(s * LANES, (s + 1) * LANES)
    return lambda rows: ref[0, rows, cols]


def _mla_kernel(qn_ref, qpe_ref, kn_ref, kpe_ref, v_ref, bias_ref, o_ref, *scratch):
    feat = _row_iota((2 * LANES, ATT_TQ))
    pe_head = (feat % (LANES // 2)) // (MLA_ROPE // 2)
    is_pe = feat >= LANES
    n_streams = o_ref.shape[2] // LANES
    assert n_streams == 2

    def make(s):
        mask_a = (feat < LANES // 2) | (is_pe & (pe_head == 2 * s))
        mask_b = ((feat >= LANES // 2) & (feat < LANES)) | (is_pe & (pe_head == 2 * s + 1))
        qn, kn = _lane_block(qn_ref, s), _lane_block(kn_ref, s)

        def load_qt2(rows):
            qt = jnp.concatenate([qn(rows), qpe_ref[0, rows, :]], axis=1).astype(F32).T
            z = jnp.zeros_like(qt)
            return jnp.concatenate([jnp.where(mask_a, qt, z), jnp.where(mask_b, qt, z)],
                                   axis=1).astype(BF16)

        def load_k(rows):
            return jnp.concatenate([kn(rows), kpe_ref[0, rows, :]], axis=1)

        def store_o(rows, val):
            o_ref[0, rows, s * LANES:(s + 1) * LANES] = val.astype(o_ref.dtype)

        return _Stream(load_qt2, load_k, _lane_block(v_ref, s), store_o)

    _pair_attention([make(s) for s in range(n_streams)], o_ref.shape[1], bias_ref, None, *scratch)


def _dil_kernel(q_ref, k_ref, v_ref, bias_ref, o_ref, *scratch):
    feat = _row_iota((LANES, ATT_TQ))
    is_a = (feat % (LANES // 2)) < DIL_HEAD_DIM // 2

    def make(s):
        q = _lane_block(q_ref, s)

        def load_qt2(rows):
            qt = q(rows).astype(F32).T
            z = jnp.zeros_like(qt)
            return jnp.concatenate([jnp.where(is_a, qt, z), jnp.where(is_a, z, qt)],
                                   axis=1).astype(BF16)

        def store_o(rows, val):
            o_ref[0, rows, s * LANES:(s + 1) * LANES] = val.astype(o_ref.dtype)

        return _Stream(load_qt2, _lane_block(k_ref, s), _lane_block(v_ref, s), store_o)

    _pair_attention([make(s) for s in range(o_ref.shape[2] // LANES)], o_ref.shape[1], bias_ref,
                    bias_ref.shape[0] - 1, *scratch)


def _attn_call(kernel, name, arrays, widths, qk_width, bias, B, S):
    n_s = ATT_PAIRS
    col = lambda w: pl.BlockSpec((1, S, w), lambda b, g: (b, 0, g))
    shared = pl.BlockSpec((1, S, LANES), lambda b, g: (b, 0, 0))
    in_specs = [shared if w is None else col(w) for w in widths] + [_const_spec(bias.shape)]
    return pl.pallas_call(
        kernel,
        out_shape=jax.ShapeDtypeStruct((B, S, N_HEADS * MLA_V), BF16),
        grid=(B, N_HEADS // 2 // n_s),
        in_specs=in_specs,
        out_specs=col(n_s * LANES),
        scratch_shapes=[pltpu.VMEM((n_s, S // ATT_TK, LANES, ATT_TK), BF16),
                        pltpu.VMEM((n_s, qk_width, 2 * ATT_TQ), BF16),
                        pltpu.VMEM((n_s, 2, ATT_TK, 2 * ATT_TQ), F32),
                        pltpu.VMEM((n_s, 8, 2 * ATT_TQ), F32),
                        pltpu.VMEM((n_s, 8, 2 * ATT_TQ), F32),
                        pltpu.VMEM((n_s, 2, LANES // 2, ATT_TQ), F32)],
        compiler_params=pltpu.CompilerParams(dimension_semantics=("arbitrary", "arbitrary"),
                                             vmem_limit_bytes=VMEM_LIMIT),
        name=name,
    )(*arrays, bias)


def _two_heads(tile_qk):
    return np.concatenate([tile_qk.T, tile_qk.T], axis=1)


def _mla_bias():
    d = np.arange(ATT_TQ)[:, None] - np.arange(ATT_TK)[None, :]
    tile = np.where(d >= 0, 0.0, MASK_BIAS).astype(np.float32)
    return jnp.asarray(_two_heads(tile)[None])


def _dil_bias():
    (far_window, far_dil), = [(w, d) for w, d in DIL_PATTERNS if w == max(p[0] for p in DIL_PATTERNS)]
    assert ATT_TK % far_dil == 0
    max_near = max(w for w, d in DIL_PATTERNS if w < far_window)
    n_tiles = -(-max_near // ATT_TK) + 2
    tiles = []
    for blk in range(n_tiles):
        delta = blk * ATT_TK + np.arange(ATT_TQ)[:, None] - np.arange(ATT_TK)[None, :]
        mult = np.zeros(delta.shape, np.int64)
        for window, dil in DIL_PATTERNS:
            mult += (delta >= 0) & (delta % dil == 0) & (delta <= window)
        tile = np.where(mult > 0, np.log2(np.maximum(mult, 1)), MASK_BIAS).astype(np.float32)
        tiles.append(_two_heads(tile))
    return jnp.asarray(np.stack(tiles))


def _ffn_kernel(x_ref, om_ref, od_ref, mod_ref, gffn_ref, wo_ref, wup_ref, wconv_ref, bconv_ref,
                wdown_ref, out_ref, carry_scr, acc_scr, h2_scr, y_scr):
    g1 = mod_ref[0, 2:3, :]
    sh2 = mod_ref[0, 3:4, :]
    sc2 = mod_ref[0, 4:5, :]
    g2 = mod_ref[0, 5:6, :]
    half = wo_ref.shape[0] // 2
    mix = _dot(om_ref[0], wo_ref[:half, :]) + _dot(od_ref[0], wo_ref[half:, :])
    x1 = x_ref[0] + g1 * mix
    h2_scr[...] = (_rms_rows(x1, D_MODEL) * (gffn_ref[...] * (1.0 + sc2)) + sh2).astype(BF16)
    out_ref[0] = x1

    @pl.when(pl.program_id(1) == 0)
    def _():
        carry_scr[...] = jnp.zeros(carry_scr.shape, F32)

    def up(f):
        for half_idx, t in enumerate((f, f + N_FT)):
            y = _dot(h2_scr[...], wup_ref[:, t * FFN_TF:(t + 1) * FFN_TF])
            y_scr[f % 2, half_idx, 0:8, :] = carry_scr[t]
            y_scr[f % 2, half_idx, 8:, :] = y
            carry_scr[t] = y[FFN_TM - 8:, :]

    def conv(f, half_idx):
        t = f + half_idx * N_FT
        yb = y_scr.at[f % 2, half_idx]
        cols = slice(t * FFN_TF, (t + 1) * FFN_TF)
        w = wconv_ref[:, cols]
        return (w[2:3, :] * yb[8:8 + FFN_TM, :] + w[1:2, :] * yb[7:7 + FFN_TM, :]
                + w[0:1, :] * yb[6:6 + FFN_TM, :] + bconv_ref[:, cols])

    up(0)
    for f in range(N_FT):
        if f + 1 < N_FT:
            up(f + 1)
        gate = conv(f, 0)
        val = conv(f, 1)
        a = (gate * (1.0 / (1.0 + jnp.exp(-gate))) * val).astype(BF16)
        d = _dot(a, wdown_ref[f * FFN_TF:(f + 1) * FFN_TF, :])
        if f == 0:
            acc_scr[...] = d
        else:
            acc_scr[...] += d
    out_ref[0] = out_ref[0] + g2 * acc_scr[...]


def _ffn(x, o_mla, o_dil, mod3, g_ffn, w_o, w_up, w_conv, b_conv, w_down):
    B, S, _ = x.shape
    tm = FFN_TM
    row = lambda w: pl.BlockSpec((1, tm, w), lambda b, i: (b, i, 0))
    consts = (g_ffn, w_o, w_up, w_conv, b_conv, w_down)
    return pl.pallas_call(
        _ffn_kernel,
        out_shape=jax.ShapeDtypeStruct(x.shape, F32),
        grid=(B, S // tm),
        in_specs=[row(D_MODEL), row(o_mla.shape[2]), row(o_dil.shape[2]),
                  pl.BlockSpec((1, 6, D_MODEL), lambda b, i: (b, 0, 0))]
                 + [_const_spec(a.shape) for a in consts],
        out_specs=row(D_MODEL),
        scratch_shapes=[pltpu.VMEM((2 * N_FT, 8, FFN_TF), F32), pltpu.VMEM((tm, D_MODEL), F32),
                        pltpu.VMEM((tm, D_MODEL), BF16),
                        pltpu.VMEM((2, 2, tm + 8, FFN_TF), F32)],
        compiler_params=pltpu.CompilerParams(dimension_semantics=("arbitrary", "arbitrary"),
                                             vmem_limit_bytes=VMEM_LIMIT),
        name="ffn",
    )(x, o_mla, o_dil, mod3, *consts)


def _layout_indices():
    lane = np.arange(LANES)
    dil_head = (lane % 64) // 32
    dil_feat = (lane // 64) * 32 + lane % 32
    dil_cols = np.concatenate([(2 * g + dil_head) * DIL_HEAD_DIM + dil_feat for g in range(4)])
    pe_head = (lane % 64) // 16
    pe_feat = (lane // 64) * 16 + lane % 16
    qb_nope = np.concatenate([h * (MLA_NOPE + MLA_ROPE) + np.arange(MLA_NOPE) for h in range(N_HEADS)])
    qb_pe = np.concatenate([(4 * g + pe_head) * (MLA_NOPE + MLA_ROPE) + MLA_NOPE + pe_feat
                            for g in range(2)])
    kvb_k = np.concatenate([h * (MLA_NOPE + MLA_V) + np.arange(MLA_NOPE) for h in range(N_HEADS)])
    kvb_v = np.concatenate([h * (MLA_NOPE + MLA_V) + MLA_NOPE + np.arange(MLA_V) for h in range(N_HEADS)])
    return dict(dil_cols=dil_cols, dil_feat=dil_feat, pe_feat=pe_feat, pe_head=pe_head,
                dil_head=dil_head, qb_cols=np.concatenate([qb_nope, qb_pe]),
                kvb_cols=np.concatenate([kvb_k, kvb_v]))


def _segment_matrices(ix):
    i = np.arange(MXU_DIM)
    lane = i % LANES
    seg64 = i // 64
    segpe = (i // LANES) * 4 + ix["pe_head"][lane]
    segpair = (i // LANES) * 2 + ix["dil_head"][lane]
    same = lambda s: jnp.asarray((s[:, None] == s[None, :]).astype(np.float32), dtype=BF16)
    return same(seg64), same(segpe), same(segpair)


def _rope_constants(ix):
    def inv_freq(d):
        half = d // 2
        return jnp.power(ROPE_THETA, -2.0 * jnp.arange(half, dtype=F32) / d)
    fd, fp = inv_freq(DIL_HEAD_DIM), inv_freq(MLA_ROPE)
    nd, npe = fd.shape[0], fp.shape[0]
    freq = jnp.concatenate([fd, fp, jnp.zeros((64 - nd - npe,), F32)])[:, None]
    lane = np.arange(LANES)
    sign = np.where(lane < 64, -1.0, 1.0)
    e = np.zeros((2 * 64, 4 * LANES), np.float32)
    e[lane % 32, lane] = 1.0
    e[64 + lane % 32, LANES + lane] = sign
    e[nd + lane % 16, 2 * LANES + lane] = 1.0
    e[64 + nd + lane % 16, 3 * LANES + lane] = sign
    return freq, jnp.asarray(e, dtype=BF16)


def kernel(x, c, positions, w_ada, b_ada, g_mix_norm, w_in, g_q_lat, w_q_b, g_kv_lat, w_kv_b,
           g_mla_q_nope, g_mla_q_pe, g_mla_k_nope, g_mla_k_pe, g_dil_q, g_dil_k, w_o,
           g_ffn_norm, w_up, w_conv, b_conv, w_down):
    B, S, D = x.shape
    assert D == D_MODEL and S % PROJ_TM == 0 and S % ATT_TQ == 0 and S % FFN_TM == 0
    assert S <= max(w for w, _ in DIL_PATTERNS)
    ix = _layout_indices()
    seg64, segpe, segpair = _segment_matrices(ix)
    freq, expand = _rope_constants(ix)
    mla_scale = (MLA_NOPE + MLA_ROPE) ** -0.5 * LOG2E
    dil_scale = DIL_HEAD_DIM ** -0.5 * LOG2E
    pos3 = positions.reshape(B, 1, S)

    for l in range(w_ada.shape[0]):
        mod3 = _ada(c, w_ada[l], b_ada[l][None, :]).reshape(B, 6, D)

        wi = w_in[l]
        kpe_cols = MLA_Q_LORA + MLA_KV_LORA + ix["pe_feat"]
        qd0 = MLA_Q_LORA + MLA_KV_LORA + MLA_ROPE
        w_in_p = jnp.concatenate([
            wi[:, :MLA_Q_LORA + MLA_KV_LORA], wi[:, kpe_cols],
            wi[:, qd0 + ix["dil_cols"]], wi[:, qd0 + DIL_WIDTH + ix["dil_cols"]],
            wi[:, qd0 + 2 * DIL_WIDTH:]], axis=1).astype(BF16)
        consts = (
            g_mix_norm[l][None, :], w_in_p,
            g_q_lat[l][None, :], w_q_b[l][:, ix["qb_cols"]].astype(BF16),
            g_kv_lat[l][None, :], w_kv_b[l][:, ix["kvb_cols"]].astype(BF16),
            (jnp.tile(g_mla_q_nope[l], N_HEADS) * mla_scale)[None, :],
            (jnp.tile(g_mla_q_pe[l][ix["pe_feat"]], 2) * mla_scale)[None, :],
            jnp.tile(g_mla_k_nope[l], N_HEADS)[None, :],
            g_mla_k_pe[l][ix["pe_feat"]][None, :],
            (jnp.tile(g_dil_q[l][ix["dil_feat"]], 4) * dil_scale)[None, :],
            jnp.tile(g_dil_k[l][ix["dil_feat"]], 4)[None, :],
            seg64, segpe, segpair, freq, expand)
        qn, qpe, kn, kpe, vm, qd, kd, vd = _proj(x, pos3, mod3, consts)

        two = ATT_PAIRS * LANES
        o_mla = _attn_call(_mla_kernel, "mla", (qn, qpe, kn, kpe, vm), (two, LANES, two, None, two),
                           2 * LANES, _mla_bias(), B, S)
        o_dil = _attn_call(_dil_kernel, "dil", (qd, kd, vd), (two, two, two), LANES, _dil_bias(), B, S)

        x = _ffn(x, o_mla, o_dil, mod3, g_ffn_norm[l][None, :], w_o[l].astype(BF16),
                 w_up[l].astype(BF16), w_conv[l], b_conv[l][None, :], w_down[l].astype(BF16))
    return x
```

```python
import functools

import numpy as np
import jax
import jax.numpy as jnp
from jax import lax
from jax.experimental import pallas as pl
from jax.experimental.pallas import tpu as pltpu

F32 = jnp.float32
BF16 = jnp.bfloat16

D_MODEL = 1024
N_HEADS = 8
MLA_NOPE = 64
MLA_ROPE = 32
MLA_V = 64
MLA_Q_LORA = 512
MLA_KV_LORA = 256
DIL_HEAD_DIM = 64
DIL_WIDTH = N_HEADS * DIL_HEAD_DIM
DIL_PATTERNS = ((128, 1), (512, 4), (2048, 16))
D_FF = 2816
ROPE_THETA = 10000.0
EPS = 1e-6
MASK_BIAS = -1e30
LOG2E = 1.4426950408889634

LANES = 128
MXU_DIM = 256
VMEM_LIMIT = 56 * 1024 * 1024
ADA_TN = 1024
PROJ_TM = 512
ATT_TQ = 256
ATT_TK = 256
ATT_PAIRS = 2
FFN_TM = 512
FFN_TF = 256
N_FT = D_FF // FFN_TF
DOWN_TILES = 2

C_QLAT = 0
C_KVLAT = C_QLAT + MLA_Q_LORA
C_KPE = C_KVLAT + MLA_KV_LORA
C_QD = C_KPE + LANES
C_KD = C_QD + DIL_WIDTH
C_VD = C_KD + DIL_WIDTH
IN_COLS_P = C_VD + DIL_WIDTH


def _dot(a, b):
    return jnp.dot(a, b, preferred_element_type=F32)


def _const_spec(shape):
    nd = len(shape)
    return pl.BlockSpec(shape, lambda *_: (0,) * nd, pipeline_mode=pl.Buffered(1))


def _split_bf16(v):
    hi = v.astype(BF16)
    lo = (v - hi.astype(F32)).astype(BF16)
    return hi, lo


def _ada_kernel(c_ref, w_ref, b_ref, o_ref):
    c = c_ref[...]
    a = c * (1.0 / (1.0 + jnp.exp(-c)))
    a_hi, a_lo = _split_bf16(a)
    w_hi, w_lo = _split_bf16(w_ref[...])
    o_ref[...] = _dot(a_hi, w_hi) + _dot(a_lo, w_hi) + _dot(a_hi, w_lo) + b_ref[...]


def _ada(c, w_ada, b_ada):
    B = c.shape[0]
    n = w_ada.shape[1]
    return pl.pallas_call(
        _ada_kernel,
        out_shape=jax.ShapeDtypeStruct((B, n), F32),
        grid=(n // ADA_TN,),
        in_specs=[pl.BlockSpec((B, D_MODEL), lambda j: (0, 0)),
                  pl.BlockSpec((D_MODEL, ADA_TN), lambda j: (0, j)),
                  pl.BlockSpec((1, ADA_TN), lambda j: (0, j))],
        out_specs=pl.BlockSpec((B, ADA_TN), lambda j: (0, j)),
        compiler_params=pltpu.CompilerParams(dimension_semantics=("arbitrary",),
                                             vmem_limit_bytes=VMEM_LIMIT),
        name="ada",
    )(c, w_ada, b_ada)


def _rms_rows(v, width):
    ms = jnp.sum(v * v, axis=-1, keepdims=True) * (1.0 / width)
    return v * lax.rsqrt(ms + EPS)


def _seg_rms(v, seg_ref, seg_width):
    outs = []
    for c0 in range(0, v.shape[1], MXU_DIM):
        blk = v[:, c0:c0 + MXU_DIM]
        ss = _dot((blk * blk).astype(BF16), seg_ref[...])
        outs.append(blk * lax.rsqrt(ss * (1.0 / seg_width) + EPS))
    return outs[0] if len(outs) == 1 else jnp.concatenate(outs, axis=1)


def _rope(v, cos, sin_signed):
    outs = []
    for c0 in range(0, v.shape[1], LANES):
        blk = v[:, c0:c0 + LANES]
        outs.append(blk * cos + pltpu.roll(blk, LANES // 2, 1) * sin_signed)
    return outs[0] if len(outs) == 1 else jnp.concatenate(outs, axis=1)


GAIN_ROWS = (("mix", D_MODEL), ("q_lat", MLA_Q_LORA), ("kv_lat", MLA_KV_LORA),
             ("q_nope", N_HEADS * MLA_NOPE), ("q_pe", N_HEADS * MLA_ROPE), ("k_nope", N_HEADS * MLA_NOPE),
             ("k_pe", LANES), ("dil_q", DIL_WIDTH), ("dil_k", DIL_WIDTH))


class _Rows:
    def __init__(self, ref):
        self._ref = ref

    def __getattr__(self, name):
        idx, width = next((i, w) for i, (n, w) in enumerate(GAIN_ROWS) if n == name)
        return self._ref[idx:idx + 1, :width]


def _proj_kernel(x_ref, pos_ref, mod_ref, gains_ref, win_ref, wqb_ref, wkvb_ref,
                 seg64_ref, segpe_ref, segpair_ref, freq_ref, expand_ref,
                 qn_ref, qpe_ref, kn_ref, kpe_ref, vm_ref, qd_ref, kd_ref, vd_ref):
    g = _Rows(gains_ref)
    x = x_ref[0]
    sh1 = mod_ref[0, 0:1, :]
    sc1 = mod_ref[0, 1:2, :]
    h = (_rms_rows(x, D_MODEL) * (g.mix * (1.0 + sc1)) + sh1).astype(BF16)

    pos = pos_ref[0].astype(F32)
    ang = freq_ref[...] * pos
    cs = jnp.concatenate([jnp.cos(ang), jnp.sin(ang)], axis=0).T
    cs_hi, cs_lo = _split_bf16(cs)
    tabs = _dot(cs_hi, expand_ref[...]) + _dot(cs_lo, expand_ref[...])
    cos_d, sin_d = tabs[:, 0:LANES], tabs[:, LANES:2 * LANES]
    cos_p, sin_p = tabs[:, 2 * LANES:3 * LANES], tabs[:, 3 * LANES:4 * LANES]

    q_lat = _dot(h, win_ref[:, C_QLAT:C_QLAT + MLA_Q_LORA])
    q_in = (_rms_rows(q_lat, MLA_Q_LORA) * g.q_lat).astype(BF16)
    q = _dot(q_in, wqb_ref[...])
    n_nope = N_HEADS * MLA_NOPE
    qn_ref[0] = (_seg_rms(q[:, :n_nope], seg64_ref, MLA_NOPE) * g.q_nope).astype(BF16)
    q_pe = _seg_rms(q[:, n_nope:], segpe_ref, MLA_ROPE) * g.q_pe
    qpe_ref[0] = _rope(q_pe, cos_p, sin_p).astype(BF16)

    kv_lat = _dot(h, win_ref[:, C_KVLAT:C_KVLAT + MLA_KV_LORA])
    kv_in = (_rms_rows(kv_lat, MLA_KV_LORA) * g.kv_lat).astype(BF16)
    kv = _dot(kv_in, wkvb_ref[...])
    kn_ref[0] = (_seg_rms(kv[:, :n_nope], seg64_ref, MLA_NOPE) * g.k_nope).astype(BF16)
    vm_ref[0] = kv[:, n_nope:].astype(BF16)
    k_pe = _dot(h, win_ref[:, C_KPE:C_KPE + LANES])
    kpe_ref[0] = _rope(_rms_rows(k_pe, LANES) * g.k_pe, cos_p, sin_p).astype(BF16)

    qd = _seg_rms(_dot(h, win_ref[:, C_QD:C_QD + DIL_WIDTH]), segpair_ref, DIL_HEAD_DIM)
    qd_ref[0] = _rope(qd * g.dil_q, cos_d, sin_d).astype(BF16)
    kd = _seg_rms(_dot(h, win_ref[:, C_KD:C_KD + DIL_WIDTH]), segpair_ref, DIL_HEAD_DIM)
    kd_ref[0] = _rope(kd * g.dil_k, cos_d, sin_d).astype(BF16)
    vd_ref[0] = _dot(h, win_ref[:, C_VD:C_VD + DIL_WIDTH]).astype(BF16)


def _proj(x, pos3, mod3, consts):
    B, S, _ = x.shape
    tm = PROJ_TM
    row = lambda w: pl.BlockSpec((1, tm, w), lambda b, i: (b, i, 0))
    out_widths = (512, 256, 512, LANES, 512, 512, 512, 512)
    in_specs = [row(D_MODEL),
                pl.BlockSpec((1, 1, tm), lambda b, i: (b, 0, i)),
                pl.BlockSpec((1, 6, D_MODEL), lambda b, i: (b, 0, 0))]
    in_specs += [_const_spec(a.shape) for a in consts]
    return pl.pallas_call(
        _proj_kernel,
        out_shape=tuple(jax.ShapeDtypeStruct((B, S, w), BF16) for w in out_widths),
        grid=(B, S // tm),
        in_specs=in_specs,
        out_specs=tuple(row(w) for w in out_widths),
        compiler_params=pltpu.CompilerParams(dimension_semantics=("arbitrary", "arbitrary"),
                                             vmem_limit_bytes=VMEM_LIMIT),
        name="proj",
    )(x, pos3, mod3, *consts)


def _row_iota(shape):
    return lax.broadcasted_iota(jnp.int32, shape, 0)


def _sublane_allreduce(v, op):
    for shift in (4, 2, 1):
        v = op(v, pltpu.roll(v, shift, 0))
    return v


def _transpose_bf16(v):
    return v.astype(F32).T.astype(BF16)


class _Stream:
    def __init__(self, load_qt2, load_k, load_v, store_o):
        self.load_qt2, self.load_k, self.load_v, self.store_o = load_qt2, load_k, load_v, store_o


def _pair_attention(streams, n_rows, bias_ref, far_bias, vt_scr, qt_scr, s_scr, m_scr, l_scr, acc_scr):
    tq, tk = ATT_TQ, ATT_TK
    assert tq == tk
    n_q = n_rows // tq
    hd = LANES // 2
    ids = range(len(streams))

    def put_queries(i):
        for s in ids:
            qt_scr[s] = streams[s].load_qt2(pl.ds(i * tq, tq))

    def put_scores(j, slot):
        k_rows = pl.ds(j * tk, tk)
        for s in ids:
            s_scr[s, slot] = _dot(streams[s].load_k(k_rows), qt_scr[s])

    def consume(j, slot, bias_idx):
        for s in ids:
            st = s_scr[s, slot]
            if bias_idx is not None:
                st = st + bias_ref[bias_idx]
            s3 = st.reshape(tk // 8, 8, 2 * tq)
            m_new = _sublane_allreduce(jnp.max(s3, axis=0), jnp.maximum)
            if j > 0:
                m_old = m_scr[s]
                m_new = jnp.maximum(m_old, m_new)
                alpha = jnp.exp2(m_old - m_new)
            p3 = jnp.exp2(s3 - m_new[None])
            l_new = _sublane_allreduce(jnp.sum(p3, axis=0), jnp.add)
            l_scr[s] = l_new if j == 0 else alpha * l_scr[s] + l_new
            m_scr[s] = m_new
            pt = p3.reshape(tk, 2 * tq).astype(BF16)
            vt = vt_scr[s, j]
            for h in range(2):
                pv = _dot(vt[h * hd:(h + 1) * hd, :], pt[:, h * tq:(h + 1) * tq])
                if j > 0:
                    a3 = alpha[:, h * tq:(h + 1) * tq][None]
                    pv = (acc_scr[s, h].reshape(hd // 8, 8, tq) * a3).reshape(hd, tq) + pv
                acc_scr[s, h] = pv

    def step(j, slot, bias_idx):
        put_scores(j + 1, 1 - slot)
        consume(j, slot, bias_idx)

    for s in ids:
        for j in range(n_rows // tk):
            vt_scr[s, j] = _transpose_bf16(streams[s].load_v(pl.ds(j * tk, tk)))
    put_queries(0)
    put_scores(0, 0)

    t = 0
    for i in range(n_q):
        for j in range(i):
            step(j, t % 2, None if far_bias is None else min(i - j, far_bias))
            t += 1
        put_queries(min(i + 1, n_q - 1))
        put_scores(0, 1 - t % 2)
        consume(i, t % 2, 0)
        t += 1

        for s in ids:
            inv_l = 1.0 / l_scr[s]
            ot = jnp.concatenate(
                [(acc_scr[s, h].reshape(hd // 8, 8, tq) * inv_l[:, h * tq:(h + 1) * tq][None]).reshape(hd, tq)
                 for h in range(2)], axis=0)
            streams[s].store_o(pl.ds(i * tq, tq), ot.T)


def _lane_block(ref, s):
    cols = slice(s * LANES, (s + 1) * LANES)
    return lambda rows: ref[0, rows, cols]


def _mla_kernel(qn_ref, qpe_ref, kn_ref, kpe_ref, v_ref, bias_ref, o_ref, *scratch):
    feat = _row_iota((2 * LANES, ATT_TQ))
    pe_head = (feat % (LANES // 2)) // (MLA_ROPE // 2)
    is_pe = feat >= LANES
    n_streams = o_ref.shape[2] // LANES
    assert n_streams == 2

    def make(s):
        mask_a = (feat < LANES // 2) | (is_pe & (pe_head == 2 * s))
        mask_b = ((feat >= LANES // 2) & (feat < LANES)) | (is_pe & (pe_head == 2 * s + 1))
        qn, kn = _lane_block(qn_ref, s), _lane_block(kn_ref, s)

        def load_qt2(rows):
            qt = jnp.concatenate([qn(rows), qpe_ref[0, rows, :]], axis=1).astype(F32).T
            z = jnp.zeros_like(qt)
            return jnp.concatenate([jnp.where(mask_a, qt, z), jnp.where(mask_b, qt, z)],
                                   axis=1).astype(BF16)

        def load_k(rows):
            return jnp.concatenate([kn(rows), kpe_ref[0, rows, :]], axis=1)

        def store_o(rows, val):
            o_ref[0, rows, s * LANES:(s + 1) * LANES] = val.astype(o_ref.dtype)

        return _Stream(load_qt2, load_k, _lane_block(v_ref, s), store_o)

    _pair_attention([make(s) for s in range(n_streams)], o_ref.shape[1], bias_ref, None, *scratch)


def _dil_kernel(q_ref, k_ref, v_ref, bias_ref, o_ref, *scratch):
    feat = _row_iota((LANES, ATT_TQ))
    is_a = (feat % (LANES // 2)) < DIL_HEAD_DIM // 2

    def make(s):
        q = _lane_block(q_ref, s)

        def load_qt2(rows):
            qt = q(rows).astype(F32).T
            z = jnp.zeros_like(qt)
            return jnp.concatenate([jnp.where(is_a, qt, z), jnp.where(is_a, z, qt)],
                                   axis=1).astype(BF16)

        def store_o(rows, val):
            o_ref[0, rows, s * LANES:(s + 1) * LANES] = val.astype(o_ref.dtype)

        return _Stream(load_qt2, _lane_block(k_ref, s), _lane_block(v_ref, s), store_o)

    _pair_attention([make(s) for s in range(o_ref.shape[2] // LANES)], o_ref.shape[1], bias_ref,
                    bias_ref.shape[0] - 1, *scratch)


def _attn_call(kernel, name, arrays, widths, qk_width, bias, B, S):
    n_s = ATT_PAIRS
    col = lambda w: pl.BlockSpec((1, S, w), lambda b, g: (b, 0, g))
    shared = pl.BlockSpec((1, S, LANES), lambda b, g: (b, 0, 0))
    in_specs = [shared if w is None else col(w) for w in widths] + [_const_spec(bias.shape)]
    return pl.pallas_call(
        kernel,
        out_shape=jax.ShapeDtypeStruct((B, S, N_HEADS * MLA_V), BF16),
        grid=(B, N_HEADS // 2 // n_s),
        in_specs=in_specs,
        out_specs=col(n_s * LANES),
        scratch_shapes=[pltpu.VMEM((n_s, S // ATT_TK, LANES, ATT_TK), BF16),
                        pltpu.VMEM((n_s, qk_width, 2 * ATT_TQ), BF16),
                        pltpu.VMEM((n_s, 2, ATT_TK, 2 * ATT_TQ), F32),
                        pltpu.VMEM((n_s, 8, 2 * ATT_TQ), F32),
                        pltpu.VMEM((n_s, 8, 2 * ATT_TQ), F32),
                        pltpu.VMEM((n_s, 2, LANES // 2, ATT_TQ), F32)],
        compiler_params=pltpu.CompilerParams(dimension_semantics=("arbitrary", "arbitrary"),
                                             vmem_limit_bytes=VMEM_LIMIT),
        name=name,
    )(*arrays, bias)


def _two_heads(tile_qk):
    return np.concatenate([tile_qk.T, tile_qk.T], axis=1)


def _mla_bias():
    d = np.arange(ATT_TQ)[:, None] - np.arange(ATT_TK)[None, :]
    tile = np.where(d >= 0, 0.0, MASK_BIAS).astype(np.float32)
    return jnp.asarray(_two_heads(tile)[None])


def _dil_bias():
    (far_window, far_dil), = [(w, d) for w, d in DIL_PATTERNS if w == max(p[0] for p in DIL_PATTERNS)]
    assert ATT_TK % far_dil == 0
    max_near = max(w for w, d in DIL_PATTERNS if w < far_window)
    n_tiles = -(-max_near // ATT_TK) + 2
    tiles = []
    for blk in range(n_tiles):
        delta = blk * ATT_TK + np.arange(ATT_TQ)[:, None] - np.arange(ATT_TK)[None, :]
        mult = np.zeros(delta.shape, np.int64)
        for window, dil in DIL_PATTERNS:
            mult += (delta >= 0) & (delta % dil == 0) & (delta <= window)
        tile = np.where(mult > 0, np.log2(np.maximum(mult, 1)), MASK_BIAS).astype(np.float32)
        tiles.append(_two_heads(tile))
    return jnp.asarray(np.stack(tiles))


def _ffn_kernel(x_ref, om_ref, od_ref, mod_ref, gffn_ref, wo_ref, wup_ref, wconv_ref, bconv_ref,
                wdown_ref, out_ref, carry_scr, acc_scr, h2_scr, y_scr, a_scr):
    g1 = mod_ref[0, 2:3, :]
    sh2 = mod_ref[0, 3:4, :]
    sc2 = mod_ref[0, 4:5, :]
    g2 = mod_ref[0, 5:6, :]
    half = wo_ref.shape[0] // 2
    mix = _dot(om_ref[0], wo_ref[:half, :]) + _dot(od_ref[0], wo_ref[half:, :])
    x1 = x_ref[0] + g1 * mix
    h2_scr[...] = (_rms_rows(x1, D_MODEL) * (gffn_ref[...] * (1.0 + sc2)) + sh2).astype(BF16)
    out_ref[0] = x1

    @pl.when(pl.program_id(1) == 0)
    def _():
        carry_scr[...] = jnp.zeros(carry_scr.shape, F32)

    def up(f):
        for half_idx, t in enumerate((f, f + N_FT)):
            y = _dot(h2_scr[...], wup_ref[:, t * FFN_TF:(t + 1) * FFN_TF])
            y_scr[f % 2, half_idx, 0:8, :] = carry_scr[t]
            y_scr[f % 2, half_idx, 8:, :] = y
            carry_scr[t] = y[FFN_TM - 8:, :]

    def conv(f, half_idx):
        t = f + half_idx * N_FT
        yb = y_scr.at[f % 2, half_idx]
        cols = slice(t * FFN_TF, (t + 1) * FFN_TF)
        w = wconv_ref[:, cols]
        return (w[2:3, :] * yb[8:8 + FFN_TM, :] + w[1:2, :] * yb[7:7 + FFN_TM, :]
                + w[0:1, :] * yb[6:6 + FFN_TM, :] + bconv_ref[:, cols])

    up(0)
    first_down = True
    for f in range(N_FT):
        if f + 1 < N_FT:
            up(f + 1)
        gate = conv(f, 0)
        val = conv(f, 1)
        k = f % DOWN_TILES
        a_scr[:, k * FFN_TF:(k + 1) * FFN_TF] = (gate * (1.0 / (1.0 + jnp.exp(-gate))) * val).astype(BF16)
        if k == DOWN_TILES - 1 or f == N_FT - 1:
            f0 = f - k
            d = _dot(a_scr[:, :(k + 1) * FFN_TF], wdown_ref[f0 * FFN_TF:(f + 1) * FFN_TF, :])
            if first_down:
                acc_scr[...] = d
                first_down = False
            else:
                acc_scr[...] += d
    out_ref[0] = out_ref[0] + g2 * acc_scr[...]


def _ffn(x, o_mla, o_dil, mod3, g_ffn, w_o, w_up, w_conv, b_conv, w_down):
    B, S, _ = x.shape
    tm = FFN_TM
    row = lambda w: pl.BlockSpec((1, tm, w), lambda b, i: (b, i, 0))
    consts = (g_ffn, w_o, w_up, w_conv, b_conv, w_down)
    return pl.pallas_call(
        _ffn_kernel,
        out_shape=jax.ShapeDtypeStruct(x.shape, F32),
        grid=(B, S // tm),
        in_specs=[row(D_MODEL), row(o_mla.shape[2]), row(o_dil.shape[2]),
                  pl.BlockSpec((1, 6, D_MODEL), lambda b, i: (b, 0, 0))]
                 + [_const_spec(a.shape) for a in consts],
        out_specs=row(D_MODEL),
        scratch_shapes=[pltpu.VMEM((2 * N_FT, 8, FFN_TF), F32), pltpu.VMEM((tm, D_MODEL), F32),
                        pltpu.VMEM((tm, D_MODEL), BF16),
                        pltpu.VMEM((2, 2, tm + 8, FFN_TF), F32),
                        pltpu.VMEM((tm, DOWN_TILES * FFN_TF), BF16)],
        compiler_params=pltpu.CompilerParams(dimension_semantics=("arbitrary", "arbitrary"),
                                             vmem_limit_bytes=VMEM_LIMIT),
        name="ffn",
    )(x, o_mla, o_dil, mod3, *consts)


def _split_halves_layout(w, n_groups, heads, half):
    lead = w.shape[:-1]
    w = w.reshape(*lead, n_groups, heads, 2, half)
    return jnp.swapaxes(w, -3, -2).reshape(*lead, n_groups * heads * 2 * half)


def _segment_matrices():
    i = np.arange(MXU_DIM)
    lane = i % LANES
    seg64 = i // 64
    segpe = (i // LANES) * 4 + (lane % 64) // (MLA_ROPE // 2)
    segpair = (i // LANES) * 2 + (lane % 64) // (DIL_HEAD_DIM // 2)
    same = lambda s: jnp.asarray((s[:, None] == s[None, :]).astype(np.float32), dtype=BF16)
    return same(seg64), same(segpe), same(segpair)


def _rope_constants():
    def inv_freq(d):
        half = d // 2
        return jnp.power(ROPE_THETA, -2.0 * jnp.arange(half, dtype=F32) / d)
    fd, fp = inv_freq(DIL_HEAD_DIM), inv_freq(MLA_ROPE)
    nd, npe = fd.shape[0], fp.shape[0]
    freq = jnp.concatenate([fd, fp, jnp.zeros((64 - nd - npe,), F32)])[:, None]
    lane = np.arange(LANES)
    sign = np.where(lane < 64, -1.0, 1.0)
    e = np.zeros((2 * 64, 4 * LANES), np.float32)
    e[lane % 32, lane] = 1.0
    e[64 + lane % 32, LANES + lane] = sign
    e[nd + lane % 16, 2 * LANES + lane] = 1.0
    e[64 + nd + lane % 16, 3 * LANES + lane] = sign
    return freq, jnp.asarray(e, dtype=BF16)


def kernel(x, c, positions, w_ada, b_ada, g_mix_norm, w_in, g_q_lat, w_q_b, g_kv_lat, w_kv_b,
           g_mla_q_nope, g_mla_q_pe, g_mla_k_nope, g_mla_k_pe, g_dil_q, g_dil_k, w_o,
           g_ffn_norm, w_up, w_conv, b_conv, w_down):
    B, S, D = x.shape
    assert D == D_MODEL and S % PROJ_TM == 0 and S % ATT_TQ == 0 and S % FFN_TM == 0
    assert S <= max(w for w, _ in DIL_PATTERNS)
    seg64, segpe, segpair = _segment_matrices()
    freq, expand = _rope_constants()
    mla_scale = (MLA_NOPE + MLA_ROPE) ** -0.5 * LOG2E
    dil_scale = DIL_HEAD_DIM ** -0.5 * LOG2E
    pos3 = positions.reshape(B, 1, S)

    for l in range(w_ada.shape[0]):
        mod3 = _ada(c, w_ada[l], b_ada[l][None, :]).reshape(B, 6, D)

        wi = w_in[l]
        c_kpe = MLA_Q_LORA + MLA_KV_LORA
        c_qd = c_kpe + MLA_ROPE
        kpe4 = jnp.broadcast_to(wi[:, c_kpe:c_qd].reshape(D, 2, 1, MLA_ROPE // 2),
                                (D, 2, 4, MLA_ROPE // 2)).reshape(D, LANES)
        dil = lambda w: _split_halves_layout(w, 4, 2, DIL_HEAD_DIM // 2)
        w_in_p = jnp.concatenate([
            wi[:, :c_kpe], kpe4, dil(wi[:, c_qd:c_qd + DIL_WIDTH]),
            dil(wi[:, c_qd + DIL_WIDTH:c_qd + 2 * DIL_WIDTH]), wi[:, c_qd + 2 * DIL_WIDTH:]],
            axis=1).astype(BF16)
        wq = w_q_b[l].reshape(MLA_Q_LORA, N_HEADS, MLA_NOPE + MLA_ROPE)
        pe4 = lambda w: _split_halves_layout(w, w.shape[-1] // LANES, 4, MLA_ROPE // 2)
        w_qb_p = jnp.concatenate([wq[:, :, :MLA_NOPE].reshape(MLA_Q_LORA, -1),
                                  pe4(wq[:, :, MLA_NOPE:].reshape(MLA_Q_LORA, -1))], axis=1).astype(BF16)
        wkv = w_kv_b[l].reshape(MLA_KV_LORA, N_HEADS, MLA_NOPE + MLA_V)
        w_kvb_p = jnp.concatenate([wkv[:, :, :MLA_NOPE].reshape(MLA_KV_LORA, -1),
                                   wkv[:, :, MLA_NOPE:].reshape(MLA_KV_LORA, -1)], axis=1).astype(BF16)
        gains = {
            "mix": g_mix_norm[l], "q_lat": g_q_lat[l], "kv_lat": g_kv_lat[l],
            "q_nope": jnp.tile(g_mla_q_nope[l], N_HEADS) * mla_scale,
            "q_pe": pe4(jnp.tile(g_mla_q_pe[l], N_HEADS)) * mla_scale,
            "k_nope": jnp.tile(g_mla_k_nope[l], N_HEADS),
            "k_pe": pe4(jnp.tile(g_mla_k_pe[l], 4)),
            "dil_q": dil(jnp.tile(g_dil_q[l], N_HEADS)) * dil_scale,
            "dil_k": dil(jnp.tile(g_dil_k[l], N_HEADS))}
        gains = jnp.stack([jnp.pad(gains[n], (0, D - w)) for n, w in GAIN_ROWS])
        consts = (gains, w_in_p, w_qb_p, w_kvb_p, seg64, segpe, segpair, freq, expand)
        qn, qpe, kn, kpe, vm, qd, kd, vd = _proj(x, pos3, mod3, consts)

        two = ATT_PAIRS * LANES
        o_mla = _attn_call(_mla_kernel, "mla", (qn, qpe, kn, kpe, vm), (two, LANES, two, None, two),
                           2 * LANES, _mla_bias(), B, S)
        o_dil = _attn_call(_dil_kernel, "dil", (qd, kd, vd), (two, two, two), LANES, _dil_bias(), B, S)

        x = _ffn(x, o_mla, o_dil, mod3, g_ffn_norm[l][None, :], w_o[l].astype(BF16),
                 w_up[l].astype(BF16), w_conv[l], b_conv[l][None, :], w_down[l].astype(BF16))
    return x
```

```python
import functools

import numpy as np
import jax
import jax.numpy as jnp
from jax import lax
from jax.experimental import pallas as pl
from jax.experimental.pallas import tpu as pltpu

F32 = jnp.float32
BF16 = jnp.bfloat16

D_MODEL = 1024
N_HEADS = 8
MLA_NOPE = 64
MLA_ROPE = 32
MLA_V = 64
MLA_Q_LORA = 512
MLA_KV_LORA = 256
DIL_HEAD_DIM = 64
DIL_WIDTH = N_HEADS * DIL_HEAD_DIM
DIL_PATTERNS = ((128, 1), (512, 4), (2048, 16))
D_FF = 2816
ROPE_THETA = 10000.0
EPS = 1e-6
MASK_BIAS = -1e30
LOG2E = 1.4426950408889634

LANES = 128
MXU_DIM = 256
VMEM_LIMIT = 56 * 1024 * 1024
ADA_TN = 1024
PROJ_TM = 1024
ATT_TQ = 256
ATT_TK = 256
ATT_PAIRS = 2
FFN_TM = 512
FFN_TF = 256
N_FT = D_FF // FFN_TF
DOWN_TILES = 2

C_QLAT = 0
C_KVLAT = C_QLAT + MLA_Q_LORA
C_KPE = C_KVLAT + MLA_KV_LORA
C_QD = C_KPE + LANES
C_KD = C_QD + DIL_WIDTH
C_VD = C_KD + DIL_WIDTH
IN_COLS_P = C_VD + DIL_WIDTH


def _dot(a, b):
    return jnp.dot(a, b, preferred_element_type=F32)


def _const_spec(shape):
    nd = len(shape)
    return pl.BlockSpec(shape, lambda *_: (0,) * nd, pipeline_mode=pl.Buffered(1))


def _split_bf16(v):
    hi = v.astype(BF16)
    lo = (v - hi.astype(F32)).astype(BF16)
    return hi, lo


def _ada_kernel(c_ref, w_ref, b_ref, o_ref):
    c = c_ref[...]
    a = c * (1.0 / (1.0 + jnp.exp(-c)))
    a_hi, a_lo = _split_bf16(a)
    w_hi, w_lo = _split_bf16(w_ref[...])
    o_ref[...] = _dot(a_hi, w_hi) + _dot(a_lo, w_hi) + _dot(a_hi, w_lo) + b_ref[...]


def _ada(c, w_ada, b_ada):
    B = c.shape[0]
    n = w_ada.shape[1]
    return pl.pallas_call(
        _ada_kernel,
        out_shape=jax.ShapeDtypeStruct((B, n), F32),
        grid=(n // ADA_TN,),
        in_specs=[pl.BlockSpec((B, D_MODEL), lambda j: (0, 0)),
                  pl.BlockSpec((D_MODEL, ADA_TN), lambda j: (0, j)),
                  pl.BlockSpec((1, ADA_TN), lambda j: (0, j))],
        out_specs=pl.BlockSpec((B, ADA_TN), lambda j: (0, j)),
        compiler_params=pltpu.CompilerParams(dimension_semantics=("arbitrary",),
                                             vmem_limit_bytes=VMEM_LIMIT),
        name="ada",
    )(c, w_ada, b_ada)


def _rms_rows(v, width):
    ms = jnp.sum(v * v, axis=-1, keepdims=True) * (1.0 / width)
    return v * lax.rsqrt(ms + EPS)


def _seg_rms(v, seg_ref, seg_width):
    outs = []
    for c0 in range(0, v.shape[1], MXU_DIM):
        blk = v[:, c0:c0 + MXU_DIM]
        ss = _dot((blk * blk).astype(BF16), seg_ref[...])
        outs.append(blk * lax.rsqrt(ss * (1.0 / seg_width) + EPS))
    return outs[0] if len(outs) == 1 else jnp.concatenate(outs, axis=1)


def _rope(v, cos, sin_signed):
    outs = []
    for c0 in range(0, v.shape[1], LANES):
        blk = v[:, c0:c0 + LANES]
        outs.append(blk * cos + pltpu.roll(blk, LANES // 2, 1) * sin_signed)
    return outs[0] if len(outs) == 1 else jnp.concatenate(outs, axis=1)


GAIN_ROWS = (("mix", D_MODEL), ("q_lat", MLA_Q_LORA), ("kv_lat", MLA_KV_LORA),
             ("q_nope", N_HEADS * MLA_NOPE), ("q_pe", N_HEADS * MLA_ROPE), ("k_nope", N_HEADS * MLA_NOPE),
             ("k_pe", LANES), ("dil_q", DIL_WIDTH), ("dil_k", DIL_WIDTH))


class _Rows:
    def __init__(self, ref):
        self._ref = ref

    def __getattr__(self, name):
        idx, width = next((i, w) for i, (n, w) in enumerate(GAIN_ROWS) if n == name)
        return self._ref[idx:idx + 1, :width]


def _proj_kernel(x_ref, pos_ref, mod_ref, gains_ref, win_ref, wqb_ref, wkvb_ref,
                 seg64_ref, segpe_ref, segpair_ref, freq_ref, expand_ref,
                 qn_ref, qpe_ref, kn_ref, kpe_ref, vm_ref, qd_ref, kd_ref, vd_ref):
    g = _Rows(gains_ref)
    x = x_ref[0]
    sh1 = mod_ref[0, 0:1, :]
    sc1 = mod_ref[0, 1:2, :]
    h = (_rms_rows(x, D_MODEL) * (g.mix * (1.0 + sc1)) + sh1).astype(BF16)

    pos = pos_ref[0].astype(F32)
    ang = freq_ref[...] * pos
    cs = jnp.concatenate([jnp.cos(ang), jnp.sin(ang)], axis=0).T
    cs_hi, cs_lo = _split_bf16(cs)
    tabs = _dot(cs_hi, expand_ref[...]) + _dot(cs_lo, expand_ref[...])
    cos_d, sin_d = tabs[:, 0:LANES], tabs[:, LANES:2 * LANES]
    cos_p, sin_p = tabs[:, 2 * LANES:3 * LANES], tabs[:, 3 * LANES:4 * LANES]

    q_lat = _dot(h, win_ref[:, C_QLAT:C_QLAT + MLA_Q_LORA])
    q_in = (_rms_rows(q_lat, MLA_Q_LORA) * g.q_lat).astype(BF16)
    q = _dot(q_in, wqb_ref[...])
    n_nope = N_HEADS * MLA_NOPE
    qn_ref[0] = (_seg_rms(q[:, :n_nope], seg64_ref, MLA_NOPE) * g.q_nope).astype(BF16)
    q_pe = _seg_rms(q[:, n_nope:], segpe_ref, MLA_ROPE) * g.q_pe
    qpe_ref[0] = _rope(q_pe, cos_p, sin_p).astype(BF16)

    kv_lat = _dot(h, win_ref[:, C_KVLAT:C_KVLAT + MLA_KV_LORA])
    kv_in = (_rms_rows(kv_lat, MLA_KV_LORA) * g.kv_lat).astype(BF16)
    kv = _dot(kv_in, wkvb_ref[...])
    kn_ref[0] = (_seg_rms(kv[:, :n_nope], seg64_ref, MLA_NOPE) * g.k_nope).astype(BF16)
    vm_ref[0] = kv[:, n_nope:].astype(BF16)
    k_pe = _dot(h, win_ref[:, C_KPE:C_KPE + LANES])
    kpe_ref[0] = _rope(_rms_rows(k_pe, LANES) * g.k_pe, cos_p, sin_p).astype(BF16)

    qd = _seg_rms(_dot(h, win_ref[:, C_QD:C_QD + DIL_WIDTH]), segpair_ref, DIL_HEAD_DIM)
    qd_ref[0] = _rope(qd * g.dil_q, cos_d, sin_d).astype(BF16)
    kd = _seg_rms(_dot(h, win_ref[:, C_KD:C_KD + DIL_WIDTH]), segpair_ref, DIL_HEAD_DIM)
    kd_ref[0] = _rope(kd * g.dil_k, cos_d, sin_d).astype(BF16)
    vd_ref[0] = _dot(h, win_ref[:, C_VD:C_VD + DIL_WIDTH]).astype(BF16)


def _proj(x, pos3, mod3, consts):
    B, S, _ = x.shape
    tm = PROJ_TM
    row = lambda w: pl.BlockSpec((1, tm, w), lambda b, i: (b, i, 0))
    out_widths = (512, 256, 512, LANES, 512, 512, 512, 512)
    in_specs = [row(D_MODEL),
                pl.BlockSpec((1, 1, tm), lambda b, i: (b, 0, i)),
                pl.BlockSpec((1, 6, D_MODEL), lambda b, i: (b, 0, 0))]
    in_specs += [_const_spec(a.shape) for a in consts]
    return pl.pallas_call(
        _proj_kernel,
        out_shape=tuple(jax.ShapeDtypeStruct((B, S, w), BF16) for w in out_widths),
        grid=(B, S // tm),
        in_specs=in_specs,
        out_specs=tuple(row(w) for w in out_widths),
        compiler_params=pltpu.CompilerParams(dimension_semantics=("arbitrary", "arbitrary"),
                                             vmem_limit_bytes=VMEM_LIMIT),
        name="proj",
    )(x, pos3, mod3, *consts)


def _row_iota(shape):
    return lax.broadcasted_iota(jnp.int32, shape, 0)


def _sublane_allreduce(v, op):
    for shift in (4, 2, 1):
        v = op(v, pltpu.roll(v, shift, 0))
    return v


def _transpose_bf16(v):
    return v.astype(F32).T.astype(BF16)


class _Stream:
    def __init__(self, load_qt2, load_k, load_v, store_o):
        self.load_qt2, self.load_k, self.load_v, self.store_o = load_qt2, load_k, load_v, store_o


def _pair_attention(streams, n_rows, bias_ref, far_bias, vt_scr, qt_scr, s_scr, m_scr, l_scr, acc_scr):
    tq, tk = ATT_TQ, ATT_TK
    assert tq == tk
    n_q = n_rows // tq
    hd = LANES // 2
    ids = range(len(streams))

    def put_queries(i):
        for s in ids:
            qt_scr[s] = streams[s].load_qt2(pl.ds(i * tq, tq))

    def put_scores(j, slot):
        k_rows = pl.ds(j * tk, tk)
        for s in ids:
            s_scr[s, slot] = _dot(streams[s].load_k(k_rows), qt_scr[s])

    def consume(j, slot, bias_idx):
        for s in ids:
            st = s_scr[s, slot]
            if bias_idx is not None:
                st = st + bias_ref[bias_idx]
            s3 = st.reshape(tk // 8, 8, 2 * tq)
            m_new = _sublane_allreduce(jnp.max(s3, axis=0), jnp.maximum)
            if j > 0:
                m_old = m_scr[s]
                m_new = jnp.maximum(m_old, m_new)
                alpha = jnp.exp2(m_old - m_new)
            p3 = jnp.exp2(s3 - m_new[None])
            l_new = _sublane_allreduce(jnp.sum(p3, axis=0), jnp.add)
            l_scr[s] = l_new if j == 0 else alpha * l_scr[s] + l_new
            m_scr[s] = m_new
            pt = p3.reshape(tk, 2 * tq).astype(BF16)
            vt = vt_scr[s, j]
            for h in range(2):
                pv = _dot(vt[h * hd:(h + 1) * hd, :], pt[:, h * tq:(h + 1) * tq])
                if j > 0:
                    a3 = alpha[:, h * tq:(h + 1) * tq][None]
                    pv = (acc_scr[s, h].reshape(hd // 8, 8, tq) * a3).reshape(hd, tq) + pv
                acc_scr[s, h] = pv

    def step(j, slot, bias_idx):
        put_scores(j + 1, 1 - slot)
        consume(j, slot, bias_idx)

    for s in ids:
        for j in range(n_rows // tk):
            vt_scr[s, j] = _transpose_bf16(streams[s].load_v(pl.ds(j * tk, tk)))
    put_queries(0)
    put_scores(0, 0)

    t = 0
    for i in range(n_q):
        for j in range(i):
            step(j, t % 2, None if far_bias is None else min(i - j, far_bias))
            t += 1
        put_queries(min(i + 1, n_q - 1))
        put_scores(0, 1 - t % 2)
        consume(i, t % 2, 0)
        t += 1

        for s in ids:
            inv_l = 1.0 / l_scr[s]
            ot = jnp.concatenate(
                [(acc_scr[s, h].reshape(hd // 8, 8, tq) * inv_l[:, h * tq:(h + 1) * tq][None]).reshape(hd, tq)
                 for h in range(2)], axis=0)
            streams[s].store_o(pl.ds(i * tq, tq), ot.T)


def _lane_block(ref, s):
    cols = slice(s * LANES, (s + 1) * LANES)
    return lambda rows: ref[0, rows, cols]


def _mla_kernel(qn_ref, qpe_ref, kn_ref, kpe_ref, v_ref, bias_ref, o_ref, *scratch):
    feat = _row_iota((2 * LANES, ATT_TQ))
    pe_head = (feat % (LANES // 2)) // (MLA_ROPE // 2)
    is_pe = feat >= LANES
    n_streams = o_ref.shape[2] // LANES
    assert n_streams == 2

    def make(s):
        mask_a = (feat < LANES // 2) | (is_pe & (pe_head == 2 * s))
        mask_b = ((feat >= LANES // 2) & (feat < LANES)) | (is_pe & (pe_head == 2 * s + 1))
        qn, kn = _lane_block(qn_ref, s), _lane_block(kn_ref, s)

        def load_qt2(rows):
            qt = jnp.concatenate([qn(rows), qpe_ref[0, rows, :]], axis=1).astype(F32).T
            z = jnp.zeros_like(qt)
            return jnp.concatenate([jnp.where(mask_a, qt, z), jnp.where(mask_b, qt, z)],
                                   axis=1).astype(BF16)

        def load_k(rows):
            return jnp.concatenate([kn(rows), kpe_ref[0, rows, :]], axis=1)

        def store_o(rows, val):
            o_ref[0, rows, s * LANES:(s + 1) * LANES] = val.astype(o_ref.dtype)

        return _Stream(load_qt2, load_k, _lane_block(v_ref, s), store_o)

    _pair_attention([make(s) for s in range(n_streams)], o_ref.shape[1], bias_ref, None, *scratch)


def _dil_kernel(q_ref, k_ref, v_ref, bias_ref, o_ref, *scratch):
    feat = _row_iota((LANES, ATT_TQ))
    is_a = (feat % (LANES // 2)) < DIL_HEAD_DIM // 2

    def make(s):
        q = _lane_block(q_ref, s)

        def load_qt2(rows):
            qt = q(rows).astype(F32).T
            z = jnp.zeros_like(qt)
            return jnp.concatenate([jnp.where(is_a, qt, z), jnp.where(is_a, z, qt)],
                                   axis=1).astype(BF16)

        def store_o(rows, val):
            o_ref[0, rows, s * LANES:(s + 1) * LANES] = val.astype(o_ref.dtype)

        return _Stream(load_qt2, _lane_block(k_ref, s), _lane_block(v_ref, s), store_o)

    _pair_attention([make(s) for s in range(o_ref.shape[2] // LANES)], o_ref.shape[1], bias_ref,
                    bias_ref.shape[0] - 1, *scratch)


def _attn_call(kernel, name, arrays, widths, qk_width, bias, B, S):
    n_s = ATT_PAIRS
    col = lambda w: pl.BlockSpec((1, S, w), lambda b, g: (b, 0, g))
    shared = pl.BlockSpec((1, S, LANES), lambda b, g: (b, 0, 0))
    in_specs = [shared if w is None else col(w) for w in widths] + [_const_spec(bias.shape)]
    return pl.pallas_call(
        kernel,
        out_shape=jax.ShapeDtypeStruct((B, S, N_HEADS * MLA_V), BF16),
        grid=(B, N_HEADS // 2 // n_s),
        in_specs=in_specs,
        out_specs=col(n_s * LANES),
        scratch_shapes=[pltpu.VMEM((n_s, S // ATT_TK, LANES, ATT_TK), BF16),
                        pltpu.VMEM((n_s, qk_width, 2 * ATT_TQ), BF16),
                        pltpu.VMEM((n_s, 2, ATT_TK, 2 * ATT_TQ), F32),
                        pltpu.VMEM((n_s, 8, 2 * ATT_TQ), F32),
                        pltpu.VMEM((n_s, 8, 2 * ATT_TQ), F32),
                        pltpu.VMEM((n_s, 2, LANES // 2, ATT_TQ), F32)],
        compiler_params=pltpu.CompilerParams(dimension_semantics=("arbitrary", "arbitrary"),
                                             vmem_limit_bytes=VMEM_LIMIT),
        name=name,
    )(*arrays, bias)


def _two_heads(tile_qk):
    return np.concatenate([tile_qk.T, tile_qk.T], axis=1)


def _mla_bias():
    d = np.arange(ATT_TQ)[:, None] - np.arange(ATT_TK)[None, :]
    tile = np.where(d >= 0, 0.0, MASK_BIAS).astype(np.float32)
    return jnp.asarray(_two_heads(tile)[None])


def _dil_bias():
    (far_window, far_dil), = [(w, d) for w, d in DIL_PATTERNS if w == max(p[0] for p in DIL_PATTERNS)]
    assert ATT_TK % far_dil == 0
    max_near = max(w for w, d in DIL_PATTERNS if w < far_window)
    n_tiles = -(-max_near // ATT_TK) + 2
    tiles = []
    for blk in range(n_tiles):
        delta = blk * ATT_TK + np.arange(ATT_TQ)[:, None] - np.arange(ATT_TK)[None, :]
        mult = np.zeros(delta.shape, np.int64)
        for window, dil in DIL_PATTERNS:
            mult += (delta >= 0) & (delta % dil == 0) & (delta <= window)
        tile = np.where(mult > 0, np.log2(np.maximum(mult, 1)), MASK_BIAS).astype(np.float32)
        tiles.append(_two_heads(tile))
    return jnp.asarray(np.stack(tiles))


def _ffn_kernel(x_ref, om_ref, od_ref, mod_ref, gffn_ref, wo_ref, wup_ref, wconv_ref, bconv_ref,
                wdown_ref, out_ref, carry_scr, acc_scr, h2_scr, y_scr, a_scr):
    g1 = mod_ref[0, 2:3, :]
    sh2 = mod_ref[0, 3:4, :]
    sc2 = mod_ref[0, 4:5, :]
    g2 = mod_ref[0, 5:6, :]
    half = wo_ref.shape[0] // 2
    mix = _dot(om_ref[0], wo_ref[:half, :]) + _dot(od_ref[0], wo_ref[half:, :])
    x1 = x_ref[0] + g1 * mix
    h2_scr[...] = (_rms_rows(x1, D_MODEL) * (gffn_ref[...] * (1.0 + sc2)) + sh2).astype(BF16)
    out_ref[0] = x1

    @pl.when(pl.program_id(1) == 0)
    def _():
        carry_scr[...] = jnp.zeros(carry_scr.shape, F32)

    def up(f):
        for half_idx, t in enumerate((f, f + N_FT)):
            y = _dot(h2_scr[...], wup_ref[:, t * FFN_TF:(t + 1) * FFN_TF])
            y_scr[f % 2, half_idx, 0:8, :] = carry_scr[t]
            y_scr[f % 2, half_idx, 8:, :] = y
            carry_scr[t] = y[FFN_TM - 8:, :]

    def conv(f, half_idx):
        t = f + half_idx * N_FT
        yb = y_scr.at[f % 2, half_idx]
        cols = slice(t * FFN_TF, (t + 1) * FFN_TF)
        w = wconv_ref[:, cols]
        return (w[2:3, :] * yb[8:8 + FFN_TM, :] + w[1:2, :] * yb[7:7 + FFN_TM, :]
                + w[0:1, :] * yb[6:6 + FFN_TM, :] + bconv_ref[:, cols])

    def down(f0, f1):
        d = _dot(a_scr[(f0 // DOWN_TILES) % 2, :, :(f1 - f0) * FFN_TF],
                 wdown_ref[f0 * FFN_TF:f1 * FFN_TF, :])
        if f0 == 0:
            acc_scr[...] = d
        else:
            acc_scr[...] += d

    up(0)
    pending = None
    for f in range(N_FT):
        if f + 1 < N_FT:
            up(f + 1)
        if pending is not None:
            down(*pending)
            pending = None
        gate = conv(f, 0)
        val = conv(f, 1)
        k = f % DOWN_TILES
        a_scr[(f // DOWN_TILES) % 2, :, k * FFN_TF:(k + 1) * FFN_TF] = (
            gate * (1.0 / (1.0 + jnp.exp(-gate))) * val).astype(BF16)
        if k == DOWN_TILES - 1 or f == N_FT - 1:
            pending = (f - k, f + 1)
    down(*pending)
    out_ref[0] = out_ref[0] + g2 * acc_scr[...]


def _ffn(x, o_mla, o_dil, mod3, g_ffn, w_o, w_up, w_conv, b_conv, w_down):
    B, S, _ = x.shape
    tm = FFN_TM
    row = lambda w: pl.BlockSpec((1, tm, w), lambda b, i: (b, i, 0))
    consts = (g_ffn, w_o, w_up, w_conv, b_conv, w_down)
    return pl.pallas_call(
        _ffn_kernel,
        out_shape=jax.ShapeDtypeStruct(x.shape, F32),
        grid=(B, S // tm),
        in_specs=[row(D_MODEL), row(o_mla.shape[2]), row(o_dil.shape[2]),
                  pl.BlockSpec((1, 6, D_MODEL), lambda b, i: (b, 0, 0))]
                 + [_const_spec(a.shape) for a in consts],
        out_specs=row(D_MODEL),
        scratch_shapes=[pltpu.VMEM((2 * N_FT, 8, FFN_TF), F32), pltpu.VMEM((tm, D_MODEL), F32),
                        pltpu.VMEM((tm, D_MODEL), BF16),
                        pltpu.VMEM((2, 2, tm + 8, FFN_TF), F32),
                        pltpu.VMEM((2, tm, DOWN_TILES * FFN_TF), BF16)],
        compiler_params=pltpu.CompilerParams(dimension_semantics=("arbitrary", "arbitrary"),
                                             vmem_limit_bytes=VMEM_LIMIT),
        name="ffn",
    )(x, o_mla, o_dil, mod3, *consts)


def _split_halves_layout(w, n_groups, heads, half):
    lead = w.shape[:-1]
    w = w.reshape(*lead, n_groups, heads, 2, half)
    return jnp.swapaxes(w, -3, -2).reshape(*lead, n_groups * heads * 2 * half)


def _segment_matrices():
    i = np.arange(MXU_DIM)
    lane = i % LANES
    seg64 = i // 64
    segpe = (i // LANES) * 4 + (lane % 64) // (MLA_ROPE // 2)
    segpair = (i // LANES) * 2 + (lane % 64) // (DIL_HEAD_DIM // 2)
    same = lambda s: jnp.asarray((s[:, None] == s[None, :]).astype(np.float32), dtype=BF16)
    return same(seg64), same(segpe), same(segpair)


def _rope_constants():
    def inv_freq(d):
        half = d // 2
        return jnp.power(ROPE_THETA, -2.0 * jnp.arange(half, dtype=F32) / d)
    fd, fp = inv_freq(DIL_HEAD_DIM), inv_freq(MLA_ROPE)
    nd, npe = fd.shape[0], fp.shape[0]
    freq = jnp.concatenate([fd, fp, jnp.zeros((64 - nd - npe,), F32)])[:, None]
    lane = np.arange(LANES)
    sign = np.where(lane < 64, -1.0, 1.0)
    e = np.zeros((2 * 64, 4 * LANES), np.float32)
    e[lane % 32, lane] = 1.0
    e[64 + lane % 32, LANES + lane] = sign
    e[nd + lane % 16, 2 * LANES + lane] = 1.0
    e[64 + nd + lane % 16, 3 * LANES + lane] = sign
    return freq, jnp.asarray(e, dtype=BF16)


def kernel(x, c, positions, w_ada, b_ada, g_mix_norm, w_in, g_q_lat, w_q_b, g_kv_lat, w_kv_b,
           g_mla_q_nope, g_mla_q_pe, g_mla_k_nope, g_mla_k_pe, g_dil_q, g_dil_k, w_o,
           g_ffn_norm, w_up, w_conv, b_conv, w_down):
    B, S, D = x.shape
    assert D == D_MODEL and S % PROJ_TM == 0 and S % ATT_TQ == 0 and S % FFN_TM == 0
    assert S <= max(w for w, _ in DIL_PATTERNS)
    seg64, segpe, segpair = _segment_matrices()
    freq, expand = _rope_constants()
    mla_scale = (MLA_NOPE + MLA_ROPE) ** -0.5 * LOG2E
    dil_scale = DIL_HEAD_DIM ** -0.5 * LOG2E
    pos3 = positions.reshape(B, 1, S)

    for l in range(w_ada.shape[0]):
        mod3 = _ada(c, w_ada[l], b_ada[l][None, :]).reshape(B, 6, D)

        wi = w_in[l]
        c_kpe = MLA_Q_LORA + MLA_KV_LORA
        c_qd = c_kpe + MLA_ROPE
        kpe4 = jnp.broadcast_to(wi[:, c_kpe:c_qd].reshape(D, 2, 1, MLA_ROPE // 2),
                                (D, 2, 4, MLA_ROPE // 2)).reshape(D, LANES)
        dil = lambda w: _split_halves_layout(w, 4, 2, DIL_HEAD_DIM // 2)
        w_in_p = jnp.concatenate([
            wi[:, :c_kpe], kpe4, dil(wi[:, c_qd:c_qd + DIL_WIDTH]),
            dil(wi[:, c_qd + DIL_WIDTH:c_qd + 2 * DIL_WIDTH]), wi[:, c_qd + 2 * DIL_WIDTH:]],
            axis=1).astype(BF16)
        wq = w_q_b[l].reshape(MLA_Q_LORA, N_HEADS, MLA_NOPE + MLA_ROPE)
        pe4 = lambda w: _split_halves_layout(w, w.shape[-1] // LANES, 4, MLA_ROPE // 2)
        w_qb_p = jnp.concatenate([wq[:, :, :MLA_NOPE].reshape(MLA_Q_LORA, -1),
                                  pe4(wq[:, :, MLA_NOPE:].reshape(MLA_Q_LORA, -1))], axis=1).astype(BF16)
        wkv = w_kv_b[l].reshape(MLA_KV_LORA, N_HEADS, MLA_NOPE + MLA_V)
        w_kvb_p = jnp.concatenate([wkv[:, :, :MLA_NOPE].reshape(MLA_KV_LORA, -1),
                                   wkv[:, :, MLA_NOPE:].reshape(MLA_KV_LORA, -1)], axis=1).astype(BF16)
        gains = {
            "mix": g_mix_norm[l], "q_lat": g_q_lat[l], "kv_lat": g_kv_lat[l],
            "q_nope": jnp.tile(g_mla_q_nope[l], N_HEADS) * mla_scale,
            "q_pe": pe4(jnp.tile(g_mla_q_pe[l], N_HEADS)) * mla_scale,
            "k_nope": jnp.tile(g_mla_k_nope[l], N_HEADS),
            "k_pe": pe4(jnp.tile(g_mla_k_pe[l], 4)),
            "dil_q": dil(jnp.tile(g_dil_q[l], N_HEADS)) * dil_scale,
            "dil_k": dil(jnp.tile(g_dil_k[l], N_HEADS))}
        gains = jnp.stack([jnp.pad(gains[n], (0, D - w)) for n, w in GAIN_ROWS])
        consts = (gains, w_in_p, w_qb_p, w_kvb_p, seg64, segpe, segpair, freq, expand)
        qn, qpe, kn, kpe, vm, qd, kd, vd = _proj(x, pos3, mod3, consts)

        two = ATT_PAIRS * LANES
        o_mla = _attn_call(_mla_kernel, "mla", (qn, qpe, kn, kpe, vm), (two, LANES, two, None, two),
                           2 * LANES, _mla_bias(), B, S)
        o_dil = _attn_call(_dil_kernel, "dil", (qd, kd, vd), (two, two, two), LANES, _dil_bias(), B, S)

        x = _ffn(x, o_mla, o_dil, mod3, g_ffn_norm[l][None, :], w_o[l].astype(BF16),
                 w_up[l].astype(BF16), w_conv[l], b_conv[l][None, :], w_down[l].astype(BF16))
    return x
```

```python
import functools

import numpy as np
import jax
import jax.numpy as jnp
from jax import lax
from jax.experimental import pallas as pl
from jax.experimental.pallas import tpu as pltpu

F32 = jnp.float32
BF16 = jnp.bfloat16

D_MODEL = 1024
N_HEADS = 8
MLA_NOPE = 64
MLA_ROPE = 32
MLA_V = 64
MLA_Q_LORA = 512
MLA_KV_LORA = 256
DIL_HEAD_DIM = 64
DIL_WIDTH = N_HEADS * DIL_HEAD_DIM
DIL_PATTERNS = ((128, 1), (512, 4), (2048, 16))
D_FF = 2816
ROPE_THETA = 10000.0
EPS = 1e-6
MASK_BIAS = -1e30
LOG2E = 1.4426950408889634

LANES = 128
MXU_DIM = 256
VMEM_LIMIT = 56 * 1024 * 1024
ADA_TN = 1024
PROJ_TM = 1024
ATT_TQ = 256
ATT_TK = 256
VT_ROWS = 64 + 16
ATT_PAIRS = 2
FFN_TM = 512
FFN_TF = 256
N_FT = D_FF // FFN_TF
DOWN_TILES = 2

C_QLAT = 0
C_KVLAT = C_QLAT + MLA_Q_LORA
C_KPE = C_KVLAT + MLA_KV_LORA
C_QD = C_KPE + LANES
C_KD = C_QD + DIL_WIDTH
C_VD = C_KD + DIL_WIDTH
IN_COLS_P = C_VD + DIL_WIDTH


def _dot(a, b):
    return jnp.dot(a, b, preferred_element_type=F32)


def _const_spec(shape):
    nd = len(shape)
    return pl.BlockSpec(shape, lambda *_: (0,) * nd, pipeline_mode=pl.Buffered(1))


def _split_bf16(v):
    hi = v.astype(BF16)
    lo = (v - hi.astype(F32)).astype(BF16)
    return hi, lo


def _ada_kernel(c_ref, w_ref, b_ref, o_ref):
    c = c_ref[...]
    a = c * (1.0 / (1.0 + jnp.exp(-c)))
    a_hi, a_lo = _split_bf16(a)
    w_hi, w_lo = _split_bf16(w_ref[...])
    o_ref[...] = _dot(a_hi, w_hi) + _dot(a_lo, w_hi) + _dot(a_hi, w_lo) + b_ref[...]


def _ada(c, w_ada, b_ada):
    B = c.shape[0]
    n = w_ada.shape[1]
    return pl.pallas_call(
        _ada_kernel,
        out_shape=jax.ShapeDtypeStruct((B, n), F32),
        grid=(n // ADA_TN,),
        in_specs=[pl.BlockSpec((B, D_MODEL), lambda j: (0, 0)),
                  pl.BlockSpec((D_MODEL, ADA_TN), lambda j: (0, j)),
                  pl.BlockSpec((1, ADA_TN), lambda j: (0, j))],
        out_specs=pl.BlockSpec((B, ADA_TN), lambda j: (0, j)),
        compiler_params=pltpu.CompilerParams(dimension_semantics=("arbitrary",),
                                             vmem_limit_bytes=VMEM_LIMIT),
        name="ada",
    )(c, w_ada, b_ada)


def _rms_rows(v, width):
    ms = jnp.sum(v * v, axis=-1, keepdims=True) * (1.0 / width)
    return v * lax.rsqrt(ms + EPS)


def _seg_rms(v, seg_ref, seg_width):
    outs = []
    for c0 in range(0, v.shape[1], MXU_DIM):
        blk = v[:, c0:c0 + MXU_DIM]
        ss = _dot((blk * blk).astype(BF16), seg_ref[...])
        outs.append(blk * lax.rsqrt(ss * (1.0 / seg_width) + EPS))
    return outs[0] if len(outs) == 1 else jnp.concatenate(outs, axis=1)


def _rope(v, cos, sin_signed):
    outs = []
    for c0 in range(0, v.shape[1], LANES):
        blk = v[:, c0:c0 + LANES]
        outs.append(blk * cos + pltpu.roll(blk, LANES // 2, 1) * sin_signed)
    return outs[0] if len(outs) == 1 else jnp.concatenate(outs, axis=1)


GAIN_ROWS = (("mix", D_MODEL), ("q_lat", MLA_Q_LORA), ("kv_lat", MLA_KV_LORA),
             ("q_nope", N_HEADS * MLA_NOPE), ("q_pe", N_HEADS * MLA_ROPE), ("k_nope", N_HEADS * MLA_NOPE),
             ("k_pe", LANES), ("dil_q", DIL_WIDTH), ("dil_k", DIL_WIDTH))


class _Rows:
    def __init__(self, ref):
        self._ref = ref

    def __getattr__(self, name):
        idx, width = next((i, w) for i, (n, w) in enumerate(GAIN_ROWS) if n == name)
        return self._ref[idx:idx + 1, :width]


def _proj_kernel(x_ref, pos_ref, mod_ref, gains_ref, win_ref, wqb_ref, wkvb_ref,
                 seg64_ref, segpe_ref, segpair_ref, freq_ref, expand_ref,
                 qn_ref, qpe_ref, kn_ref, kpe_ref, vm_ref, qd_ref, kd_ref, vd_ref):
    g = _Rows(gains_ref)
    x = x_ref[0]
    sh1 = mod_ref[0, 0:1, :]
    sc1 = mod_ref[0, 1:2, :]
    h = (_rms_rows(x, D_MODEL) * (g.mix * (1.0 + sc1)) + sh1).astype(BF16)

    q_lat = _dot(h, win_ref[:, C_QLAT:C_QLAT + MLA_Q_LORA])
    kv_lat = _dot(h, win_ref[:, C_KVLAT:C_KVLAT + MLA_KV_LORA])
    qd_raw = _dot(h, win_ref[:, C_QD:C_QD + DIL_WIDTH])

    pos = pos_ref[0].astype(F32)
    ang = freq_ref[...] * pos
    cs = jnp.concatenate([jnp.cos(ang), jnp.sin(ang)], axis=0).T
    cs_hi, cs_lo = _split_bf16(cs)
    tabs = _dot(cs_hi, expand_ref[...]) + _dot(cs_lo, expand_ref[...])
    cos_d, sin_d = tabs[:, 0:LANES], tabs[:, LANES:2 * LANES]
    cos_p, sin_p = tabs[:, 2 * LANES:3 * LANES], tabs[:, 3 * LANES:4 * LANES]

    q_in = (_rms_rows(q_lat, MLA_Q_LORA) * g.q_lat).astype(BF16)
    q = _dot(q_in, wqb_ref[...])
    kv_in = (_rms_rows(kv_lat, MLA_KV_LORA) * g.kv_lat).astype(BF16)
    kv = _dot(kv_in, wkvb_ref[...])
    kd_raw = _dot(h, win_ref[:, C_KD:C_KD + DIL_WIDTH])
    qd_ref[0] = _rope(_seg_rms(qd_raw, segpair_ref, DIL_HEAD_DIM) * g.dil_q, cos_d, sin_d).astype(BF16)
    k_pe = _dot(h, win_ref[:, C_KPE:C_KPE + LANES])
    vd_ref[0] = _dot(h, win_ref[:, C_VD:C_VD + DIL_WIDTH]).astype(BF16)

    n_nope = N_HEADS * MLA_NOPE
    qn_ref[0] = (_seg_rms(q[:, :n_nope], seg64_ref, MLA_NOPE) * g.q_nope).astype(BF16)
    q_pe = _seg_rms(q[:, n_nope:], segpe_ref, MLA_ROPE) * g.q_pe
    qpe_ref[0] = _rope(q_pe, cos_p, sin_p).astype(BF16)
    kn_ref[0] = (_seg_rms(kv[:, :n_nope], seg64_ref, MLA_NOPE) * g.k_nope).astype(BF16)
    vm_ref[0] = kv[:, n_nope:].astype(BF16)
    kd_ref[0] = _rope(_seg_rms(kd_raw, segpair_ref, DIL_HEAD_DIM) * g.dil_k, cos_d, sin_d).astype(BF16)
    kpe_ref[0] = _rope(_rms_rows(k_pe, LANES) * g.k_pe, cos_p, sin_p).astype(BF16)


def _proj(x, pos3, mod3, consts):
    B, S, _ = x.shape
    tm = PROJ_TM
    row = lambda w: pl.BlockSpec((1, tm, w), lambda b, i: (b, i, 0))
    out_widths = (512, 256, 512, LANES, 512, 512, 512, 512)
    in_specs = [row(D_MODEL),
                pl.BlockSpec((1, 1, tm), lambda b, i: (b, 0, i)),
                pl.BlockSpec((1, 6, D_MODEL), lambda b, i: (b, 0, 0))]
    in_specs += [_const_spec(a.shape) for a in consts]
    return pl.pallas_call(
        _proj_kernel,
        out_shape=tuple(jax.ShapeDtypeStruct((B, S, w), BF16) for w in out_widths),
        grid=(B, S // tm),
        in_specs=in_specs,
        out_specs=tuple(row(w) for w in out_widths),
        compiler_params=pltpu.CompilerParams(dimension_semantics=("arbitrary", "arbitrary"),
                                             vmem_limit_bytes=VMEM_LIMIT),
        name="proj",
    )(x, pos3, mod3, *consts)


def _row_iota(shape):
    return lax.broadcasted_iota(jnp.int32, shape, 0)


def _sublane_allreduce(v, op):
    for shift in (4, 2, 1):
        v = op(v, pltpu.roll(v, shift, 0))
    return v


def _transpose_bf16(v):
    return v.astype(F32).T.astype(BF16)


class _Stream:
    def __init__(self, load_qt2, load_k, load_v, store_o):
        self.load_qt2, self.load_k, self.load_v, self.store_o = load_qt2, load_k, load_v, store_o


def _pair_attention(streams, n_rows, bias_ref, far_bias, vt_scr, qt_scr, s_scr, m_scr, l_scr, acc_scr):
    tq, tk = ATT_TQ, ATT_TK
    assert tq == tk
    n_q = n_rows // tq
    hd = LANES // 2
    ids = range(len(streams))

    def put_queries(i):
        for s in ids:
            qt_scr[s] = streams[s].load_qt2(pl.ds(i * tq, tq))

    def put_scores(j, slot):
        k_rows = pl.ds(j * tk, tk)
        for s in ids:
            s_scr[s, slot] = _dot(streams[s].load_k(k_rows), qt_scr[s])

    def consume(j, slot, bias_idx):
        for s in ids:
            st = s_scr[s, slot]
            if bias_idx is not None:
                st = st + bias_ref[bias_idx]
            s3 = st.reshape(tk // 8, 8, 2 * tq)
            m_new = _sublane_allreduce(jnp.max(s3, axis=0), jnp.maximum)
            if j > 0:
                m_old = m_scr[s]
                m_new = jnp.maximum(m_old, m_new)
                alpha = jnp.exp2(m_old - m_new)
            p3 = jnp.exp2(s3 - m_new[None])
            m_scr[s] = m_new
            pt = p3.reshape(tk, 2 * tq).astype(BF16)
            vt = vt_scr[s, j]
            l_new = []
            for h in range(2):
                pv = _dot(vt[h * VT_ROWS:(h + 1) * VT_ROWS, :], pt[:, h * tq:(h + 1) * tq])
                l_new.append(pv[hd:hd + 8, :])
                pv = pv[:hd, :]
                if j > 0:
                    a3 = alpha[:, h * tq:(h + 1) * tq][None]
                    pv = (acc_scr[s, h].reshape(hd // 8, 8, tq) * a3).reshape(hd, tq) + pv
                acc_scr[s, h] = pv
            l_new = jnp.concatenate(l_new, axis=1)
            l_scr[s] = l_new if j == 0 else alpha * l_scr[s] + l_new

    def step(j, slot, bias_idx):
        put_scores(j + 1, 1 - slot)
        consume(j, slot, bias_idx)

    ones = jnp.ones((VT_ROWS - hd, tk), BF16)
    for s in ids:
        for j in range(n_rows // tk):
            vt = _transpose_bf16(streams[s].load_v(pl.ds(j * tk, tk)))
            vt_scr[s, j] = jnp.concatenate([vt[:hd], ones, vt[hd:], ones], axis=0)
    put_queries(0)
    put_scores(0, 0)

    t = 0
    for i in range(n_q):
        for j in range(i):
            step(j, t % 2, None if far_bias is None else min(i - j, far_bias))
            t += 1
        put_queries(min(i + 1, n_q - 1))
        put_scores(0, 1 - t % 2)
        consume(i, t % 2, 0)
        t += 1

        for s in ids:
            inv_l = 1.0 / l_scr[s]
            ot = jnp.concatenate(
                [(acc_scr[s, h].reshape(hd // 8, 8, tq) * inv_l[:, h * tq:(h + 1) * tq][None]).reshape(hd, tq)
                 for h in range(2)], axis=0)
            streams[s].store_o(pl.ds(i * tq, tq), ot.T)


def _lane_block(ref, s):
    cols = slice(s * LANES, (s + 1) * LANES)
    return lambda rows: ref[0, rows, cols]


def _mla_kernel(qn_ref, qpe_ref, kn_ref, kpe_ref, v_ref, bias_ref, o_ref, *scratch):
    feat = _row_iota((2 * LANES, ATT_TQ))
    pe_head = (feat % (LANES // 2)) // (MLA_ROPE // 2)
    is_pe = feat >= LANES
    n_streams = o_ref.shape[2] // LANES
    assert n_streams == 2

    def make(s):
        mask_a = (feat < LANES // 2) | (is_pe & (pe_head == 2 * s))
        mask_b = ((feat >= LANES // 2) & (feat < LANES)) | (is_pe & (pe_head == 2 * s + 1))
        qn, kn = _lane_block(qn_ref, s), _lane_block(kn_ref, s)

        def load_qt2(rows):
            qt = jnp.concatenate([qn(rows), qpe_ref[0, rows, :]], axis=1).astype(F32).T
            z = jnp.zeros_like(qt)
            return jnp.concatenate([jnp.where(mask_a, qt, z), jnp.where(mask_b, qt, z)],
                                   axis=1).astype(BF16)

        def load_k(rows):
            return jnp.concatenate([kn(rows), kpe_ref[0, rows, :]], axis=1)

        def store_o(rows, val):
            o_ref[0, rows, s * LANES:(s + 1) * LANES] = val.astype(o_ref.dtype)

        return _Stream(load_qt2, load_k, _lane_block(v_ref, s), store_o)

    _pair_attention([make(s) for s in range(n_streams)], o_ref.shape[1], bias_ref, None, *scratch)


def _dil_kernel(q_ref, k_ref, v_ref, bias_ref, o_ref, *scratch):
    feat = _row_iota((LANES, ATT_TQ))
    is_a = (feat % (LANES // 2)) < DIL_HEAD_DIM // 2

    def make(s):
        q = _lane_block(q_ref, s)

        def load_qt2(rows):
            qt = q(rows).astype(F32).T
            z = jnp.zeros_like(qt)
            return jnp.concatenate([jnp.where(is_a, qt, z), jnp.where(is_a, z, qt)],
                                   axis=1).astype(BF16)

        def store_o(rows, val):
            o_ref[0, rows, s * LANES:(s + 1) * LANES] = val.astype(o_ref.dtype)

        return _Stream(load_qt2, _lane_block(k_ref, s), _lane_block(v_ref, s), store_o)

    _pair_attention([make(s) for s in range(o_ref.shape[2] // LANES)], o_ref.shape[1], bias_ref,
                    bias_ref.shape[0] - 1, *scratch)


def _attn_call(kernel, name, arrays, widths, qk_width, bias, B, S):
    n_s = ATT_PAIRS
    col = lambda w: pl.BlockSpec((1, S, w), lambda b, g: (b, 0, g))
    shared = pl.BlockSpec((1, S, LANES), lambda b, g: (b, 0, 0))
    in_specs = [shared if w is None else col(w) for w in widths] + [_const_spec(bias.shape)]
    return pl.pallas_call(
        kernel,
        out_shape=jax.ShapeDtypeStruct((B, S, N_HEADS * MLA_V), BF16),
        grid=(B, N_HEADS // 2 // n_s),
        in_specs=in_specs,
        out_specs=col(n_s * LANES),
        scratch_shapes=[pltpu.VMEM((n_s, S // ATT_TK, 2 * VT_ROWS, ATT_TK), BF16),
                        pltpu.VMEM((n_s, qk_width, 2 * ATT_TQ), BF16),
                        pltpu.VMEM((n_s, 2, ATT_TK, 2 * ATT_TQ), F32),
                        pltpu.VMEM((n_s, 8, 2 * ATT_TQ), F32),
                        pltpu.VMEM((n_s, 8, 2 * ATT_TQ), F32),
                        pltpu.VMEM((n_s, 2, LANES // 2, ATT_TQ), F32)],
        compiler_params=pltpu.CompilerParams(dimension_semantics=("arbitrary", "arbitrary"),
                                             vmem_limit_bytes=VMEM_LIMIT),
        name=name,
    )(*arrays, bias)


def _two_heads(tile_qk):
    return np.concatenate([tile_qk.T, tile_qk.T], axis=1)


def _mla_bias():
    d = np.arange(ATT_TQ)[:, None] - np.arange(ATT_TK)[None, :]
    tile = np.where(d >= 0, 0.0, MASK_BIAS).astype(np.float32)
    return jnp.asarray(_two_heads(tile)[None])


def _dil_bias():
    (far_window, far_dil), = [(w, d) for w, d in DIL_PATTERNS if w == max(p[0] for p in DIL_PATTERNS)]
    assert ATT_TK % far_dil == 0
    max_near = max(w for w, d in DIL_PATTERNS if w < far_window)
    n_tiles = -(-max_near // ATT_TK) + 2
    tiles = []
    for blk in range(n_tiles):
        delta = blk * ATT_TK + np.arange(ATT_TQ)[:, None] - np.arange(ATT_TK)[None, :]
        mult = np.zeros(delta.shape, np.int64)
        for window, dil in DIL_PATTERNS:
            mult += (delta >= 0) & (delta % dil == 0) & (delta <= window)
        tile = np.where(mult > 0, np.log2(np.maximum(mult, 1)), MASK_BIAS).astype(np.float32)
        tiles.append(_two_heads(tile))
    return jnp.asarray(np.stack(tiles))


def _ffn_kernel(x_ref, om_ref, od_ref, mod_ref, gffn_ref, wo_ref, wup_ref, wconv_ref, bconv_ref,
                wdown_ref, out_ref, carry_scr, acc_scr, h2_scr, y_scr, a_scr):
    g1 = mod_ref[0, 2:3, :]
    sh2 = mod_ref[0, 3:4, :]
    sc2 = mod_ref[0, 4:5, :]
    g2 = mod_ref[0, 5:6, :]
    half = wo_ref.shape[0] // 2
    mix = _dot(om_ref[0], wo_ref[:half, :]) + _dot(od_ref[0], wo_ref[half:, :])
    x1 = x_ref[0] + g1 * mix
    h2_scr[...] = (_rms_rows(x1, D_MODEL) * (gffn_ref[...] * (1.0 + sc2)) + sh2).astype(BF16)
    out_ref[0] = x1

    @pl.when(pl.program_id(1) == 0)
    def _():
        carry_scr[...] = jnp.zeros(carry_scr.shape, F32)

    def up(f):
        for half_idx, t in enumerate((f, f + N_FT)):
            y = _dot(h2_scr[...], wup_ref[:, t * FFN_TF:(t + 1) * FFN_TF])
            y_scr[f % 2, half_idx, 0:8, :] = carry_scr[t]
            y_scr[f % 2, half_idx, 8:, :] = y
            carry_scr[t] = y[FFN_TM - 8:, :]

    def conv(f, half_idx):
        t = f + half_idx * N_FT
        yb = y_scr.at[f % 2, half_idx]
        cols = slice(t * FFN_TF, (t + 1) * FFN_TF)
        w = wconv_ref[:, cols]
        return (w[2:3, :] * yb[8:8 + FFN_TM, :] + w[1:2, :] * yb[7:7 + FFN_TM, :]
                + w[0:1, :] * yb[6:6 + FFN_TM, :] + bconv_ref[:, cols])

    def down(f0, f1):
        d = _dot(a_scr[(f0 // DOWN_TILES) % 2, :, :(f1 - f0) * FFN_TF],
                 wdown_ref[f0 * FFN_TF:f1 * FFN_TF, :])
        if f0 == 0:
            acc_scr[...] = d
        else:
            acc_scr[...] += d

    up(0)
    pending = None
    for f in range(N_FT):
        if f + 1 < N_FT:
            up(f + 1)
        if pending is not None:
            down(*pending)
            pending = None
        gate = conv(f, 0)
        val = conv(f, 1)
        k = f % DOWN_TILES
        a_scr[(f // DOWN_TILES) % 2, :, k * FFN_TF:(k + 1) * FFN_TF] = (
            gate * (1.0 / (1.0 + jnp.exp(-gate))) * val).astype(BF16)
        if k == DOWN_TILES - 1 or f == N_FT - 1:
            pending = (f - k, f + 1)
    down(*pending)
    out_ref[0] = out_ref[0] + g2 * acc_scr[...]


def _ffn(x, o_mla, o_dil, mod3, g_ffn, w_o, w_up, w_conv, b_conv, w_down):
    B, S, _ = x.shape
    tm = FFN_TM
    row = lambda w: pl.BlockSpec((1, tm, w), lambda b, i: (b, i, 0))
    consts = (g_ffn, w_o, w_up, w_conv, b_conv, w_down)
    return pl.pallas_call(
        _ffn_kernel,
        out_shape=jax.ShapeDtypeStruct(x.shape, F32),
        grid=(B, S // tm),
        in_specs=[row(D_MODEL), row(o_mla.shape[2]), row(o_dil.shape[2]),
                  pl.BlockSpec((1, 6, D_MODEL), lambda b, i: (b, 0, 0))]
                 + [_const_spec(a.shape) for a in consts],
        out_specs=row(D_MODEL),
        scratch_shapes=[pltpu.VMEM((2 * N_FT, 8, FFN_TF), F32), pltpu.VMEM((tm, D_MODEL), F32),
                        pltpu.VMEM((tm, D_MODEL), BF16),
                        pltpu.VMEM((2, 2, tm + 8, FFN_TF), F32),
                        pltpu.VMEM((2, tm, DOWN_TILES * FFN_TF), BF16)],
        compiler_params=pltpu.CompilerParams(dimension_semantics=("arbitrary", "arbitrary"),
                                             vmem_limit_bytes=VMEM_LIMIT),
        name="ffn",
    )(x, o_mla, o_dil, mod3, *consts)


def _split_halves_layout(w, n_groups, heads, half):
    lead = w.shape[:-1]
    w = w.reshape(*lead, n_groups, heads, 2, half)
    return jnp.swapaxes(w, -3, -2).reshape(*lead, n_groups * heads * 2 * half)


def _segment_matrices():
    i = np.arange(MXU_DIM)
    lane = i % LANES
    seg64 = i // 64
    segpe = (i // LANES) * 4 + (lane % 64) // (MLA_ROPE // 2)
    segpair = (i // LANES) * 2 + (lane % 64) // (DIL_HEAD_DIM // 2)
    same = lambda s: jnp.asarray((s[:, None] == s[None, :]).astype(np.float32), dtype=BF16)
    return same(seg64), same(segpe), same(segpair)


def _rope_constants():
    def inv_freq(d):
        half = d // 2
        return jnp.power(ROPE_THETA, -2.0 * jnp.arange(half, dtype=F32) / d)
    fd, fp = inv_freq(DIL_HEAD_DIM), inv_freq(MLA_ROPE)
    nd, npe = fd.shape[0], fp.shape[0]
    freq = jnp.concatenate([fd, fp, jnp.zeros((64 - nd - npe,), F32)])[:, None]
    lane = np.arange(LANES)
    sign = np.where(lane < 64, -1.0, 1.0)
    e = np.zeros((2 * 64, 4 * LANES), np.float32)
    e[lane % 32, lane] = 1.0
    e[64 + lane % 32, LANES + lane] = sign
    e[nd + lane % 16, 2 * LANES + lane] = 1.0
    e[64 + nd + lane % 16, 3 * LANES + lane] = sign
    return freq, jnp.asarray(e, dtype=BF16)


def kernel(x, c, positions, w_ada, b_ada, g_mix_norm, w_in, g_q_lat, w_q_b, g_kv_lat, w_kv_b,
           g_mla_q_nope, g_mla_q_pe, g_mla_k_nope, g_mla_k_pe, g_dil_q, g_dil_k, w_o,
           g_ffn_norm, w_up, w_conv, b_conv, w_down):
    B, S, D = x.shape
    assert D == D_MODEL and S % PROJ_TM == 0 and S % ATT_TQ == 0 and S % FFN_TM == 0
    assert S <= max(w for w, _ in DIL_PATTERNS)
    seg64, segpe, segpair = _segment_matrices()
    freq, expand = _rope_constants()
    mla_scale = (MLA_NOPE + MLA_ROPE) ** -0.5 * LOG2E
    dil_scale = DIL_HEAD_DIM ** -0.5 * LOG2E
    pos3 = positions.reshape(B, 1, S)

    for l in range(w_ada.shape[0]):
        mod3 = _ada(c, w_ada[l], b_ada[l][None, :]).reshape(B, 6, D)

        wi = w_in[l]
        c_kpe = MLA_Q_LORA + MLA_KV_LORA
        c_qd = c_kpe + MLA_ROPE
        kpe4 = jnp.broadcast_to(wi[:, c_kpe:c_qd].reshape(D, 2, 1, MLA_ROPE // 2),
                                (D, 2, 4, MLA_ROPE // 2)).reshape(D, LANES)
        dil = lambda w: _split_halves_layout(w, 4, 2, DIL_HEAD_DIM // 2)
        w_in_p = jnp.concatenate([
            wi[:, :c_kpe], kpe4, dil(wi[:, c_qd:c_qd + DIL_WIDTH]),
            dil(wi[:, c_qd + DIL_WIDTH:c_qd + 2 * DIL_WIDTH]), wi[:, c_qd + 2 * DIL_WIDTH:]],
            axis=1).astype(BF16)
        wq = w_q_b[l].reshape(MLA_Q_LORA, N_HEADS, MLA_NOPE + MLA_ROPE)
        pe4 = lambda w: _split_halves_layout(w, w.shape[-1] // LANES, 4, MLA_ROPE // 2)
        w_qb_p = jnp.concatenate([wq[:, :, :MLA_NOPE].reshape(MLA_Q_LORA, -1),
                                  pe4(wq[:, :, MLA_NOPE:].reshape(MLA_Q_LORA, -1))], axis=1).astype(BF16)
        wkv = w_kv_b[l].reshape(MLA_KV_LORA, N_HEADS, MLA_NOPE + MLA_V)
        w_kvb_p = jnp.concatenate([wkv[:, :, :MLA_NOPE].reshape(MLA_KV_LORA, -1),
                                   wkv[:, :, MLA_NOPE:].reshape(MLA_KV_LORA, -1)], axis=1).astype(BF16)
        gains = {
            "mix": g_mix_norm[l], "q_lat": g_q_lat[l], "kv_lat": g_kv_lat[l],
            "q_nope": jnp.tile(g_mla_q_nope[l], N_HEADS) * mla_scale,
            "q_pe": pe4(jnp.tile(g_mla_q_pe[l], N_HEADS)) * mla_scale,
            "k_nope": jnp.tile(g_mla_k_nope[l], N_HEADS),
            "k_pe": pe4(jnp.tile(g_mla_k_pe[l], 4)),
            "dil_q": dil(jnp.tile(g_dil_q[l], N_HEADS)) * dil_scale,
            "dil_k": dil(jnp.tile(g_dil_k[l], N_HEADS))}
        gains = jnp.stack([jnp.pad(gains[n], (0, D - w)) for n, w in GAIN_ROWS])
        consts = (gains, w_in_p, w_qb_p, w_kvb_p, seg64, segpe, segpair, freq, expand)
        qn, qpe, kn, kpe, vm, qd, kd, vd = _proj(x, pos3, mod3, consts)

        two = ATT_PAIRS * LANES
        o_mla = _attn_call(_mla_kernel, "mla", (qn, qpe, kn, kpe, vm), (two, LANES, two, None, two),
                           2 * LANES, _mla_bias(), B, S)
        o_dil = _attn_call(_dil_kernel, "dil", (qd, kd, vd), (two, two, two), LANES, _dil_bias(), B, S)

        x = _ffn(x, o_mla, o_dil, mod3, g_ffn_norm[l][None, :], w_o[l].astype(BF16),
                 w_up[l].astype(BF16), w_conv[l], b_conv[l][None, :], w_down[l].astype(BF16))
    return x
```

```python
import functools

import numpy as np
import jax
import jax.numpy as jnp
from jax import lax
from jax.experimental import pallas as pl
from jax.experimental.pallas import tpu as pltpu

F32 = jnp.float32
BF16 = jnp.bfloat16

D_MODEL = 1024
N_HEADS = 8
MLA_NOPE = 64
MLA_ROPE = 32
MLA_V = 64
MLA_Q_LORA = 512
MLA_KV_LORA = 256
DIL_HEAD_DIM = 64
DIL_WIDTH = N_HEADS * DIL_HEAD_DIM
DIL_PATTERNS = ((128, 1), (512, 4), (2048, 16))
D_FF = 2816
ROPE_THETA = 10000.0
EPS = 1e-6
MASK_BIAS = -1e30
LOG2E = 1.4426950408889634

LANES = 128
MXU_DIM = 256
VMEM_LIMIT = 56 * 1024 * 1024
ADA_TN = 1024
PROJ_TM = 1024
ATT_TQ = 256
ATT_TK = 256
VT_ROWS = 64 + 16
ATT_PAIRS = 2
FFN_TM = 512
FFN_TF = 256
N_FT = D_FF // FFN_TF
DOWN_TILES = 2
STAGE_AT = N_FT - 4

C_QLAT = 0
C_KVLAT = C_QLAT + MLA_Q_LORA
C_KPE = C_KVLAT + MLA_KV_LORA
C_QD = C_KPE + LANES
C_KD = C_QD + DIL_WIDTH
C_VD = C_KD + DIL_WIDTH
IN_COLS_P = C_VD + DIL_WIDTH


def _dot(a, b):
    return jnp.dot(a, b, preferred_element_type=F32)


def _const_spec(shape):
    nd = len(shape)
    return pl.BlockSpec(shape, lambda *_: (0,) * nd, pipeline_mode=pl.Buffered(1))


def _split_bf16(v):
    hi = v.astype(BF16)
    lo = (v - hi.astype(F32)).astype(BF16)
    return hi, lo


def _ada_kernel(c_ref, w_ref, b_ref, o_ref):
    c = c_ref[...]
    a = c * (1.0 / (1.0 + jnp.exp(-c)))
    n = a.shape[0]
    o2 = _dot(jnp.concatenate(_split_bf16(a), axis=0), w_ref[...].astype(BF16))
    o_ref[...] = o2[:n] + o2[n:] + b_ref[...]


def _ada(c, w_ada, b_ada):
    B = c.shape[0]
    n = w_ada.shape[1]
    return pl.pallas_call(
        _ada_kernel,
        out_shape=jax.ShapeDtypeStruct((B, n), F32),
        grid=(n // ADA_TN,),
        in_specs=[pl.BlockSpec((B, D_MODEL), lambda j: (0, 0)),
                  pl.BlockSpec((D_MODEL, ADA_TN), lambda j: (0, j)),
                  pl.BlockSpec((1, ADA_TN), lambda j: (0, j))],
        out_specs=pl.BlockSpec((B, ADA_TN), lambda j: (0, j)),
        compiler_params=pltpu.CompilerParams(dimension_semantics=("arbitrary",),
                                             vmem_limit_bytes=VMEM_LIMIT),
        name="ada",
    )(c, w_ada, b_ada)


def _rms_rows(v, width):
    ms = jnp.sum(v * v, axis=-1, keepdims=True) * (1.0 / width)
    return v * lax.rsqrt(ms + EPS)


def _seg_rms(v, seg_ref, seg_width):
    outs = []
    for c0 in range(0, v.shape[1], MXU_DIM):
        blk = v[:, c0:c0 + MXU_DIM]
        ss = _dot((blk * blk).astype(BF16), seg_ref[...])
        outs.append(blk * lax.rsqrt(ss * (1.0 / seg_width) + EPS))
    return outs[0] if len(outs) == 1 else jnp.concatenate(outs, axis=1)


def _rope(v, cos, sin_signed):
    outs = []
    for c0 in range(0, v.shape[1], LANES):
        blk = v[:, c0:c0 + LANES]
        outs.append(blk * cos + pltpu.roll(blk, LANES // 2, 1) * sin_signed)
    return outs[0] if len(outs) == 1 else jnp.concatenate(outs, axis=1)


GAIN_ROWS = (("mix", D_MODEL), ("q_lat", MLA_Q_LORA), ("kv_lat", MLA_KV_LORA),
             ("q_nope", N_HEADS * MLA_NOPE), ("q_pe", N_HEADS * MLA_ROPE), ("k_nope", N_HEADS * MLA_NOPE),
             ("k_pe", LANES), ("dil_q", DIL_WIDTH), ("dil_k", DIL_WIDTH))


class _Rows:
    def __init__(self, ref):
        self._ref = ref

    def __getattr__(self, name):
        idx, width = next((i, w) for i, (n, w) in enumerate(GAIN_ROWS) if n == name)
        return self._ref[idx:idx + 1, :width]


def _proj_kernel(x_ref, pos_ref, mod_ref, gains_ref, win_ref, wqb_ref, wkvb_ref,
                 seg64_ref, segpe_ref, segpair_ref, freq_ref, expand_ref,
                 qn_ref, qpe_ref, kn_ref, kpe_ref, vm_ref, qd_ref, kd_ref, vd_ref):
    g = _Rows(gains_ref)
    x = x_ref[0]
    sh1 = mod_ref[0, 0:1, :]
    sc1 = mod_ref[0, 1:2, :]
    h = (_rms_rows(x, D_MODEL) * (g.mix * (1.0 + sc1)) + sh1).astype(BF16)

    q_lat = _dot(h, win_ref[:, C_QLAT:C_QLAT + MLA_Q_LORA])
    kv_lat = _dot(h, win_ref[:, C_KVLAT:C_KVLAT + MLA_KV_LORA])
    qd_raw = _dot(h, win_ref[:, C_QD:C_QD + DIL_WIDTH])

    pos = pos_ref[0].astype(F32)
    ang = freq_ref[...] * pos
    cs = jnp.concatenate([jnp.cos(ang), jnp.sin(ang)], axis=0).T
    cs_hi, cs_lo = _split_bf16(cs)
    tabs = _dot(cs_hi, expand_ref[...]) + _dot(cs_lo, expand_ref[...])
    cos_d, sin_d = tabs[:, 0:LANES], tabs[:, LANES:2 * LANES]
    cos_p, sin_p = tabs[:, 2 * LANES:3 * LANES], tabs[:, 3 * LANES:4 * LANES]

    q_in = (_rms_rows(q_lat, MLA_Q_LORA) * g.q_lat).astype(BF16)
    q = _dot(q_in, wqb_ref[...])
    kv_in = (_rms_rows(kv_lat, MLA_KV_LORA) * g.kv_lat).astype(BF16)
    kv = _dot(kv_in, wkvb_ref[...])
    kd_raw = _dot(h, win_ref[:, C_KD:C_KD + DIL_WIDTH])
    qd_ref[0] = _rope(_seg_rms(qd_raw, segpair_ref, DIL_HEAD_DIM) * g.dil_q, cos_d, sin_d).astype(BF16)
    k_pe = _dot(h, win_ref[:, C_KPE:C_KPE + LANES])
    vd_ref[0] = _dot(h, win_ref[:, C_VD:C_VD + DIL_WIDTH]).astype(BF16)

    n_nope = N_HEADS * MLA_NOPE
    qn_ref[0] = (_seg_rms(q[:, :n_nope], seg64_ref, MLA_NOPE) * g.q_nope).astype(BF16)
    q_pe = _seg_rms(q[:, n_nope:], segpe_ref, MLA_ROPE) * g.q_pe
    qpe_ref[0] = _rope(q_pe, cos_p, sin_p).astype(BF16)
    kn_ref[0] = (_seg_rms(kv[:, :n_nope], seg64_ref, MLA_NOPE) * g.k_nope).astype(BF16)
    vm_ref[0] = kv[:, n_nope:].astype(BF16)
    kd_ref[0] = _rope(_seg_rms(kd_raw, segpair_ref, DIL_HEAD_DIM) * g.dil_k, cos_d, sin_d).astype(BF16)
    kpe_ref[0] = _rope(_rms_rows(k_pe, LANES) * g.k_pe, cos_p, sin_p).astype(BF16)


def _proj(x, pos3, mod3, consts):
    B, S, _ = x.shape
    tm = PROJ_TM
    row = lambda w: pl.BlockSpec((1, tm, w), lambda b, i: (b, i, 0))
    out_widths = (512, 256, 512, LANES, 512, 512, 512, 512)
    in_specs = [row(D_MODEL),
                pl.BlockSpec((1, 1, tm), lambda b, i: (b, 0, i)),
                pl.BlockSpec((1, 6, D_MODEL), lambda b, i: (b, 0, 0))]
    in_specs += [_const_spec(a.shape) for a in consts]
    return pl.pallas_call(
        _proj_kernel,
        out_shape=tuple(jax.ShapeDtypeStruct((B, S, w), BF16) for w in out_widths),
        grid=(B, S // tm),
        in_specs=in_specs,
        out_specs=tuple(row(w) for w in out_widths),
        compiler_params=pltpu.CompilerParams(dimension_semantics=("arbitrary", "arbitrary"),
                                             vmem_limit_bytes=VMEM_LIMIT),
        name="proj",
    )(x, pos3, mod3, *consts)


def _row_iota(shape):
    return lax.broadcasted_iota(jnp.int32, shape, 0)


def _sublane_allreduce(v, op):
    for shift in (4, 2, 1):
        v = op(v, pltpu.roll(v, shift, 0))
    return v


def _transpose_bf16(v):
    return v.astype(F32).T.astype(BF16)


class _Stream:
    def __init__(self, load_qt2, load_k, load_v, store_o):
        self.load_qt2, self.load_k, self.load_v, self.store_o = load_qt2, load_k, load_v, store_o


def _pair_attention(streams, n_rows, bias_ref, far_bias, vt_scr, qt_scr, s_scr, m_scr, l_scr, acc_scr):
    tq, tk = ATT_TQ, ATT_TK
    assert tq == tk
    n_q = n_rows // tq
    hd = LANES // 2
    ids = range(len(streams))

    def put_queries(i):
        for s in ids:
            qt_scr[s] = streams[s].load_qt2(pl.ds(i * tq, tq))

    def put_scores(j, slot):
        k_rows = pl.ds(j * tk, tk)
        for s in ids:
            s_scr[s, slot] = _dot(streams[s].load_k(k_rows), qt_scr[s])

    def consume(j, slot, bias_idx):
        for s in ids:
            st = s_scr[s, slot]
            if bias_idx is not None:
                st = st + bias_ref[bias_idx]
            s3 = st.reshape(tk // 8, 8, 2 * tq)
            m_new = _sublane_allreduce(jnp.max(s3, axis=0), jnp.maximum)
            if j > 0:
                m_old = m_scr[s]
                m_new = jnp.maximum(m_old, m_new)
                alpha = jnp.exp2(m_old - m_new)
            p3 = jnp.exp2(s3 - m_new[None])
            m_scr[s] = m_new
            pt = p3.reshape(tk, 2 * tq).astype(BF16)
            vt = vt_scr[s, j]
            l_new = []
            for h in range(2):
                pv = _dot(vt[h * VT_ROWS:(h + 1) * VT_ROWS, :], pt[:, h * tq:(h + 1) * tq])
                l_new.append(pv[hd:hd + 8, :])
                pv = pv[:hd, :]
                if j > 0:
                    a3 = alpha[:, h * tq:(h + 1) * tq][None]
                    pv = (acc_scr[s, h].reshape(hd // 8, 8, tq) * a3).reshape(hd, tq) + pv
                acc_scr[s, h] = pv
            l_new = jnp.concatenate(l_new, axis=1)
            l_scr[s] = l_new if j == 0 else alpha * l_scr[s] + l_new

    def step(j, slot, bias_idx):
        put_scores(j + 1, 1 - slot)
        consume(j, slot, bias_idx)

    ones = jnp.ones((VT_ROWS - hd, tk), BF16)
    for s in ids:
        for j in range(n_rows // tk):
            vt = _transpose_bf16(streams[s].load_v(pl.ds(j * tk, tk)))
            vt_scr[s, j] = jnp.concatenate([vt[:hd], ones, vt[hd:], ones], axis=0)
    put_queries(0)
    put_scores(0, 0)

    t = 0
    for i in range(n_q):
        for j in range(i):
            step(j, t % 2, None if far_bias is None else min(i - j, far_bias))
            t += 1
        put_queries(min(i + 1, n_q - 1))
        put_scores(0, 1 - t % 2)
        consume(i, t % 2, 0)
        t += 1

        for s in ids:
            inv_l = 1.0 / l_scr[s]
            ot = jnp.concatenate(
                [(acc_scr[s, h].reshape(hd // 8, 8, tq) * inv_l[:, h * tq:(h + 1) * tq][None]).reshape(hd, tq)
                 for h in range(2)], axis=0)
            streams[s].store_o(pl.ds(i * tq, tq), ot.T)


def _lane_block(ref, s):
    cols = slice(s * LANES, (s + 1) * LANES)
    return lambda rows: ref[0, rows, cols]


def _mla_kernel(qn_ref, qpe_ref, kn_ref, kpe_ref, v_ref, bias_ref, o_ref, *scratch):
    feat = _row_iota((2 * LANES, ATT_TQ))
    pe_head = (feat % (LANES // 2)) // (MLA_ROPE // 2)
    is_pe = feat >= LANES
    n_streams = o_ref.shape[2] // LANES
    assert n_streams == 2

    def make(s):
        mask_a = (feat < LANES // 2) | (is_pe & (pe_head == 2 * s))
        mask_b = ((feat >= LANES // 2) & (feat < LANES)) | (is_pe & (pe_head == 2 * s + 1))
        qn, kn = _lane_block(qn_ref, s), _lane_block(kn_ref, s)

        def load_qt2(rows):
            qt = jnp.concatenate([qn(rows), qpe_ref[0, rows, :]], axis=1).astype(F32).T
            z = jnp.zeros_like(qt)
            return jnp.concatenate([jnp.where(mask_a, qt, z), jnp.where(mask_b, qt, z)],
                                   axis=1).astype(BF16)

        def load_k(rows):
            return jnp.concatenate([kn(rows), kpe_ref[0, rows, :]], axis=1)

        def store_o(rows, val):
            o_ref[0, rows, s * LANES:(s + 1) * LANES] = val.astype(o_ref.dtype)

        return _Stream(load_qt2, load_k, _lane_block(v_ref, s), store_o)

    _pair_attention([make(s) for s in range(n_streams)], o_ref.shape[1], bias_ref, None, *scratch)


def _dil_kernel(q_ref, k_ref, v_ref, bias_ref, o_ref, *scratch):
    feat = _row_iota((LANES, ATT_TQ))
    is_a = (feat % (LANES // 2)) < DIL_HEAD_DIM // 2

    def make(s):
        q = _lane_block(q_ref, s)

        def load_qt2(rows):
            qt = q(rows).astype(F32).T
            z = jnp.zeros_like(qt)
            return jnp.concatenate([jnp.where(is_a, qt, z), jnp.where(is_a, z, qt)],
                                   axis=1).astype(BF16)

        def store_o(rows, val):
            o_ref[0, rows, s * LANES:(s + 1) * LANES] = val.astype(o_ref.dtype)

        return _Stream(load_qt2, _lane_block(k_ref, s), _lane_block(v_ref, s), store_o)

    _pair_attention([make(s) for s in range(o_ref.shape[2] // LANES)], o_ref.shape[1], bias_ref,
                    bias_ref.shape[0] - 1, *scratch)


def _attn_call(kernel, name, arrays, widths, qk_width, bias, B, S):
    n_s = ATT_PAIRS
    col = lambda w: pl.BlockSpec((1, S, w), lambda b, g: (b, 0, g))
    shared = pl.BlockSpec((1, S, LANES), lambda b, g: (b, 0, 0))
    in_specs = [shared if w is None else col(w) for w in widths] + [_const_spec(bias.shape)]
    return pl.pallas_call(
        kernel,
        out_shape=jax.ShapeDtypeStruct((B, S, N_HEADS * MLA_V), BF16),
        grid=(B, N_HEADS // 2 // n_s),
        in_specs=in_specs,
        out_specs=col(n_s * LANES),
        scratch_shapes=[pltpu.VMEM((n_s, S // ATT_TK, 2 * VT_ROWS, ATT_TK), BF16),
                        pltpu.VMEM((n_s, qk_width, 2 * ATT_TQ), BF16),
                        pltpu.VMEM((n_s, 2, ATT_TK, 2 * ATT_TQ), F32),
                        pltpu.VMEM((n_s, 8, 2 * ATT_TQ), F32),
                        pltpu.VMEM((n_s, 8, 2 * ATT_TQ), F32),
                        pltpu.VMEM((n_s, 2, LANES // 2, ATT_TQ), F32)],
        compiler_params=pltpu.CompilerParams(dimension_semantics=("arbitrary", "arbitrary"),
                                             vmem_limit_bytes=VMEM_LIMIT),
        name=name,
    )(*arrays, bias)


def _two_heads(tile_qk):
    return np.concatenate([tile_qk.T, tile_qk.T], axis=1)


def _mla_bias():
    d = np.arange(ATT_TQ)[:, None] - np.arange(ATT_TK)[None, :]
    tile = np.where(d >= 0, 0.0, MASK_BIAS).astype(np.float32)
    return jnp.asarray(_two_heads(tile)[None])


def _dil_bias():
    (far_window, far_dil), = [(w, d) for w, d in DIL_PATTERNS if w == max(p[0] for p in DIL_PATTERNS)]
    assert ATT_TK % far_dil == 0
    max_near = max(w for w, d in DIL_PATTERNS if w < far_window)
    n_tiles = -(-max_near // ATT_TK) + 2
    tiles = []
    for blk in range(n_tiles):
        delta = blk * ATT_TK + np.arange(ATT_TQ)[:, None] - np.arange(ATT_TK)[None, :]
        mult = np.zeros(delta.shape, np.int64)
        for window, dil in DIL_PATTERNS:
            mult += (delta >= 0) & (delta % dil == 0) & (delta <= window)
        tile = np.where(mult > 0, np.log2(np.maximum(mult, 1)), MASK_BIAS).astype(np.float32)
        tiles.append(_two_heads(tile))
    return jnp.asarray(np.stack(tiles))


def _ffn_kernel(x0_ref, om0_ref, od0_ref, mod0_ref, xn_ref, omn_ref, odn_ref, modn_ref, mod_ref,
                gffn_ref, wo_ref, wup_ref, wconv_ref, bconv_ref, wdown_ref, out_ref,
                carry_scr, acc_scr, h2_scr, y_scr, a_scr, x1n_scr, h2n_scr):
    half = wo_ref.shape[0] // 2

    def stage_tile(x_ref, om_ref, od_ref, m_ref):
        g1, sh2, sc2 = m_ref[0, 2:3, :], m_ref[0, 3:4, :], m_ref[0, 4:5, :]
        mix = _dot(om_ref[0], wo_ref[:half, :]) + _dot(od_ref[0], wo_ref[half:, :])
        x1 = x_ref[0] + g1 * mix
        x1n_scr[...] = x1
        h2n_scr[...] = (_rms_rows(x1, D_MODEL) * (gffn_ref[...] * (1.0 + sc2)) + sh2).astype(BF16)

    @pl.when((pl.program_id(0) == 0) & (pl.program_id(1) == 0))
    def _():
        stage_tile(x0_ref, om0_ref, od0_ref, mod0_ref)

    g2 = mod_ref[0, 5:6, :]
    h2_scr[...] = h2n_scr[...]
    out_ref[0] = x1n_scr[...]

    @pl.when(pl.program_id(1) == 0)
    def _():
        carry_scr[...] = jnp.zeros(carry_scr.shape, F32)

    def up(f):
        for half_idx, t in enumerate((f, f + N_FT)):
            y = _dot(h2_scr[...], wup_ref[:, t * FFN_TF:(t + 1) * FFN_TF])
            y_scr[f % 2, half_idx, 0:8, :] = carry_scr[t]
            y_scr[f % 2, half_idx, 8:, :] = y
            carry_scr[t] = y[FFN_TM - 8:, :]

    def conv(f, half_idx):
        t = f + half_idx * N_FT
        yb = y_scr.at[f % 2, half_idx]
        cols = slice(t * FFN_TF, (t + 1) * FFN_TF)
        w = wconv_ref[:, cols]
        return (w[2:3, :] * yb[8:8 + FFN_TM, :] + w[1:2, :] * yb[7:7 + FFN_TM, :]
                + w[0:1, :] * yb[6:6 + FFN_TM, :] + bconv_ref[:, cols])

    def down(f0, f1):
        d = _dot(a_scr[(f0 // DOWN_TILES) % 2, :, :(f1 - f0) * FFN_TF],
                 wdown_ref[f0 * FFN_TF:f1 * FFN_TF, :])
        if f0 == 0:
            acc_scr[...] = d
        else:
            acc_scr[...] += d

    up(0)
    pending = None
    for f in range(N_FT):
        if f + 1 < N_FT:
            up(f + 1)
        if pending is not None:
            down(*pending)
            pending = None
        if f == STAGE_AT:
            stage_tile(xn_ref, omn_ref, odn_ref, modn_ref)
        gate = conv(f, 0)
        val = conv(f, 1)
        k = f % DOWN_TILES
        a_scr[(f // DOWN_TILES) % 2, :, k * FFN_TF:(k + 1) * FFN_TF] = (
            gate * (1.0 / (1.0 + jnp.exp(-gate))) * val).astype(BF16)
        if k == DOWN_TILES - 1 or f == N_FT - 1:
            pending = (f - k, f + 1)
    down(*pending)
    out_ref[0] = out_ref[0] + g2 * acc_scr[...]


def _ffn(x, o_mla, o_dil, mod3, g_ffn, w_o, w_up, w_conv, b_conv, w_down):
    B, S, _ = x.shape
    tm = FFN_TM
    nt = S // tm

    def nxt(b, i):
        t = jnp.minimum(b * nt + i + 1, B * nt - 1)
        return t // nt, t % nt

    first = lambda w: pl.BlockSpec((1, tm, w), lambda b, i: (0, 0, 0), pipeline_mode=pl.Buffered(1))
    following = lambda w: pl.BlockSpec((1, tm, w), lambda b, i: (*nxt(b, i), 0))
    mod_spec = lambda fn: pl.BlockSpec((1, 6, D_MODEL), fn)
    wm, wd = o_mla.shape[2], o_dil.shape[2]
    consts = (g_ffn, w_o, w_up, w_conv, b_conv, w_down)
    return pl.pallas_call(
        _ffn_kernel,
        out_shape=jax.ShapeDtypeStruct(x.shape, F32),
        grid=(B, nt),
        in_specs=[first(D_MODEL), first(wm), first(wd), mod_spec(lambda b, i: (0, 0, 0)),
                  following(D_MODEL), following(wm), following(wd),
                  mod_spec(lambda b, i: (nxt(b, i)[0], 0, 0)), mod_spec(lambda b, i: (b, 0, 0))]
                 + [_const_spec(a.shape) for a in consts],
        out_specs=pl.BlockSpec((1, tm, D_MODEL), lambda b, i: (b, i, 0)),
        scratch_shapes=[pltpu.VMEM((2 * N_FT, 8, FFN_TF), F32),
                        pltpu.VMEM((tm, D_MODEL), F32),
                        pltpu.VMEM((tm, D_MODEL), BF16),
                        pltpu.VMEM((2, 2, tm + 8, FFN_TF), F32),
                        pltpu.VMEM((2, tm, DOWN_TILES * FFN_TF), BF16),
                        pltpu.VMEM((tm, D_MODEL), F32),
                        pltpu.VMEM((tm, D_MODEL), BF16)],
        compiler_params=pltpu.CompilerParams(dimension_semantics=("arbitrary", "arbitrary"),
                                             vmem_limit_bytes=VMEM_LIMIT),
        name="ffn",
    )(x, o_mla, o_dil, mod3, x, o_mla, o_dil, mod3, mod3, *consts)


def _split_halves_layout(w, n_groups, heads, half):
    lead = w.shape[:-1]
    w = w.reshape(*lead, n_groups, heads, 2, half)
    return jnp.swapaxes(w, -3, -2).reshape(*lead, n_groups * heads * 2 * half)


def _segment_matrices():
    i = np.arange(MXU_DIM)
    lane = i % LANES
    seg64 = i // 64
    segpe = (i // LANES) * 4 + (lane % 64) // (MLA_ROPE // 2)
    segpair = (i // LANES) * 2 + (lane % 64) // (DIL_HEAD_DIM // 2)
    same = lambda s: jnp.asarray((s[:, None] == s[None, :]).astype(np.float32), dtype=BF16)
    return same(seg64), same(segpe), same(segpair)


def _rope_constants():
    def inv_freq(d):
        half = d // 2
        return jnp.power(ROPE_THETA, -2.0 * jnp.arange(half, dtype=F32) / d)
    fd, fp = inv_freq(DIL_HEAD_DIM), inv_freq(MLA_ROPE)
    nd, npe = fd.shape[0], fp.shape[0]
    freq = jnp.concatenate([fd, fp, jnp.zeros((64 - nd - npe,), F32)])[:, None]
    lane = np.arange(LANES)
    sign = np.where(lane < 64, -1.0, 1.0)
    e = np.zeros((2 * 64, 4 * LANES), np.float32)
    e[lane % 32, lane] = 1.0
    e[64 + lane % 32, LANES + lane] = sign
    e[nd + lane % 16, 2 * LANES + lane] = 1.0
    e[64 + nd + lane % 16, 3 * LANES + lane] = sign
    return freq, jnp.asarray(e, dtype=BF16)


def kernel(x, c, positions, w_ada, b_ada, g_mix_norm, w_in, g_q_lat, w_q_b, g_kv_lat, w_kv_b,
           g_mla_q_nope, g_mla_q_pe, g_mla_k_nope, g_mla_k_pe, g_dil_q, g_dil_k, w_o,
           g_ffn_norm, w_up, w_conv, b_conv, w_down):
    B, S, D = x.shape
    assert D == D_MODEL and S % PROJ_TM == 0 and S % ATT_TQ == 0 and S % FFN_TM == 0
    assert S <= max(w for w, _ in DIL_PATTERNS)
    seg64, segpe, segpair = _segment_matrices()
    freq, expand = _rope_constants()
    mla_scale = (MLA_NOPE + MLA_ROPE) ** -0.5 * LOG2E
    dil_scale = DIL_HEAD_DIM ** -0.5 * LOG2E
    pos3 = positions.reshape(B, 1, S)

    for l in range(w_ada.shape[0]):
        mod3 = _ada(c, w_ada[l], b_ada[l][None, :]).reshape(B, 6, D)

        wi = w_in[l]
        c_kpe = MLA_Q_LORA + MLA_KV_LORA
        c_qd = c_kpe + MLA_ROPE
        kpe4 = jnp.broadcast_to(wi[:, c_kpe:c_qd].reshape(D, 2, 1, MLA_ROPE // 2),
                                (D, 2, 4, MLA_ROPE // 2)).reshape(D, LANES)
        dil = lambda w: _split_halves_layout(w, 4, 2, DIL_HEAD_DIM // 2)
        w_in_p = jnp.concatenate([
            wi[:, :c_kpe], kpe4, dil(wi[:, c_qd:c_qd + DIL_WIDTH]),
            dil(wi[:, c_qd + DIL_WIDTH:c_qd + 2 * DIL_WIDTH]), wi[:, c_qd + 2 * DIL_WIDTH:]],
            axis=1).astype(BF16)
        wq = w_q_b[l].reshape(MLA_Q_LORA, N_HEADS, MLA_NOPE + MLA_ROPE)
        pe4 = lambda w: _split_halves_layout(w, w.shape[-1] // LANES, 4, MLA_ROPE // 2)
        w_qb_p = jnp.concatenate([wq[:, :, :MLA_NOPE].reshape(MLA_Q_LORA, -1),
                                  pe4(wq[:, :, MLA_NOPE:].reshape(MLA_Q_LORA, -1))], axis=1).astype(BF16)
        wkv = w_kv_b[l].reshape(MLA_KV_LORA, N_HEADS, MLA_NOPE + MLA_V)
        w_kvb_p = jnp.concatenate([wkv[:, :, :MLA_NOPE].reshape(MLA_KV_LORA, -1),
                                   wkv[:, :, MLA_NOPE:].reshape(MLA_KV_LORA, -1)], axis=1).astype(BF16)
        gains = {
            "mix": g_mix_norm[l], "q_lat": g_q_lat[l], "kv_lat": g_kv_lat[l],
            "q_nope": jnp.tile(g_mla_q_nope[l], N_HEADS) * mla_scale,
            "q_pe": pe4(jnp.tile(g_mla_q_pe[l], N_HEADS)) * mla_scale,
            "k_nope": jnp.tile(g_mla_k_nope[l], N_HEADS),
            "k_pe": pe4(jnp.tile(g_mla_k_pe[l], 4)),
            "dil_q": dil(jnp.tile(g_dil_q[l], N_HEADS)) * dil_scale,
            "dil_k": dil(jnp.tile(g_dil_k[l], N_HEADS))}
        gains = jnp.stack([jnp.pad(gains[n], (0, D - w)) for n, w in GAIN_ROWS])
        consts = (gains, w_in_p, w_qb_p, w_kvb_p, seg64, segpe, segpair, freq, expand)
        qn, qpe, kn, kpe, vm, qd, kd, vd = _proj(x, pos3, mod3, consts)

        two = ATT_PAIRS * LANES
        o_mla = _attn_call(_mla_kernel, "mla", (qn, qpe, kn, kpe, vm), (two, LANES, two, None, two),
                           2 * LANES, _mla_bias(), B, S)
        o_dil = _attn_call(_dil_kernel, "dil", (qd, kd, vd), (two, two, two), LANES, _dil_bias(), B, S)

        x = _ffn(x, o_mla, o_dil, mod3, g_ffn_norm[l][None, :], w_o[l].astype(BF16),
                 w_up[l].astype(BF16), w_conv[l], b_conv[l][None, :], w_down[l].astype(BF16))
    return x
```

```python
import numpy as np
import jax
import jax.numpy as jnp
from jax import lax
from jax.experimental import pallas as pl
from jax.experimental.pallas import tpu as pltpu

F32 = jnp.float32
BF16 = jnp.bfloat16

D_MODEL = 1024
N_HEADS = 8
MLA_NOPE = 64
MLA_ROPE = 32
MLA_V = 64
MLA_Q_LORA = 512
MLA_KV_LORA = 256
DIL_HEAD_DIM = 64
DIL_WIDTH = N_HEADS * DIL_HEAD_DIM
DIL_PATTERNS = ((128, 1), (512, 4), (2048, 16))
D_FF = 2816
ROPE_THETA = 10000.0
EPS = 1e-6
MASK_BIAS = -1e30
LOG2E = 1.4426950408889634

LANES = 128
MXU_DIM = 256
VMEM_LIMIT = 56 * 1024 * 1024
ADA_TN = 1024
PROJ_TM = 1024
ATT_TQ = 256
ATT_TK = 256
VT_ROWS = 64 + 16
ATT_PAIRS = 2
FFN_TM = 512
FFN_TF = 256
N_FT = D_FF // FFN_TF
DOWN_TILES = 2
STAGE_AT = N_FT - 4

C_QLAT = 0
C_KVLAT = C_QLAT + MLA_Q_LORA
C_KPE = C_KVLAT + MLA_KV_LORA
C_QD = C_KPE + LANES
C_KD = C_QD + DIL_WIDTH
C_VD = C_KD + DIL_WIDTH


def _dot(a, b):
    return jnp.dot(a, b, preferred_element_type=F32)


def _const_spec(shape):
    nd = len(shape)
    return pl.BlockSpec(shape, lambda *_: (0,) * nd, pipeline_mode=pl.Buffered(1))


def _split_bf16(v):
    hi = v.astype(BF16)
    lo = (v - hi.astype(F32)).astype(BF16)
    return hi, lo


def _ada_kernel(c_ref, w_ref, b_ref, o_ref):
    c = c_ref[...]
    a = c * (1.0 / (1.0 + jnp.exp(-c)))
    n = a.shape[0]
    o2 = _dot(jnp.concatenate(_split_bf16(a), axis=0), w_ref[...].astype(BF16))
    o_ref[...] = o2[:n] + o2[n:] + b_ref[...]


def _ada(c, w_ada, b_ada):
    B = c.shape[0]
    n = w_ada.shape[1]
    return pl.pallas_call(
        _ada_kernel,
        out_shape=jax.ShapeDtypeStruct((B, n), F32),
        grid=(n // ADA_TN,),
        in_specs=[pl.BlockSpec((B, D_MODEL), lambda j: (0, 0)),
                  pl.BlockSpec((D_MODEL, ADA_TN), lambda j: (0, j)),
                  pl.BlockSpec((1, ADA_TN), lambda j: (0, j))],
        out_specs=pl.BlockSpec((B, ADA_TN), lambda j: (0, j)),
        compiler_params=pltpu.CompilerParams(dimension_semantics=("arbitrary",),
                                             vmem_limit_bytes=VMEM_LIMIT),
        name="ada",
    )(c, w_ada, b_ada)


def _rms_rows(v, width):
    ms = jnp.sum(v * v, axis=-1, keepdims=True) * (1.0 / width)
    return v * lax.rsqrt(ms + EPS)


def _seg_rms(v, seg_ref, seg_width):
    outs = []
    for c0 in range(0, v.shape[1], MXU_DIM):
        blk = v[:, c0:c0 + MXU_DIM]
        ss = _dot((blk * blk).astype(BF16), seg_ref[...])
        outs.append(blk * lax.rsqrt(ss * (1.0 / seg_width) + EPS))
    return outs[0] if len(outs) == 1 else jnp.concatenate(outs, axis=1)


def _rope(v, cos, sin_signed):
    outs = []
    for c0 in range(0, v.shape[1], LANES):
        blk = v[:, c0:c0 + LANES]
        outs.append(blk * cos + pltpu.roll(blk, LANES // 2, 1) * sin_signed)
    return outs[0] if len(outs) == 1 else jnp.concatenate(outs, axis=1)


GAIN_ROWS = (("mix", D_MODEL), ("q_lat", MLA_Q_LORA), ("kv_lat", MLA_KV_LORA),
             ("q_nope", N_HEADS * MLA_NOPE), ("q_pe", N_HEADS * MLA_ROPE), ("k_nope", N_HEADS * MLA_NOPE),
             ("k_pe", LANES), ("dil_q", DIL_WIDTH), ("dil_k", DIL_WIDTH))


class _Rows:
    def __init__(self, ref):
        self._ref = ref

    def __getattr__(self, name):
        idx, width = next((i, w) for i, (n, w) in enumerate(GAIN_ROWS) if n == name)
        return self._ref[idx:idx + 1, :width]


def _proj_kernel(x_ref, pos_ref, mod_ref, gains_ref, win_ref, wqb_ref, wkvb_ref,
                 seg64_ref, segpe_ref, segpair_ref, freq_ref, expand_ref,
                 qn_ref, qpe_ref, kn_ref, kpe_ref, vm_ref, qd_ref, kd_ref, vd_ref):
    g = _Rows(gains_ref)
    x = x_ref[0]
    sh1 = mod_ref[0, 0:1, :]
    sc1 = mod_ref[0, 1:2, :]
    h = (_rms_rows(x, D_MODEL) * (g.mix * (1.0 + sc1)) + sh1).astype(BF16)

    q_lat = _dot(h, win_ref[:, C_QLAT:C_QLAT + MLA_Q_LORA])
    kv_lat = _dot(h, win_ref[:, C_KVLAT:C_KVLAT + MLA_KV_LORA])
    qd_raw = _dot(h, win_ref[:, C_QD:C_QD + DIL_WIDTH])

    pos = pos_ref[0].astype(F32)
    ang = freq_ref[...] * pos
    cs = jnp.concatenate([jnp.cos(ang), jnp.sin(ang)], axis=0).T
    cs_hi, cs_lo = _split_bf16(cs)
    tabs = _dot(cs_hi, expand_ref[...]) + _dot(cs_lo, expand_ref[...])
    cos_d, sin_d = tabs[:, 0:LANES], tabs[:, LANES:2 * LANES]
    cos_p, sin_p = tabs[:, 2 * LANES:3 * LANES], tabs[:, 3 * LANES:4 * LANES]

    q_in = (_rms_rows(q_lat, MLA_Q_LORA) * g.q_lat).astype(BF16)
    q = _dot(q_in, wqb_ref[...])
    kv_in = (_rms_rows(kv_lat, MLA_KV_LORA) * g.kv_lat).astype(BF16)
    kv = _dot(kv_in, wkvb_ref[...])
    kd_raw = _dot(h, win_ref[:, C_KD:C_KD + DIL_WIDTH])
    qd_ref[0] = _rope(_seg_rms(qd_raw, segpair_ref, DIL_HEAD_DIM) * g.dil_q, cos_d, sin_d).astype(BF16)
    k_pe = _dot(h, win_ref[:, C_KPE:C_KPE + LANES])
    vd_ref[0] = _dot(h, win_ref[:, C_VD:C_VD + DIL_WIDTH]).astype(BF16)

    n_nope = N_HEADS * MLA_NOPE
    qn_ref[0] = (_seg_rms(q[:, :n_nope], seg64_ref, MLA_NOPE) * g.q_nope).astype(BF16)
    q_pe = _seg_rms(q[:, n_nope:], segpe_ref, MLA_ROPE) * g.q_pe
    qpe_ref[0] = _rope(q_pe, cos_p, sin_p).astype(BF16)
    kn_ref[0] = (_seg_rms(kv[:, :n_nope], seg64_ref, MLA_NOPE) * g.k_nope).astype(BF16)
    vm_ref[0] = kv[:, n_nope:].astype(BF16)
    kd_ref[0] = _rope(_seg_rms(kd_raw, segpair_ref, DIL_HEAD_DIM) * g.dil_k, cos_d, sin_d).astype(BF16)
    kpe_ref[0] = _rope(_rms_rows(k_pe, LANES) * g.k_pe, cos_p, sin_p).astype(BF16)


def _proj(x, pos3, mod3, consts):
    B, S, _ = x.shape
    tm = PROJ_TM
    row = lambda w: pl.BlockSpec((1, tm, w), lambda b, i: (b, i, 0))
    out_widths = (512, 256, 512, LANES, 512, 512, 512, 512)
    in_specs = [row(D_MODEL),
                pl.BlockSpec((1, 1, tm), lambda b, i: (b, 0, i)),
                pl.BlockSpec((1, 6, D_MODEL), lambda b, i: (b, 0, 0))]
    in_specs += [_const_spec(a.shape) for a in consts]
    return pl.pallas_call(
        _proj_kernel,
        out_shape=tuple(jax.ShapeDtypeStruct((B, S, w), BF16) for w in out_widths),
        grid=(B, S // tm),
        in_specs=in_specs,
        out_specs=tuple(row(w) for w in out_widths),
        compiler_params=pltpu.CompilerParams(dimension_semantics=("arbitrary", "arbitrary"),
                                             vmem_limit_bytes=VMEM_LIMIT),
        name="proj",
    )(x, pos3, mod3, *consts)


def _row_iota(shape):
    return lax.broadcasted_iota(jnp.int32, shape, 0)


def _sublane_allreduce(v, op):
    for shift in (4, 2, 1):
        v = op(v, pltpu.roll(v, shift, 0))
    return v


def _transpose_bf16(v):
    return v.astype(F32).T.astype(BF16)


class _Stream:
    def __init__(self, load_qt2, load_k, load_v, store_o):
        self.load_qt2, self.load_k, self.load_v, self.store_o = load_qt2, load_k, load_v, store_o


def _pair_attention(streams, n_rows, bias_ref, far_bias, vt_scr, qt_scr, s_scr, m_scr, l_scr, acc_scr):
    tq, tk = ATT_TQ, ATT_TK
    assert tq == tk
    n_q = n_rows // tq
    hd = LANES // 2
    ids = range(len(streams))

    def put_queries(i):
        for s in ids:
            qt_scr[s] = streams[s].load_qt2(pl.ds(i * tq, tq))

    def put_scores(j, slot):
        k_rows = pl.ds(j * tk, tk)
        for s in ids:
            s_scr[s, slot] = _dot(streams[s].load_k(k_rows), qt_scr[s])

    def consume(j, slot, bias_idx):
        for s in ids:
            st = s_scr[s, slot]
            if bias_idx is not None:
                st = st + bias_ref[bias_idx]
            s3 = st.reshape(tk // 8, 8, 2 * tq)
            m_new = _sublane_allreduce(jnp.max(s3, axis=0), jnp.maximum)
            if j > 0:
                m_old = m_scr[s]
                m_new = jnp.maximum(m_old, m_new)
                alpha = jnp.exp2(m_old - m_new)
            p3 = jnp.exp2(s3 - m_new[None])
            m_scr[s] = m_new
            pt = p3.reshape(tk, 2 * tq).astype(BF16)
            vt = vt_scr[s, j]
            l_new = []
            for h in range(2):
                pv = _dot(vt[h * VT_ROWS:(h + 1) * VT_ROWS, :], pt[:, h * tq:(h + 1) * tq])
                l_new.append(pv[hd:hd + 8, :])
                pv = pv[:hd, :]
                if j > 0:
                    a3 = alpha[:, h * tq:(h + 1) * tq][None]
                    pv = (acc_scr[s, h].reshape(hd // 8, 8, tq) * a3).reshape(hd, tq) + pv
                acc_scr[s, h] = pv
            l_new = jnp.concatenate(l_new, axis=1)
            l_scr[s] = l_new if j == 0 else alpha * l_scr[s] + l_new

    def step(j, slot, bias_idx):
        put_scores(j + 1, 1 - slot)
        consume(j, slot, bias_idx)

    ones = jnp.ones((VT_ROWS - hd, tk), BF16)
    for s in ids:
        for j in range(n_rows // tk):
            vt = _transpose_bf16(streams[s].load_v(pl.ds(j * tk, tk)))
            vt_scr[s, j] = jnp.concatenate([vt[:hd], ones, vt[hd:], ones], axis=0)
    put_queries(0)
    put_scores(0, 0)

    t = 0
    for i in range(n_q):
        for j in range(i):
            step(j, t % 2, None if far_bias is None else min(i - j, far_bias))
            t += 1
        put_queries(min(i + 1, n_q - 1))
        put_scores(0, 1 - t % 2)
        consume(i, t % 2, 0)
        t += 1

        for s in ids:
            inv_l = 1.0 / l_scr[s]
            ot = jnp.concatenate(
                [(acc_scr[s, h].reshape(hd // 8, 8, tq) * inv_l[:, h * tq:(h + 1) * tq][None]).reshape(hd, tq)
                 for h in range(2)], axis=0)
            streams[s].store_o(pl.ds(i * tq, tq), ot.T)


def _lane_block(ref, s):
    cols = slice(s * LANES, (s + 1) * LANES)
    return lambda rows: ref[0, rows, cols]


def _mla_kernel(qn_ref, qpe_ref, kn_ref, kpe_ref, v_ref, bias_ref, o_ref, *scratch):
    feat = _row_iota((2 * LANES, ATT_TQ))
    pe_head = (feat % (LANES // 2)) // (MLA_ROPE // 2)
    is_pe = feat >= LANES
    n_streams = o_ref.shape[2] // LANES
    assert n_streams == 2

    def make(s):
        mask_a = (feat < LANES // 2) | (is_pe & (pe_head == 2 * s))
        mask_b = ((feat >= LANES // 2) & (feat < LANES)) | (is_pe & (pe_head == 2 * s + 1))
        qn, kn = _lane_block(qn_ref, s), _lane_block(kn_ref, s)

        def load_qt2(rows):
            qt = jnp.concatenate([qn(rows), qpe_ref[0, rows, :]], axis=1).astype(F32).T
            z = jnp.zeros_like(qt)
            return jnp.concatenate([jnp.where(mask_a, qt, z), jnp.where(mask_b, qt, z)],
                                   axis=1).astype(BF16)

        def load_k(rows):
            return jnp.concatenate([kn(rows), kpe_ref[0, rows, :]], axis=1)

        def store_o(rows, val):
            o_ref[0, rows, s * LANES:(s + 1) * LANES] = val.astype(o_ref.dtype)

        return _Stream(load_qt2, load_k, _lane_block(v_ref, s), store_o)

    _pair_attention([make(s) for s in range(n_streams)], o_ref.shape[1], bias_ref, None, *scratch)


def _dil_kernel(q_ref, k_ref, v_ref, bias_ref, o_ref, *scratch):
    feat = _row_iota((LANES, ATT_TQ))
    is_a = (feat % (LANES // 2)) < DIL_HEAD_DIM // 2

    def make(s):
        q = _lane_block(q_ref, s)

        def load_qt2(rows):
            qt = q(rows).astype(F32).T
            z = jnp.zeros_like(qt)
            return jnp.concatenate([jnp.where(is_a, qt, z), jnp.where(is_a, z, qt)],
                                   axis=1).astype(BF16)

        def store_o(rows, val):
            o_ref[0, rows, s * LANES:(s + 1) * LANES] = val.astype(o_ref.dtype)

        return _Stream(load_qt2, _lane_block(k_ref, s), _lane_block(v_ref, s), store_o)

    _pair_attention([make(s) for s in range(o_ref.shape[2] // LANES)], o_ref.shape[1], bias_ref,
                    bias_ref.shape[0] - 1, *scratch)


def _attn_call(kernel, name, arrays, widths, qk_width, bias, B, S):
    n_s = ATT_PAIRS
    col = lambda w: pl.BlockSpec((1, S, w), lambda b, g: (b, 0, g))
    shared = pl.BlockSpec((1, S, LANES), lambda b, g: (b, 0, 0))
    in_specs = [shared if w is None else col(w) for w in widths] + [_const_spec(bias.shape)]
    return pl.pallas_call(
        kernel,
        out_shape=jax.ShapeDtypeStruct((B, S, N_HEADS * MLA_V), BF16),
        grid=(B, N_HEADS // 2 // n_s),
        in_specs=in_specs,
        out_specs=col(n_s * LANES),
        scratch_shapes=[pltpu.VMEM((n_s, S // ATT_TK, 2 * VT_ROWS, ATT_TK), BF16),
                        pltpu.VMEM((n_s, qk_width, 2 * ATT_TQ), BF16),
                        pltpu.VMEM((n_s, 2, ATT_TK, 2 * ATT_TQ), F32),
                        pltpu.VMEM((n_s, 8, 2 * ATT_TQ), F32),
                        pltpu.VMEM((n_s, 8, 2 * ATT_TQ), F32),
                        pltpu.VMEM((n_s, 2, LANES // 2, ATT_TQ), F32)],
        compiler_params=pltpu.CompilerParams(dimension_semantics=("arbitrary", "arbitrary"),
                                             vmem_limit_bytes=VMEM_LIMIT),
        name=name,
    )(*arrays, bias)


def _two_heads(tile_qk):
    return np.concatenate([tile_qk.T, tile_qk.T], axis=1)


def _mla_bias():
    d = np.arange(ATT_TQ)[:, None] - np.arange(ATT_TK)[None, :]
    tile = np.where(d >= 0, 0.0, MASK_BIAS).astype(np.float32)
    return jnp.asarray(_two_heads(tile)[None])


def _dil_bias():
    (far_window, far_dil), = [(w, d) for w, d in DIL_PATTERNS if w == max(p[0] for p in DIL_PATTERNS)]
    assert ATT_TK % far_dil == 0
    max_near = max(w for w, d in DIL_PATTERNS if w < far_window)
    n_tiles = -(-max_near // ATT_TK) + 2
    tiles = []
    for blk in range(n_tiles):
        delta = blk * ATT_TK + np.arange(ATT_TQ)[:, None] - np.arange(ATT_TK)[None, :]
        mult = np.zeros(delta.shape, np.int64)
        for window, dil in DIL_PATTERNS:
            mult += (delta >= 0) & (delta % dil == 0) & (delta <= window)
        tile = np.where(mult > 0, np.log2(np.maximum(mult, 1)), MASK_BIAS).astype(np.float32)
        tiles.append(_two_heads(tile))
    return jnp.asarray(np.stack(tiles))


def _ffn_kernel(x0_ref, om0_ref, od0_ref, mod0_ref, xn_ref, omn_ref, odn_ref, modn_ref, mod_ref,
                gffn_ref, wo_ref, wup_ref, wconv_ref, bconv_ref, wdown_ref, out_ref,
                carry_scr, acc_scr, h2_scr, y_scr, a_scr, x1n_scr, h2n_scr):
    half = wo_ref.shape[0] // 2

    def stage_tile(x_ref, om_ref, od_ref, m_ref):
        g1, sh2, sc2 = m_ref[0, 2:3, :], m_ref[0, 3:4, :], m_ref[0, 4:5, :]
        mix = _dot(om_ref[0], wo_ref[:half, :]) + _dot(od_ref[0], wo_ref[half:, :])
        x1 = x_ref[0] + g1 * mix
        x1n_scr[...] = x1
        h2n_scr[...] = (_rms_rows(x1, D_MODEL) * (gffn_ref[...] * (1.0 + sc2)) + sh2).astype(BF16)

    @pl.when((pl.program_id(0) == 0) & (pl.program_id(1) == 0))
    def _():
        stage_tile(x0_ref, om0_ref, od0_ref, mod0_ref)

    g2 = mod_ref[0, 5:6, :]
    h2_scr[...] = h2n_scr[...]
    out_ref[0] = x1n_scr[...]

    @pl.when(pl.program_id(1) == 0)
    def _():
        carry_scr[...] = jnp.zeros(carry_scr.shape, F32)

    def up(f):
        for half_idx, t in enumerate((f, f + N_FT)):
            y = _dot(h2_scr[...], wup_ref[:, t * FFN_TF:(t + 1) * FFN_TF])
            y_scr[f % 2, half_idx, 0:8, :] = carry_scr[t]
            y_scr[f % 2, half_idx, 8:, :] = y
            carry_scr[t] = y[FFN_TM - 8:, :]

    def conv(f, half_idx):
        t = f + half_idx * N_FT
        yb = y_scr.at[f % 2, half_idx]
        cols = slice(t * FFN_TF, (t + 1) * FFN_TF)
        w = wconv_ref[:, cols]
        return (w[2:3, :] * yb[8:8 + FFN_TM, :] + w[1:2, :] * yb[7:7 + FFN_TM, :]
                + w[0:1, :] * yb[6:6 + FFN_TM, :] + bconv_ref[:, cols])

    def down(f0, f1):
        d = _dot(a_scr[(f0 // DOWN_TILES) % 2, :, :(f1 - f0) * FFN_TF],
                 wdown_ref[f0 * FFN_TF:f1 * FFN_TF, :])
        if f0 == 0:
            acc_scr[...] = d
        else:
            acc_scr[...] += d

    up(0)
    pending = None
    for f in range(N_FT):
        if f + 1 < N_FT:
            up(f + 1)
        if pending is not None:
            down(*pending)
            pending = None
        if f == STAGE_AT:
            stage_tile(xn_ref, omn_ref, odn_ref, modn_ref)
        gate = conv(f, 0)
        val = conv(f, 1)
        k = f % DOWN_TILES
        a_scr[(f // DOWN_TILES) % 2, :, k * FFN_TF:(k + 1) * FFN_TF] = (
            gate * (1.0 / (1.0 + jnp.exp(-gate))) * val).astype(BF16)
        if k == DOWN_TILES - 1 or f == N_FT - 1:
            pending = (f - k, f + 1)
    down(*pending)
    out_ref[0] = out_ref[0] + g2 * acc_scr[...]


def _ffn(x, o_mla, o_dil, mod3, g_ffn, w_o, w_up, w_conv, b_conv, w_down):
    B, S, _ = x.shape
    tm = FFN_TM
    nt = S // tm

    def nxt(b, i):
        t = jnp.minimum(b * nt + i + 1, B * nt - 1)
        return t // nt, t % nt

    first = lambda w: pl.BlockSpec((1, tm, w), lambda b, i: (0, 0, 0), pipeline_mode=pl.Buffered(1))
    following = lambda w: pl.BlockSpec((1, tm, w), lambda b, i: (*nxt(b, i), 0))
    mod_spec = lambda fn: pl.BlockSpec((1, 6, D_MODEL), fn)
    wm, wd = o_mla.shape[2], o_dil.shape[2]
    consts = (g_ffn, w_o, w_up, w_conv, b_conv, w_down)
    return pl.pallas_call(
        _ffn_kernel,
        out_shape=jax.ShapeDtypeStruct(x.shape, F32),
        grid=(B, nt),
        in_specs=[first(D_MODEL), first(wm), first(wd), mod_spec(lambda b, i: (0, 0, 0)),
                  following(D_MODEL), following(wm), following(wd),
                  mod_spec(lambda b, i: (nxt(b, i)[0], 0, 0)), mod_spec(lambda b, i: (b, 0, 0))]
                 + [_const_spec(a.shape) for a in consts],
        out_specs=pl.BlockSpec((1, tm, D_MODEL), lambda b, i: (b, i, 0)),
        scratch_shapes=[pltpu.VMEM((2 * N_FT, 8, FFN_TF), F32),
                        pltpu.VMEM((tm, D_MODEL), F32),
                        pltpu.VMEM((tm, D_MODEL), BF16),
                        pltpu.VMEM((2, 2, tm + 8, FFN_TF), F32),
                        pltpu.VMEM((2, tm, DOWN_TILES * FFN_TF), BF16),
                        pltpu.VMEM((tm, D_MODEL), F32),
                        pltpu.VMEM((tm, D_MODEL), BF16)],
        compiler_params=pltpu.CompilerParams(dimension_semantics=("arbitrary", "arbitrary"),
                                             vmem_limit_bytes=VMEM_LIMIT),
        name="ffn",
    )(x, o_mla, o_dil, mod3, x, o_mla, o_dil, mod3, mod3, *consts)


def _split_halves_layout(w, n_groups, heads, half):
    lead = w.shape[:-1]
    w = w.reshape(*lead, n_groups, heads, 2, half)
    return jnp.swapaxes(w, -3, -2).reshape(*lead, n_groups * heads * 2 * half)


def _segment_matrices():
    i = np.arange(MXU_DIM)
    lane = i % LANES
    seg64 = i // 64
    segpe = (i // LANES) * 4 + (lane % 64) // (MLA_ROPE // 2)
    segpair = (i // LANES) * 2 + (lane % 64) // (DIL_HEAD_DIM // 2)
    same = lambda s: jnp.asarray((s[:, None] == s[None, :]).astype(np.float32), dtype=BF16)
    return same(seg64), same(segpe), same(segpair)


def _rope_constants():
    def inv_freq(d):
        half = d // 2
        return jnp.power(ROPE_THETA, -2.0 * jnp.arange(half, dtype=F32) / d)
    fd, fp = inv_freq(DIL_HEAD_DIM), inv_freq(MLA_ROPE)
    nd, npe = fd.shape[0], fp.shape[0]
    freq = jnp.concatenate([fd, fp, jnp.zeros((64 - nd - npe,), F32)])[:, None]
    lane = np.arange(LANES)
    sign = np.where(lane < 64, -1.0, 1.0)
    e = np.zeros((2 * 64, 4 * LANES), np.float32)
    e[lane % 32, lane] = 1.0
    e[64 + lane % 32, LANES + lane] = sign
    e[nd + lane % 16, 2 * LANES + lane] = 1.0
    e[64 + nd + lane % 16, 3 * LANES + lane] = sign
    return freq, jnp.asarray(e, dtype=BF16)


def kernel(x, c, positions, w_ada, b_ada, g_mix_norm, w_in, g_q_lat, w_q_b, g_kv_lat, w_kv_b,
           g_mla_q_nope, g_mla_q_pe, g_mla_k_nope, g_mla_k_pe, g_dil_q, g_dil_k, w_o,
           g_ffn_norm, w_up, w_conv, b_conv, w_down):
    B, S, D = x.shape
    assert D == D_MODEL and S % PROJ_TM == 0 and S % ATT_TQ == 0 and S % FFN_TM == 0
    assert S <= max(w for w, _ in DIL_PATTERNS)
    seg64, segpe, segpair = _segment_matrices()
    freq, expand = _rope_constants()
    mla_scale = (MLA_NOPE + MLA_ROPE) ** -0.5 * LOG2E
    dil_scale = DIL_HEAD_DIM ** -0.5 * LOG2E
    pos3 = positions.reshape(B, 1, S)

    for l in range(w_ada.shape[0]):
        mod3 = _ada(c, w_ada[l], b_ada[l][None, :]).reshape(B, 6, D)

        wi = w_in[l]
        c_kpe = MLA_Q_LORA + MLA_KV_LORA
        c_qd = c_kpe + MLA_ROPE
        kpe4 = jnp.broadcast_to(wi[:, c_kpe:c_qd].reshape(D, 2, 1, MLA_ROPE // 2),
                                (D, 2, 4, MLA_ROPE // 2)).reshape(D, LANES)
        dil = lambda w: _split_halves_layout(w, 4, 2, DIL_HEAD_DIM // 2)
        w_in_p = jnp.concatenate([
            wi[:, :c_kpe], kpe4, dil(wi[:, c_qd:c_qd + DIL_WIDTH]),
            dil(wi[:, c_qd + DIL_WIDTH:c_qd + 2 * DIL_WIDTH]), wi[:, c_qd + 2 * DIL_WIDTH:]],
            axis=1).astype(BF16)
        wq = w_q_b[l].reshape(MLA_Q_LORA, N_HEADS, MLA_NOPE + MLA_ROPE)
        pe4 = lambda w: _split_halves_layout(w, w.shape[-1] // LANES, 4, MLA_ROPE // 2)
        w_qb_p = jnp.concatenate([wq[:, :, :MLA_NOPE].reshape(MLA_Q_LORA, -1),
                                  pe4(wq[:, :, MLA_NOPE:].reshape(MLA_Q_LORA, -1))], axis=1).astype(BF16)
        wkv = w_kv_b[l].reshape(MLA_KV_LORA, N_HEADS, MLA_NOPE + MLA_V)
        w_kvb_p = jnp.concatenate([wkv[:, :, :MLA_NOPE].reshape(MLA_KV_LORA, -1),
                                   wkv[:, :, MLA_NOPE:].reshape(MLA_KV_LORA, -1)], axis=1).astype(BF16)
        gains = {
            "mix": g_mix_norm[l], "q_lat": g_q_lat[l], "kv_lat": g_kv_lat[l],
            "q_nope": jnp.tile(g_mla_q_nope[l], N_HEADS) * mla_scale,
            "q_pe": pe4(jnp.tile(g_mla_q_pe[l], N_HEADS)) * mla_scale,
            "k_nope": jnp.tile(g_mla_k_nope[l], N_HEADS),
            "k_pe": pe4(jnp.tile(g_mla_k_pe[l], 4)),
            "dil_q": dil(jnp.tile(g_dil_q[l], N_HEADS)) * dil_scale,
            "dil_k": dil(jnp.tile(g_dil_k[l], N_HEADS))}
        gains = jnp.stack([jnp.pad(gains[n], (0, D - w)) for n, w in GAIN_ROWS])
        consts = (gains, w_in_p, w_qb_p, w_kvb_p, seg64, segpe, segpair, freq, expand)
        qn, qpe, kn, kpe, vm, qd, kd, vd = _proj(x, pos3, mod3, consts)

        two = ATT_PAIRS * LANES
        o_mla = _attn_call(_mla_kernel, "mla", (qn, qpe, kn, kpe, vm), (two, LANES, two, None, two),
                           2 * LANES, _mla_bias(), B, S)
        o_dil = _attn_call(_dil_kernel, "dil", (qd, kd, vd), (two, two, two), LANES, _dil_bias(), B, S)

        x = _ffn(x, o_mla, o_dil, mod3, g_ffn_norm[l][None, :], w_o[l].astype(BF16),
                 w_up[l].astype(BF16), w_conv[l], b_conv[l][None, :], w_down[l].astype(BF16))
    return x
```

```python
import numpy as np
import jax
import jax.numpy as jnp
from jax import lax
from jax.experimental import pallas as pl
from jax.experimental.pallas import tpu as pltpu

F32 = jnp.float32
BF16 = jnp.bfloat16

D_MODEL = 1024
N_HEADS = 8
MLA_NOPE = 64
MLA_ROPE = 32
MLA_V = 64
MLA_Q_LORA = 512
MLA_KV_LORA = 256
DIL_HEAD_DIM = 64
DIL_WIDTH = N_HEADS * DIL_HEAD_DIM
DIL_PATTERNS = ((128, 1), (512, 4), (2048, 16))
D_FF = 2816
ROPE_THETA = 10000.0
EPS = 1e-6
MASK_BIAS = -1e30
LOG2E = 1.4426950408889634

LANES = 128
MXU_DIM = 256
VMEM_LIMIT = 56 * 1024 * 1024
ADA_TN = 1024
PROJ_TM = 1024
ATT_TQ = 256
ATT_TK = 256
VT_ROWS = 64 + 16
ATT_PAIRS = 2
FFN_TM = 512
FFN_TF = 256
N_FT = D_FF // FFN_TF
DOWN_TILES = 2
STAGE_AT = N_FT - 4

C_QLAT = 0
C_KVLAT = C_QLAT + MLA_Q_LORA
C_KPE = C_KVLAT + MLA_KV_LORA
C_QD = C_KPE + LANES
C_KD = C_QD + DIL_WIDTH
C_VD = C_KD + DIL_WIDTH


def _dot(a, b):
    return jnp.dot(a, b, preferred_element_type=F32)


def _const_spec(shape):
    nd = len(shape)
    return pl.BlockSpec(shape, lambda *_: (0,) * nd, pipeline_mode=pl.Buffered(1))


def _split_bf16(v):
    hi = v.astype(BF16)
    lo = (v - hi.astype(F32)).astype(BF16)
    return hi, lo


def _ada_kernel(c_ref, w_ref, b_ref, o_ref):
    c = c_ref[...]
    a = c * (1.0 / (1.0 + jnp.exp(-c)))
    n = a.shape[0]
    o2 = _dot(jnp.concatenate(_split_bf16(a), axis=0), w_ref[...].astype(BF16))
    o_ref[...] = o2[:n] + o2[n:] + b_ref[...]


def _ada(c, w_ada, b_ada):
    B = c.shape[0]
    n = w_ada.shape[1]
    return pl.pallas_call(
        _ada_kernel,
        out_shape=jax.ShapeDtypeStruct((B, n), F32),
        grid=(n // ADA_TN,),
        in_specs=[pl.BlockSpec((B, D_MODEL), lambda j: (0, 0)),
                  pl.BlockSpec((D_MODEL, ADA_TN), lambda j: (0, j)),
                  pl.BlockSpec((1, ADA_TN), lambda j: (0, j))],
        out_specs=pl.BlockSpec((B, ADA_TN), lambda j: (0, j)),
        compiler_params=pltpu.CompilerParams(dimension_semantics=("arbitrary",),
                                             vmem_limit_bytes=VMEM_LIMIT),
        name="ada",
    )(c, w_ada, b_ada)


def _rms_rows(v, width):
    ms = jnp.sum(v * v, axis=-1, keepdims=True) * (1.0 / width)
    return v * lax.rsqrt(ms + EPS)


def _seg_rms(v, seg_ref, seg_width):
    outs = []
    for c0 in range(0, v.shape[1], MXU_DIM):
        blk = v[:, c0:c0 + MXU_DIM]
        ss = _dot((blk * blk).astype(BF16), seg_ref[...])
        outs.append(blk * lax.rsqrt(ss * (1.0 / seg_width) + EPS))
    return outs[0] if len(outs) == 1 else jnp.concatenate(outs, axis=1)


def _rope(v, cos, sin_signed, half):
    lane = lax.broadcasted_iota(jnp.int32, (v.shape[0], LANES), 1)
    first = lane % (2 * half) < half
    outs = []
    for c0 in range(0, v.shape[1], LANES):
        blk = v[:, c0:c0 + LANES]
        partner = jnp.where(first, pltpu.roll(blk, LANES - half, 1), pltpu.roll(blk, half, 1))
        outs.append(blk * cos + partner * sin_signed)
    return outs[0] if len(outs) == 1 else jnp.concatenate(outs, axis=1)


GAIN_ROWS = (("mix", D_MODEL), ("q_lat", MLA_Q_LORA), ("kv_lat", MLA_KV_LORA),
             ("q_nope", N_HEADS * MLA_NOPE), ("q_pe", N_HEADS * MLA_ROPE), ("k_nope", N_HEADS * MLA_NOPE),
             ("k_pe", LANES), ("dil_q", DIL_WIDTH), ("dil_k", DIL_WIDTH))


class _Rows:
    def __init__(self, ref):
        self._ref = ref

    def __getattr__(self, name):
        idx, width = next((i, w) for i, (n, w) in enumerate(GAIN_ROWS) if n == name)
        return self._ref[idx:idx + 1, :width]


def _proj_kernel(x_ref, pos_ref, mod_ref, gains_ref, win_ref, wqb_ref, wkvb_ref,
                 seg64_ref, seg32_ref, freq_ref, expand_ref,
                 qn_ref, qpe_ref, kn_ref, kpe_ref, vm_ref, qd_ref, kd_ref, vd_ref):
    g = _Rows(gains_ref)
    hd_half, pe_half = DIL_HEAD_DIM // 2, MLA_ROPE // 2
    x = x_ref[0]
    sh1 = mod_ref[0, 0:1, :]
    sc1 = mod_ref[0, 1:2, :]
    h = (_rms_rows(x, D_MODEL) * (g.mix * (1.0 + sc1)) + sh1).astype(BF16)

    q_lat = _dot(h, win_ref[:, C_QLAT:C_QLAT + MLA_Q_LORA])
    kv_lat = _dot(h, win_ref[:, C_KVLAT:C_KVLAT + MLA_KV_LORA])
    qd_raw = _dot(h, win_ref[:, C_QD:C_QD + DIL_WIDTH])

    pos = pos_ref[0].astype(F32)
    ang = freq_ref[...] * pos
    cs = jnp.concatenate([jnp.cos(ang), jnp.sin(ang)], axis=0).T
    cs_hi, cs_lo = _split_bf16(cs)
    tabs = _dot(cs_hi, expand_ref[...]) + _dot(cs_lo, expand_ref[...])
    cos_d, sin_d = tabs[:, 0:LANES], tabs[:, LANES:2 * LANES]
    cos_p, sin_p = tabs[:, 2 * LANES:3 * LANES], tabs[:, 3 * LANES:4 * LANES]

    q_in = (_rms_rows(q_lat, MLA_Q_LORA) * g.q_lat).astype(BF16)
    q = _dot(q_in, wqb_ref[...])
    kv_in = (_rms_rows(kv_lat, MLA_KV_LORA) * g.kv_lat).astype(BF16)
    kv = _dot(kv_in, wkvb_ref[...])
    kd_raw = _dot(h, win_ref[:, C_KD:C_KD + DIL_WIDTH])
    qd_ref[0] = _rope(_seg_rms(qd_raw, seg64_ref, DIL_HEAD_DIM) * g.dil_q, cos_d, sin_d, hd_half).astype(BF16)
    k_pe = _dot(h, win_ref[:, C_KPE:C_KPE + LANES])
    vd_ref[0] = _dot(h, win_ref[:, C_VD:C_VD + DIL_WIDTH]).astype(BF16)

    n_nope = N_HEADS * MLA_NOPE
    qn_ref[0] = (_seg_rms(q[:, :n_nope], seg64_ref, MLA_NOPE) * g.q_nope).astype(BF16)
    q_pe = _seg_rms(q[:, n_nope:], seg32_ref, MLA_ROPE) * g.q_pe
    qpe_ref[0] = _rope(q_pe, cos_p, sin_p, pe_half).astype(BF16)
    kn_ref[0] = (_seg_rms(kv[:, :n_nope], seg64_ref, MLA_NOPE) * g.k_nope).astype(BF16)
    vm_ref[0] = kv[:, n_nope:].astype(BF16)
    kd_ref[0] = _rope(_seg_rms(kd_raw, seg64_ref, DIL_HEAD_DIM) * g.dil_k, cos_d, sin_d, hd_half).astype(BF16)
    kpe_ref[0] = _rope(_rms_rows(k_pe, LANES) * g.k_pe, cos_p, sin_p, pe_half).astype(BF16)


def _proj(x, pos3, mod3, consts):
    B, S, _ = x.shape
    tm = PROJ_TM
    row = lambda w: pl.BlockSpec((1, tm, w), lambda b, i: (b, i, 0))
    out_widths = (512, 256, 512, LANES, 512, 512, 512, 512)
    in_specs = [row(D_MODEL),
                pl.BlockSpec((1, 1, tm), lambda b, i: (b, 0, i)),
                pl.BlockSpec((1, 6, D_MODEL), lambda b, i: (b, 0, 0))]
    in_specs += [_const_spec(a.shape) for a in consts]
    return pl.pallas_call(
        _proj_kernel,
        out_shape=tuple(jax.ShapeDtypeStruct((B, S, w), BF16) for w in out_widths),
        grid=(B, S // tm),
        in_specs=in_specs,
        out_specs=tuple(row(w) for w in out_widths),
        compiler_params=pltpu.CompilerParams(dimension_semantics=("arbitrary", "arbitrary"),
                                             vmem_limit_bytes=VMEM_LIMIT),
        name="proj",
    )(x, pos3, mod3, *consts)


def _row_iota(shape):
    return lax.broadcasted_iota(jnp.int32, shape, 0)


def _sublane_allreduce(v, op):
    for shift in (4, 2, 1):
        v = op(v, pltpu.roll(v, shift, 0))
    return v


def _transpose_bf16(v):
    return v.astype(F32).T.astype(BF16)


class _Stream:
    def __init__(self, load_qt2, load_k, load_v, store_o):
        self.load_qt2, self.load_k, self.load_v, self.store_o = load_qt2, load_k, load_v, store_o


def _pair_attention(streams, n_rows, bias_ref, far_bias, vt_scr, qt_scr, s_scr, m_scr, l_scr, acc_scr):
    tq, tk = ATT_TQ, ATT_TK
    assert tq == tk
    n_q = n_rows // tq
    hd = LANES // 2
    ids = range(len(streams))

    def put_queries(i):
        for s in ids:
            qt_scr[s] = streams[s].load_qt2(pl.ds(i * tq, tq))

    def put_scores(j, slot):
        k_rows = pl.ds(j * tk, tk)
        for s in ids:
            s_scr[s, slot] = _dot(streams[s].load_k(k_rows), qt_scr[s])

    def consume(j, slot, bias_idx):
        for s in ids:
            st = s_scr[s, slot]
            if bias_idx is not None:
                st = st + bias_ref[bias_idx]
            s3 = st.reshape(tk // 8, 8, 2 * tq)
            m_new = _sublane_allreduce(jnp.max(s3, axis=0), jnp.maximum)
            if j > 0:
                m_old = m_scr[s]
                m_new = jnp.maximum(m_old, m_new)
                alpha = jnp.exp2(m_old - m_new)
            p3 = jnp.exp2(s3 - m_new[None])
            m_scr[s] = m_new
            pt = p3.reshape(tk, 2 * tq).astype(BF16)
            vt = vt_scr[s, j]
            l_new = []
            for h in range(2):
                pv = _dot(vt[h * VT_ROWS:(h + 1) * VT_ROWS, :], pt[:, h * tq:(h + 1) * tq])
                l_new.append(pv[hd:hd + 8, :])
                pv = pv[:hd, :]
                if j > 0:
                    a3 = alpha[:, h * tq:(h + 1) * tq][None]
                    pv = (acc_scr[s, h].reshape(hd // 8, 8, tq) * a3).reshape(hd, tq) + pv
                acc_scr[s, h] = pv
            l_new = jnp.concatenate(l_new, axis=1)
            l_scr[s] = l_new if j == 0 else alpha * l_scr[s] + l_new

    def step(j, slot, bias_idx):
        put_scores(j + 1, 1 - slot)
        consume(j, slot, bias_idx)

    ones = jnp.ones((VT_ROWS - hd, tk), BF16)
    for s in ids:
        for j in range(n_rows // tk):
            vt = _transpose_bf16(streams[s].load_v(pl.ds(j * tk, tk)))
            vt_scr[s, j] = jnp.concatenate([vt[:hd], ones, vt[hd:], ones], axis=0)
    put_queries(0)
    put_scores(0, 0)

    t = 0
    for i in range(n_q):
        for j in range(i):
            step(j, t % 2, None if far_bias is None else min(i - j, far_bias))
            t += 1
        put_queries(min(i + 1, n_q - 1))
        put_scores(0, 1 - t % 2)
        consume(i, t % 2, 0)
        t += 1

        for s in ids:
            inv_l = 1.0 / l_scr[s]
            ot = jnp.concatenate(
                [(acc_scr[s, h].reshape(hd // 8, 8, tq) * inv_l[:, h * tq:(h + 1) * tq][None]).reshape(hd, tq)
                 for h in range(2)], axis=0)
            streams[s].store_o(pl.ds(i * tq, tq), ot.T)


def _lane_block(ref, s):
    cols = slice(s * LANES, (s + 1) * LANES)
    return lambda rows: ref[0, rows, cols]


def _mla_kernel(qn_ref, qpe_ref, kn_ref, kpe_ref, v_ref, bias_ref, o_ref, *scratch):
    feat = _row_iota((2 * LANES, ATT_TQ))
    pe_head = (feat % LANES) // MLA_ROPE
    is_pe = feat >= LANES
    n_streams = o_ref.shape[2] // LANES
    assert n_streams == 2

    def make(s):
        mask_a = (feat < LANES // 2) | (is_pe & (pe_head == 2 * s))
        mask_b = ((feat >= LANES // 2) & (feat < LANES)) | (is_pe & (pe_head == 2 * s + 1))
        qn, kn = _lane_block(qn_ref, s), _lane_block(kn_ref, s)

        def load_qt2(rows):
            qt = jnp.concatenate([qn(rows), qpe_ref[0, rows, :]], axis=1).astype(F32).T
            z = jnp.zeros_like(qt)
            return jnp.concatenate([jnp.where(mask_a, qt, z), jnp.where(mask_b, qt, z)],
                                   axis=1).astype(BF16)

        def load_k(rows):
            return jnp.concatenate([kn(rows), kpe_ref[0, rows, :]], axis=1)

        def store_o(rows, val):
            o_ref[0, rows, s * LANES:(s + 1) * LANES] = val.astype(o_ref.dtype)

        return _Stream(load_qt2, load_k, _lane_block(v_ref, s), store_o)

    _pair_attention([make(s) for s in range(n_streams)], o_ref.shape[1], bias_ref, None, *scratch)


def _dil_kernel(q_ref, k_ref, v_ref, bias_ref, o_ref, *scratch):
    feat = _row_iota((LANES, ATT_TQ))
    is_a = feat < DIL_HEAD_DIM

    def make(s):
        q = _lane_block(q_ref, s)

        def load_qt2(rows):
            qt = q(rows).astype(F32).T
            z = jnp.zeros_like(qt)
            return jnp.concatenate([jnp.where(is_a, qt, z), jnp.where(is_a, z, qt)],
                                   axis=1).astype(BF16)

        def store_o(rows, val):
            o_ref[0, rows, s * LANES:(s + 1) * LANES] = val.astype(o_ref.dtype)

        return _Stream(load_qt2, _lane_block(k_ref, s), _lane_block(v_ref, s), store_o)

    _pair_attention([make(s) for s in range(o_ref.shape[2] // LANES)], o_ref.shape[1], bias_ref,
                    bias_ref.shape[0] - 1, *scratch)


def _attn_call(kernel, name, arrays, widths, qk_width, bias, B, S):
    n_s = ATT_PAIRS
    col = lambda w: pl.BlockSpec((1, S, w), lambda b, g: (b, 0, g))
    shared = pl.BlockSpec((1, S, LANES), lambda b, g: (b, 0, 0))
    in_specs = [shared if w is None else col(w) for w in widths] + [_const_spec(bias.shape)]
    return pl.pallas_call(
        kernel,
        out_shape=jax.ShapeDtypeStruct((B, S, N_HEADS * MLA_V), BF16),
        grid=(B, N_HEADS // 2 // n_s),
        in_specs=in_specs,
        out_specs=col(n_s * LANES),
        scratch_shapes=[pltpu.VMEM((n_s, S // ATT_TK, 2 * VT_ROWS, ATT_TK), BF16),
                        pltpu.VMEM((n_s, qk_width, 2 * ATT_TQ), BF16),
                        pltpu.VMEM((n_s, 2, ATT_TK, 2 * ATT_TQ), F32),
                        pltpu.VMEM((n_s, 8, 2 * ATT_TQ), F32),
                        pltpu.VMEM((n_s, 8, 2 * ATT_TQ), F32),
                        pltpu.VMEM((n_s, 2, LANES // 2, ATT_TQ), F32)],
        compiler_params=pltpu.CompilerParams(dimension_semantics=("arbitrary", "arbitrary"),
                                             vmem_limit_bytes=VMEM_LIMIT),
        name=name,
    )(*arrays, bias)


def _two_heads(tile_qk):
    return np.concatenate([tile_qk.T, tile_qk.T], axis=1)


def _mla_bias():
    d = np.arange(ATT_TQ)[:, None] - np.arange(ATT_TK)[None, :]
    tile = np.where(d >= 0, 0.0, MASK_BIAS).astype(np.float32)
    return jnp.asarray(_two_heads(tile)[None])


def _dil_bias():
    (far_window, far_dil), = [(w, d) for w, d in DIL_PATTERNS if w == max(p[0] for p in DIL_PATTERNS)]
    assert ATT_TK % far_dil == 0
    max_near = max(w for w, d in DIL_PATTERNS if w < far_window)
    n_tiles = -(-max_near // ATT_TK) + 2
    tiles = []
    for blk in range(n_tiles):
        delta = blk * ATT_TK + np.arange(ATT_TQ)[:, None] - np.arange(ATT_TK)[None, :]
        mult = np.zeros(delta.shape, np.int64)
        for window, dil in DIL_PATTERNS:
            mult += (delta >= 0) & (delta % dil == 0) & (delta <= window)
        tile = np.where(mult > 0, np.log2(np.maximum(mult, 1)), MASK_BIAS).astype(np.float32)
        tiles.append(_two_heads(tile))
    return jnp.asarray(np.stack(tiles))


def _ffn_kernel(x0_ref, om0_ref, od0_ref, mod0_ref, xn_ref, omn_ref, odn_ref, modn_ref, mod_ref,
                gffn_ref, wo_ref, wup_ref, wconv_ref, bconv_ref, wdown_ref, out_ref,
                carry_scr, acc_scr, h2_scr, y_scr, a_scr, x1n_scr, h2n_scr):
    half = wo_ref.shape[0] // 2

    def stage_tile(x_ref, om_ref, od_ref, m_ref):
        g1, sh2, sc2 = m_ref[0, 2:3, :], m_ref[0, 3:4, :], m_ref[0, 4:5, :]
        mix = _dot(om_ref[0], wo_ref[:half, :]) + _dot(od_ref[0], wo_ref[half:, :])
        x1 = x_ref[0] + g1 * mix
        x1n_scr[...] = x1
        h2n_scr[...] = (_rms_rows(x1, D_MODEL) * (gffn_ref[...] * (1.0 + sc2)) + sh2).astype(BF16)

    @pl.when((pl.program_id(0) == 0) & (pl.program_id(1) == 0))
    def _():
        stage_tile(x0_ref, om0_ref, od0_ref, mod0_ref)

    g2 = mod_ref[0, 5:6, :]
    h2_scr[...] = h2n_scr[...]
    out_ref[0] = x1n_scr[...]

    @pl.when(pl.program_id(1) == 0)
    def _():
        carry_scr[...] = jnp.zeros(carry_scr.shape, F32)

    def up(f):
        for half_idx, t in enumerate((f, f + N_FT)):
            y = _dot(h2_scr[...], wup_ref[:, t * FFN_TF:(t + 1) * FFN_TF])
            y_scr[f % 2, half_idx, 0:8, :] = carry_scr[t]
            y_scr[f % 2, half_idx, 8:, :] = y
            carry_scr[t] = y[FFN_TM - 8:, :]

    def conv(f, half_idx):
        t = f + half_idx * N_FT
        yb = y_scr.at[f % 2, half_idx]
        cols = slice(t * FFN_TF, (t + 1) * FFN_TF)
        w = wconv_ref[:, cols]
        return (w[2:3, :] * yb[8:8 + FFN_TM, :] + w[1:2, :] * yb[7:7 + FFN_TM, :]
                + w[0:1, :] * yb[6:6 + FFN_TM, :] + bconv_ref[:, cols])

    def down(f0, f1):
        d = _dot(a_scr[(f0 // DOWN_TILES) % 2, :, :(f1 - f0) * FFN_TF],
                 wdown_ref[f0 * FFN_TF:f1 * FFN_TF, :])
        if f0 == 0:
            acc_scr[...] = d
        else:
            acc_scr[...] += d

    up(0)
    pending = None
    for f in range(N_FT):
        if f + 1 < N_FT:
            up(f + 1)
        if pending is not None:
            down(*pending)
            pending = None
        if f == STAGE_AT:
            stage_tile(xn_ref, omn_ref, odn_ref, modn_ref)
        gate = conv(f, 0)
        val = conv(f, 1)
        k = f % DOWN_TILES
        a_scr[(f // DOWN_TILES) % 2, :, k * FFN_TF:(k + 1) * FFN_TF] = (
            gate * (1.0 / (1.0 + jnp.exp(-gate))) * val).astype(BF16)
        if k == DOWN_TILES - 1 or f == N_FT - 1:
            pending = (f - k, f + 1)
    down(*pending)
    out_ref[0] = out_ref[0] + g2 * acc_scr[...]


def _ffn(x, o_mla, o_dil, mod3, g_ffn, w_o, w_up, w_conv, b_conv, w_down):
    B, S, _ = x.shape
    tm = FFN_TM
    nt = S // tm

    def nxt(b, i):
        t = jnp.minimum(b * nt + i + 1, B * nt - 1)
        return t // nt, t % nt

    first = lambda w: pl.BlockSpec((1, tm, w), lambda b, i: (0, 0, 0), pipeline_mode=pl.Buffered(1))
    following = lambda w: pl.BlockSpec((1, tm, w), lambda b, i: (*nxt(b, i), 0))
    mod_spec = lambda fn: pl.BlockSpec((1, 6, D_MODEL), fn)
    wm, wd = o_mla.shape[2], o_dil.shape[2]
    consts = (g_ffn, w_o, w_up, w_conv, b_conv, w_down)
    return pl.pallas_call(
        _ffn_kernel,
        out_shape=jax.ShapeDtypeStruct(x.shape, F32),
        grid=(B, nt),
        in_specs=[first(D_MODEL), first(wm), first(wd), mod_spec(lambda b, i: (0, 0, 0)),
                  following(D_MODEL), following(wm), following(wd),
                  mod_spec(lambda b, i: (nxt(b, i)[0], 0, 0)), mod_spec(lambda b, i: (b, 0, 0))]
                 + [_const_spec(a.shape) for a in consts],
        out_specs=pl.BlockSpec((1, tm, D_MODEL), lambda b, i: (b, i, 0)),
        scratch_shapes=[pltpu.VMEM((2 * N_FT, 8, FFN_TF), F32),
                        pltpu.VMEM((tm, D_MODEL), F32),
                        pltpu.VMEM((tm, D_MODEL), BF16),
                        pltpu.VMEM((2, 2, tm + 8, FFN_TF), F32),
                        pltpu.VMEM((2, tm, DOWN_TILES * FFN_TF), BF16),
                        pltpu.VMEM((tm, D_MODEL), F32),
                        pltpu.VMEM((tm, D_MODEL), BF16)],
        compiler_params=pltpu.CompilerParams(dimension_semantics=("arbitrary", "arbitrary"),
                                             vmem_limit_bytes=VMEM_LIMIT),
        name="ffn",
    )(x, o_mla, o_dil, mod3, x, o_mla, o_dil, mod3, mod3, *consts)


def _segment_matrices():
    i = np.arange(MXU_DIM)
    seg64 = i // 64
    seg32 = i // MLA_ROPE
    same = lambda s: jnp.asarray((s[:, None] == s[None, :]).astype(np.float32), dtype=BF16)
    return same(seg64), same(seg32)


def _rope_constants():
    def inv_freq(d):
        half = d // 2
        return jnp.power(ROPE_THETA, -2.0 * jnp.arange(half, dtype=F32) / d)
    fd, fp = inv_freq(DIL_HEAD_DIM), inv_freq(MLA_ROPE)
    nd, npe = fd.shape[0], fp.shape[0]
    freq = jnp.concatenate([fd, fp, jnp.zeros((64 - nd - npe,), F32)])[:, None]
    lane = np.arange(LANES)
    sign = lambda half: np.where(lane % (2 * half) < half, -1.0, 1.0)
    e = np.zeros((2 * 64, 4 * LANES), np.float32)
    e[lane % nd, lane] = 1.0
    e[64 + lane % nd, LANES + lane] = sign(nd)
    e[nd + lane % npe, 2 * LANES + lane] = 1.0
    e[64 + nd + lane % npe, 3 * LANES + lane] = sign(npe)
    return freq, jnp.asarray(e, dtype=BF16)


def kernel(x, c, positions, w_ada, b_ada, g_mix_norm, w_in, g_q_lat, w_q_b, g_kv_lat, w_kv_b,
           g_mla_q_nope, g_mla_q_pe, g_mla_k_nope, g_mla_k_pe, g_dil_q, g_dil_k, w_o,
           g_ffn_norm, w_up, w_conv, b_conv, w_down):
    B, S, D = x.shape
    assert D == D_MODEL and S % PROJ_TM == 0 and S % ATT_TQ == 0 and S % FFN_TM == 0
    assert S <= max(w for w, _ in DIL_PATTERNS)
    seg64, seg32 = _segment_matrices()
    freq, expand = _rope_constants()
    mla_scale = (MLA_NOPE + MLA_ROPE) ** -0.5 * LOG2E
    dil_scale = DIL_HEAD_DIM ** -0.5 * LOG2E
    pos3 = positions.reshape(B, 1, S)

    for l in range(w_ada.shape[0]):
        mod3 = _ada(c, w_ada[l], b_ada[l][None, :]).reshape(B, 6, D)

        wi = w_in[l]
        c_kpe = MLA_Q_LORA + MLA_KV_LORA
        c_qd = c_kpe + MLA_ROPE
        w_in_p = jnp.concatenate([wi[:, :c_kpe], jnp.tile(wi[:, c_kpe:c_qd], (1, LANES // MLA_ROPE)),
                                  wi[:, c_qd:]], axis=1).astype(BF16)
        wq = w_q_b[l].reshape(MLA_Q_LORA, N_HEADS, MLA_NOPE + MLA_ROPE)
        w_qb_p = jnp.concatenate([wq[:, :, :MLA_NOPE].reshape(MLA_Q_LORA, -1),
                                  wq[:, :, MLA_NOPE:].reshape(MLA_Q_LORA, -1)], axis=1).astype(BF16)
        wkv = w_kv_b[l].reshape(MLA_KV_LORA, N_HEADS, MLA_NOPE + MLA_V)
        w_kvb_p = jnp.concatenate([wkv[:, :, :MLA_NOPE].reshape(MLA_KV_LORA, -1),
                                   wkv[:, :, MLA_NOPE:].reshape(MLA_KV_LORA, -1)], axis=1).astype(BF16)
        gains = {
            "mix": g_mix_norm[l], "q_lat": g_q_lat[l], "kv_lat": g_kv_lat[l],
            "q_nope": jnp.tile(g_mla_q_nope[l], N_HEADS) * mla_scale,
            "q_pe": jnp.tile(g_mla_q_pe[l], N_HEADS) * mla_scale,
            "k_nope": jnp.tile(g_mla_k_nope[l], N_HEADS),
            "k_pe": jnp.tile(g_mla_k_pe[l], LANES // MLA_ROPE),
            "dil_q": jnp.tile(g_dil_q[l], N_HEADS) * dil_scale,
            "dil_k": jnp.tile(g_dil_k[l], N_HEADS)}
        gains = jnp.stack([jnp.pad(gains[n], (0, D - w)) for n, w in GAIN_ROWS])
        consts = (gains, w_in_p, w_qb_p, w_kvb_p, seg64, seg32, freq, expand)
        qn, qpe, kn, kpe, vm, qd, kd, vd = _proj(x, pos3, mod3, consts)

        two = ATT_PAIRS * LANES
        o_mla = _attn_call(_mla_kernel, "mla", (qn, qpe, kn, kpe, vm), (two, LANES, two, None, two),
                           2 * LANES, _mla_bias(), B, S)
        o_dil = _attn_call(_dil_kernel, "dil", (qd, kd, vd), (two, two, two), LANES, _dil_bias(), B, S)

        x = _ffn(x, o_mla, o_dil, mod3, g_ffn_norm[l][None, :], w_o[l].astype(BF16),
                 w_up[l].astype(BF16), w_conv[l], b_conv[l][None, :], w_down[l].astype(BF16))
    return x
```

```python
import numpy as np
import jax
import jax.numpy as jnp
from jax import lax
from jax.experimental import pallas as pl
from jax.experimental.pallas import tpu as pltpu

F32 = jnp.float32
BF16 = jnp.bfloat16

D_MODEL = 1024
N_HEADS = 8
MLA_NOPE = 64
MLA_ROPE = 32
MLA_V = 64
MLA_Q_LORA = 512
MLA_KV_LORA = 256
DIL_HEAD_DIM = 64
DIL_WIDTH = N_HEADS * DIL_HEAD_DIM
DIL_PATTERNS = ((128, 1), (512, 4), (2048, 16))
D_FF = 2816
ROPE_THETA = 10000.0
EPS = 1e-6
MASK_BIAS = -1e30
LOG2E = 1.4426950408889634

LANES = 128
MXU_DIM = 256
VMEM_LIMIT = 56 * 1024 * 1024
ADA_TN = 1024
PROJ_TM = 1024
ATT_TQ = 256
ATT_TK = 256
VT_ROWS = 64 + 16
ATT_PAIRS = 2
FFN_TM = 512
FFN_TF = 256
N_FT = D_FF // FFN_TF
DOWN_TILES = 2
STAGE_AT = N_FT - 3

C_QLAT = 0
C_KVLAT = C_QLAT + MLA_Q_LORA
C_KPE = C_KVLAT + MLA_KV_LORA
C_QD = C_KPE + LANES
C_KD = C_QD + DIL_WIDTH
C_VD = C_KD + DIL_WIDTH


def _dot(a, b):
    return jnp.dot(a, b, preferred_element_type=F32)


def _const_spec(shape):
    nd = len(shape)
    return pl.BlockSpec(shape, lambda *_: (0,) * nd, pipeline_mode=pl.Buffered(1))


def _split_bf16(v):
    hi = v.astype(BF16)
    lo = (v - hi.astype(F32)).astype(BF16)
    return hi, lo


def _ada_kernel(c_ref, w_ref, b_ref, o_ref):
    c = c_ref[...]
    a = c * (1.0 / (1.0 + jnp.exp(-c)))
    n = a.shape[0]
    o2 = _dot(jnp.concatenate(_split_bf16(a), axis=0), w_ref[...].astype(BF16))
    o_ref[...] = o2[:n] + o2[n:] + b_ref[...]


def _ada(c, w_ada, b_ada):
    B = c.shape[0]
    n = w_ada.shape[1]
    return pl.pallas_call(
        _ada_kernel,
        out_shape=jax.ShapeDtypeStruct((B, n), F32),
        grid=(n // ADA_TN,),
        in_specs=[pl.BlockSpec((B, D_MODEL), lambda j: (0, 0)),
                  pl.BlockSpec((D_MODEL, ADA_TN), lambda j: (0, j)),
                  pl.BlockSpec((1, ADA_TN), lambda j: (0, j))],
        out_specs=pl.BlockSpec((B, ADA_TN), lambda j: (0, j)),
        compiler_params=pltpu.CompilerParams(dimension_semantics=("arbitrary",),
                                             vmem_limit_bytes=VMEM_LIMIT),
        name="ada",
    )(c, w_ada, b_ada)


def _rms_rows(v, width):
    ms = jnp.sum(v * v, axis=-1, keepdims=True) * (1.0 / width)
    return v * lax.rsqrt(ms + EPS)


def _seg_rms(v, seg_ref, seg_width):
    outs = []
    for c0 in range(0, v.shape[1], MXU_DIM):
        blk = v[:, c0:c0 + MXU_DIM]
        ss = _dot((blk * blk).astype(BF16), seg_ref[...])
        outs.append(blk * lax.rsqrt(ss * (1.0 / seg_width) + EPS))
    return outs[0] if len(outs) == 1 else jnp.concatenate(outs, axis=1)


def _rope(v, cos, sin_signed, half):
    lane = lax.broadcasted_iota(jnp.int32, (v.shape[0], LANES), 1)
    first = lane % (2 * half) < half
    outs = []
    for c0 in range(0, v.shape[1], LANES):
        blk = v[:, c0:c0 + LANES]
        partner = jnp.where(first, pltpu.roll(blk, LANES - half, 1), pltpu.roll(blk, half, 1))
        outs.append(blk * cos + partner * sin_signed)
    return outs[0] if len(outs) == 1 else jnp.concatenate(outs, axis=1)


GAIN_ROWS = (("mix", D_MODEL), ("q_lat", MLA_Q_LORA), ("kv_lat", MLA_KV_LORA),
             ("q_nope", N_HEADS * MLA_NOPE), ("q_pe", N_HEADS * MLA_ROPE), ("k_nope", N_HEADS * MLA_NOPE),
             ("k_pe", LANES), ("dil_q", DIL_WIDTH), ("dil_k", DIL_WIDTH))


class _Rows:
    def __init__(self, ref):
        self._ref = ref

    def __getattr__(self, name):
        idx, width = next((i, w) for i, (n, w) in enumerate(GAIN_ROWS) if n == name)
        return self._ref[idx:idx + 1, :width]


def _proj_kernel(x_ref, pos_ref, mod_ref, gains_ref, win_ref, wqb_ref, wkvb_ref,
                 seg64_ref, seg32_ref, freq_ref, expand_ref,
                 qn_ref, qpe_ref, kn_ref, kpe_ref, vm_ref, qd_ref, kd_ref, vd_ref):
    g = _Rows(gains_ref)
    hd_half, pe_half = DIL_HEAD_DIM // 2, MLA_ROPE // 2
    x = x_ref[0]
    sh1 = mod_ref[0, 0:1, :]
    sc1 = mod_ref[0, 1:2, :]
    h = (_rms_rows(x, D_MODEL) * (g.mix * (1.0 + sc1)) + sh1).astype(BF16)

    q_lat = _dot(h, win_ref[:, C_QLAT:C_QLAT + MLA_Q_LORA])
    kv_lat = _dot(h, win_ref[:, C_KVLAT:C_KVLAT + MLA_KV_LORA])
    qd_raw = _dot(h, win_ref[:, C_QD:C_QD + DIL_WIDTH])

    pos = pos_ref[0].astype(F32)
    ang = freq_ref[...] * pos
    cs = jnp.concatenate([jnp.cos(ang), jnp.sin(ang)], axis=0).T
    cs_hi, cs_lo = _split_bf16(cs)
    tabs = _dot(cs_hi, expand_ref[...]) + _dot(cs_lo, expand_ref[...])
    cos_d, sin_d = tabs[:, 0:LANES], tabs[:, LANES:2 * LANES]
    cos_p, sin_p = tabs[:, 2 * LANES:3 * LANES], tabs[:, 3 * LANES:4 * LANES]

    q_in = (_rms_rows(q_lat, MLA_Q_LORA) * g.q_lat).astype(BF16)
    q = _dot(q_in, wqb_ref[...])
    kv_in = (_rms_rows(kv_lat, MLA_KV_LORA) * g.kv_lat).astype(BF16)
    kv = _dot(kv_in, wkvb_ref[...])
    kd_raw = _dot(h, win_ref[:, C_KD:C_KD + DIL_WIDTH])
    qd_ref[0] = _rope(_seg_rms(qd_raw, seg64_ref, DIL_HEAD_DIM) * g.dil_q, cos_d, sin_d, hd_half).astype(BF16)
    k_pe = _dot(h, win_ref[:, C_KPE:C_KPE + LANES])
    vd_ref[0] = _dot(h, win_ref[:, C_VD:C_VD + DIL_WIDTH]).astype(BF16)

    n_nope = N_HEADS * MLA_NOPE
    qn_ref[0] = (_seg_rms(q[:, :n_nope], seg64_ref, MLA_NOPE) * g.q_nope).astype(BF16)
    q_pe = _seg_rms(q[:, n_nope:], seg32_ref, MLA_ROPE) * g.q_pe
    qpe_ref[0] = _rope(q_pe, cos_p, sin_p, pe_half).astype(BF16)
    kn_ref[0] = (_seg_rms(kv[:, :n_nope], seg64_ref, MLA_NOPE) * g.k_nope).astype(BF16)
    vm_ref[0] = kv[:, n_nope:].astype(BF16)
    kd_ref[0] = _rope(_seg_rms(kd_raw, seg64_ref, DIL_HEAD_DIM) * g.dil_k, cos_d, sin_d, hd_half).astype(BF16)
    kpe_ref[0] = _rope(_rms_rows(k_pe, LANES) * g.k_pe, cos_p, sin_p, pe_half).astype(BF16)


def _proj(x, pos3, mod3, consts):
    B, S, _ = x.shape
    tm = PROJ_TM
    row = lambda w: pl.BlockSpec((1, tm, w), lambda b, i: (b, i, 0))
    out_widths = (512, 256, 512, LANES, 512, 512, 512, 512)
    in_specs = [row(D_MODEL),
                pl.BlockSpec((1, 1, tm), lambda b, i: (b, 0, i)),
                pl.BlockSpec((1, 6, D_MODEL), lambda b, i: (b, 0, 0))]
    in_specs += [_const_spec(a.shape) for a in consts]
    return pl.pallas_call(
        _proj_kernel,
        out_shape=tuple(jax.ShapeDtypeStruct((B, S, w), BF16) for w in out_widths),
        grid=(B, S // tm),
        in_specs=in_specs,
        out_specs=tuple(row(w) for w in out_widths),
        compiler_params=pltpu.CompilerParams(dimension_semantics=("arbitrary", "arbitrary"),
                                             vmem_limit_bytes=VMEM_LIMIT),
        name="proj",
    )(x, pos3, mod3, *consts)


def _row_iota(shape):
    return lax.broadcasted_iota(jnp.int32, shape, 0)


def _sublane_allreduce(v, op):
    for shift in (4, 2, 1):
        v = op(v, pltpu.roll(v, shift, 0))
    return v


def _transpose_bf16(v):
    return v.astype(F32).T.astype(BF16)


class _Stream:
    def __init__(self, load_qt2, load_k, load_v, store_o):
        self.load_qt2, self.load_k, self.load_v, self.store_o = load_qt2, load_k, load_v, store_o


def _pair_attention(streams, n_rows, bias_ref, far_bias, vt_scr, qt_scr, s_scr, m_scr, l_scr, acc_scr):
    tq, tk = ATT_TQ, ATT_TK
    assert tq == tk
    n_q = n_rows // tq
    hd = LANES // 2
    ids = range(len(streams))

    def put_queries(i):
        for s in ids:
            qt_scr[s] = streams[s].load_qt2(pl.ds(i * tq, tq))

    def put_scores(j, slot):
        k_rows = pl.ds(j * tk, tk)
        for s in ids:
            s_scr[s, slot] = _dot(streams[s].load_k(k_rows), qt_scr[s])

    def consume(j, slot, bias_idx):
        for s in ids:
            st = s_scr[s, slot]
            if bias_idx is not None:
                st = st + bias_ref[bias_idx]
            s3 = st.reshape(tk // 8, 8, 2 * tq)
            m_new = _sublane_allreduce(jnp.max(s3, axis=0), jnp.maximum)
            if j > 0:
                m_old = m_scr[s]
                m_new = jnp.maximum(m_old, m_new)
                alpha = jnp.exp2(m_old - m_new)
            p3 = jnp.exp2(s3 - m_new[None])
            m_scr[s] = m_new
            pt = p3.reshape(tk, 2 * tq).astype(BF16)
            vt = vt_scr[s, j]
            l_new = []
            for h in range(2):
                pv = _dot(vt[h * VT_ROWS:(h + 1) * VT_ROWS, :], pt[:, h * tq:(h + 1) * tq])
                l_new.append(pv[hd:hd + 8, :])
                pv = pv[:hd, :]
                if j > 0:
                    a3 = alpha[:, h * tq:(h + 1) * tq][None]
                    pv = (acc_scr[s, h].reshape(hd // 8, 8, tq) * a3).reshape(hd, tq) + pv
                acc_scr[s, h] = pv
            l_new = jnp.concatenate(l_new, axis=1)
            l_scr[s] = l_new if j == 0 else alpha * l_scr[s] + l_new

    def step(j, slot, bias_idx):
        put_scores(j + 1, 1 - slot)
        consume(j, slot, bias_idx)

    ones = jnp.ones((VT_ROWS - hd, tk), BF16)
    for s in ids:
        for j in range(n_rows // tk):
            vt = _transpose_bf16(streams[s].load_v(pl.ds(j * tk, tk)))
            vt_scr[s, j] = jnp.concatenate([vt[:hd], ones, vt[hd:], ones], axis=0)
    put_queries(0)
    put_scores(0, 0)

    t = 0
    for i in range(n_q):
        for j in range(i):
            step(j, t % 2, None if far_bias is None else min(i - j, far_bias))
            t += 1
        put_queries(min(i + 1, n_q - 1))
        put_scores(0, 1 - t % 2)
        consume(i, t % 2, 0)
        t += 1

        for s in ids:
            inv_l = 1.0 / l_scr[s]
            ot = jnp.concatenate(
                [(acc_scr[s, h].reshape(hd // 8, 8, tq) * inv_l[:, h * tq:(h + 1) * tq][None]).reshape(hd, tq)
                 for h in range(2)], axis=0)
            streams[s].store_o(pl.ds(i * tq, tq), ot.T)


def _lane_block(ref, s):
    cols = slice(s * LANES, (s + 1) * LANES)
    return lambda rows: ref[0, rows, cols]


def _mla_kernel(qn_ref, qpe_ref, kn_ref, kpe_ref, v_ref, bias_ref, o_ref, *scratch):
    feat = _row_iota((2 * LANES, ATT_TQ))
    pe_head = (feat % LANES) // MLA_ROPE
    is_pe = feat >= LANES
    n_streams = o_ref.shape[2] // LANES
    assert n_streams == 2

    def make(s):
        mask_a = (feat < LANES // 2) | (is_pe & (pe_head == 2 * s))
        mask_b = ((feat >= LANES // 2) & (feat < LANES)) | (is_pe & (pe_head == 2 * s + 1))
        qn, kn = _lane_block(qn_ref, s), _lane_block(kn_ref, s)

        def load_qt2(rows):
            qt = jnp.concatenate([qn(rows), qpe_ref[0, rows, :]], axis=1).astype(F32).T
            z = jnp.zeros_like(qt)
            return jnp.concatenate([jnp.where(mask_a, qt, z), jnp.where(mask_b, qt, z)],
                                   axis=1).astype(BF16)

        def load_k(rows):
            return jnp.concatenate([kn(rows), kpe_ref[0, rows, :]], axis=1)

        def store_o(rows, val):
            o_ref[0, rows, s * LANES:(s + 1) * LANES] = val.astype(o_ref.dtype)

        return _Stream(load_qt2, load_k, _lane_block(v_ref, s), store_o)

    _pair_attention([make(s) for s in range(n_streams)], o_ref.shape[1], bias_ref, None, *scratch)


def _dil_kernel(q_ref, k_ref, v_ref, bias_ref, o_ref, *scratch):
    feat = _row_iota((LANES, ATT_TQ))
    is_a = feat < DIL_HEAD_DIM

    def make(s):
        q = _lane_block(q_ref, s)

        def load_qt2(rows):
            qt = q(rows).astype(F32).T
            z = jnp.zeros_like(qt)
            return jnp.concatenate([jnp.where(is_a, qt, z), jnp.where(is_a, z, qt)],
                                   axis=1).astype(BF16)

        def store_o(rows, val):
            o_ref[0, rows, s * LANES:(s + 1) * LANES] = val.astype(o_ref.dtype)

        return _Stream(load_qt2, _lane_block(k_ref, s), _lane_block(v_ref, s), store_o)

    _pair_attention([make(s) for s in range(o_ref.shape[2] // LANES)], o_ref.shape[1], bias_ref,
                    bias_ref.shape[0] - 1, *scratch)


def _attn_call(kernel, name, arrays, widths, qk_width, bias, B, S):
    n_s = ATT_PAIRS
    col = lambda w: pl.BlockSpec((1, S, w), lambda b, g: (b, 0, g))
    shared = pl.BlockSpec((1, S, LANES), lambda b, g: (b, 0, 0))
    in_specs = [shared if w is None else col(w) for w in widths] + [_const_spec(bias.shape)]
    return pl.pallas_call(
        kernel,
        out_shape=jax.ShapeDtypeStruct((B, S, N_HEADS * MLA_V), BF16),
        grid=(B, N_HEADS // 2 // n_s),
        in_specs=in_specs,
        out_specs=col(n_s * LANES),
        scratch_shapes=[pltpu.VMEM((n_s, S // ATT_TK, 2 * VT_ROWS, ATT_TK), BF16),
                        pltpu.VMEM((n_s, qk_width, 2 * ATT_TQ), BF16),
                        pltpu.VMEM((n_s, 2, ATT_TK, 2 * ATT_TQ), F32),
                        pltpu.VMEM((n_s, 8, 2 * ATT_TQ), F32),
                        pltpu.VMEM((n_s, 8, 2 * ATT_TQ), F32),
                        pltpu.VMEM((n_s, 2, LANES // 2, ATT_TQ), F32)],
        compiler_params=pltpu.CompilerParams(dimension_semantics=("arbitrary", "arbitrary"),
                                             vmem_limit_bytes=VMEM_LIMIT),
        name=name,
    )(*arrays, bias)


def _two_heads(tile_qk):
    return np.concatenate([tile_qk.T, tile_qk.T], axis=1)


def _mla_bias():
    d = np.arange(ATT_TQ)[:, None] - np.arange(ATT_TK)[None, :]
    tile = np.where(d >= 0, 0.0, MASK_BIAS).astype(np.float32)
    return jnp.asarray(_two_heads(tile)[None])


def _dil_bias():
    (far_window, far_dil), = [(w, d) for w, d in DIL_PATTERNS if w == max(p[0] for p in DIL_PATTERNS)]
    assert ATT_TK % far_dil == 0
    max_near = max(w for w, d in DIL_PATTERNS if w < far_window)
    n_tiles = -(-max_near // ATT_TK) + 2
    tiles = []
    for blk in range(n_tiles):
        delta = blk * ATT_TK + np.arange(ATT_TQ)[:, None] - np.arange(ATT_TK)[None, :]
        mult = np.zeros(delta.shape, np.int64)
        for window, dil in DIL_PATTERNS:
            mult += (delta >= 0) & (delta % dil == 0) & (delta <= window)
        tile = np.where(mult > 0, np.log2(np.maximum(mult, 1)), MASK_BIAS).astype(np.float32)
        tiles.append(_two_heads(tile))
    return jnp.asarray(np.stack(tiles))


def _ffn_kernel(x0_ref, om0_ref, od0_ref, mod0_ref, xn_ref, omn_ref, odn_ref, modn_ref, mod_ref,
                gffn_ref, wo_ref, wup_ref, wconv_ref, bconv_ref, wdown_ref, out_ref,
                carry_scr, acc_scr, h2_scr, y_scr, a_scr, x1n_scr, h2n_scr):
    half = wo_ref.shape[0] // 2

    def stage_tile(x_ref, om_ref, od_ref, m_ref):
        g1, sh2, sc2 = m_ref[0, 2:3, :], m_ref[0, 3:4, :], m_ref[0, 4:5, :]
        mix = _dot(om_ref[0], wo_ref[:half, :]) + _dot(od_ref[0], wo_ref[half:, :])
        x1 = x_ref[0] + g1 * mix
        x1n_scr[...] = x1
        h2n_scr[...] = (_rms_rows(x1, D_MODEL) * (gffn_ref[...] * (1.0 + sc2)) + sh2).astype(BF16)

    @pl.when((pl.program_id(0) == 0) & (pl.program_id(1) == 0))
    def _():
        stage_tile(x0_ref, om0_ref, od0_ref, mod0_ref)

    g2 = mod_ref[0, 5:6, :]
    h2_scr[...] = h2n_scr[...]
    out_ref[0] = x1n_scr[...]

    @pl.when(pl.program_id(1) == 0)
    def _():
        carry_scr[...] = jnp.zeros(carry_scr.shape, F32)

    def up(f):
        for half_idx, t in enumerate((f, f + N_FT)):
            y = _dot(h2_scr[...], wup_ref[:, t * FFN_TF:(t + 1) * FFN_TF])
            y_scr[f % 2, half_idx, 0:8, :] = carry_scr[t]
            y_scr[f % 2, half_idx, 8:, :] = y
            carry_scr[t] = y[FFN_TM - 8:, :]

    def conv(f, half_idx):
        t = f + half_idx * N_FT
        yb = y_scr.at[f % 2, half_idx]
        cols = slice(t * FFN_TF, (t + 1) * FFN_TF)
        w = wconv_ref[:, cols]
        return (w[2:3, :] * yb[8:8 + FFN_TM, :] + w[1:2, :] * yb[7:7 + FFN_TM, :]
                + w[0:1, :] * yb[6:6 + FFN_TM, :] + bconv_ref[:, cols])

    def down(f0, f1):
        d = _dot(a_scr[(f0 // DOWN_TILES) % 2, :, :(f1 - f0) * FFN_TF],
                 wdown_ref[f0 * FFN_TF:f1 * FFN_TF, :])
        if f0 == 0:
            acc_scr[...] = d
        else:
            acc_scr[...] += d

    up(0)
    pending = None
    for f in range(N_FT):
        if f + 1 < N_FT:
            up(f + 1)
        if pending is not None:
            down(*pending)
            pending = None
        if f == STAGE_AT:
            stage_tile(xn_ref, omn_ref, odn_ref, modn_ref)
        gate = conv(f, 0)
        val = conv(f, 1)
        k = f % DOWN_TILES
        a_scr[(f // DOWN_TILES) % 2, :, k * FFN_TF:(k + 1) * FFN_TF] = (
            gate * (1.0 / (1.0 + jnp.exp(-gate))) * val).astype(BF16)
        if k == DOWN_TILES - 1 or f == N_FT - 1:
            pending = (f - k, f + 1)
    down(*pending)
    out_ref[0] = out_ref[0] + g2 * acc_scr[...]


def _ffn(x, o_mla, o_dil, mod3, g_ffn, w_o, w_up, w_conv, b_conv, w_down):
    B, S, _ = x.shape
    tm = FFN_TM
    nt = S // tm

    def nxt(b, i):
        t = jnp.minimum(b * nt + i + 1, B * nt - 1)
        return t // nt, t % nt

    first = lambda w: pl.BlockSpec((1, tm, w), lambda b, i: (0, 0, 0), pipeline_mode=pl.Buffered(1))
    following = lambda w: pl.BlockSpec((1, tm, w), lambda b, i: (*nxt(b, i), 0))
    mod_spec = lambda fn: pl.BlockSpec((1, 6, D_MODEL), fn)
    wm, wd = o_mla.shape[2], o_dil.shape[2]
    consts = (g_ffn, w_o, w_up, w_conv, b_conv, w_down)
    return pl.pallas_call(
        _ffn_kernel,
        out_shape=jax.ShapeDtypeStruct(x.shape, F32),
        grid=(B, nt),
        in_specs=[first(D_MODEL), first(wm), first(wd), mod_spec(lambda b, i: (0, 0, 0)),
                  following(D_MODEL), following(wm), following(wd),
                  mod_spec(lambda b, i: (nxt(b, i)[0], 0, 0)), mod_spec(lambda b, i: (b, 0, 0))]
                 + [_const_spec(a.shape) for a in consts],
        out_specs=pl.BlockSpec((1, tm, D_MODEL), lambda b, i: (b, i, 0)),
        scratch_shapes=[pltpu.VMEM((2 * N_FT, 8, FFN_TF), F32),
                        pltpu.VMEM((tm, D_MODEL), F32),
                        pltpu.VMEM((tm, D_MODEL), BF16),
                        pltpu.VMEM((2, 2, tm + 8, FFN_TF), F32),
                        pltpu.VMEM((2, tm, DOWN_TILES * FFN_TF), BF16),
                        pltpu.VMEM((tm, D_MODEL), F32),
                        pltpu.VMEM((tm, D_MODEL), BF16)],
        compiler_params=pltpu.CompilerParams(dimension_semantics=("arbitrary", "arbitrary"),
                                             vmem_limit_bytes=VMEM_LIMIT),
        name="ffn",
    )(x, o_mla, o_dil, mod3, x, o_mla, o_dil, mod3, mod3, *consts)


def _segment_matrices():
    i = np.arange(MXU_DIM)
    seg64 = i // 64
    seg32 = i // MLA_ROPE
    same = lambda s: jnp.asarray((s[:, None] == s[None, :]).astype(np.float32), dtype=BF16)
    return same(seg64), same(seg32)


def _rope_constants():
    def inv_freq(d):
        half = d // 2
        return jnp.power(ROPE_THETA, -2.0 * jnp.arange(half, dtype=F32) / d)
    fd, fp = inv_freq(DIL_HEAD_DIM), inv_freq(MLA_ROPE)
    nd, npe = fd.shape[0], fp.shape[0]
    freq = jnp.concatenate([fd, fp, jnp.zeros((64 - nd - npe,), F32)])[:, None]
    lane = np.arange(LANES)
    sign = lambda half: np.where(lane % (2 * half) < half, -1.0, 1.0)
    e = np.zeros((2 * 64, 4 * LANES), np.float32)
    e[lane % nd, lane] = 1.0
    e[64 + lane % nd, LANES + lane] = sign(nd)
    e[nd + lane % npe, 2 * LANES + lane] = 1.0
    e[64 + nd + lane % npe, 3 * LANES + lane] = sign(npe)
    return freq, jnp.asarray(e, dtype=BF16)


def kernel(x, c, positions, w_ada, b_ada, g_mix_norm, w_in, g_q_lat, w_q_b, g_kv_lat, w_kv_b,
           g_mla_q_nope, g_mla_q_pe, g_mla_k_nope, g_mla_k_pe, g_dil_q, g_dil_k, w_o,
           g_ffn_norm, w_up, w_conv, b_conv, w_down):
    B, S, D = x.shape
    assert D == D_MODEL and S % PROJ_TM == 0 and S % ATT_TQ == 0 and S % FFN_TM == 0
    assert S <= max(w for w, _ in DIL_PATTERNS)
    seg64, seg32 = _segment_matrices()
    freq, expand = _rope_constants()
    mla_scale = (MLA_NOPE + MLA_ROPE) ** -0.5 * LOG2E
    dil_scale = DIL_HEAD_DIM ** -0.5 * LOG2E
    pos3 = positions.reshape(B, 1, S)

    for l in range(w_ada.shape[0]):
        mod3 = _ada(c, w_ada[l], b_ada[l][None, :]).reshape(B, 6, D)

        wi = w_in[l]
        c_kpe = MLA_Q_LORA + MLA_KV_LORA
        c_qd = c_kpe + MLA_ROPE
        w_in_p = jnp.concatenate([wi[:, :c_kpe], jnp.tile(wi[:, c_kpe:c_qd], (1, LANES // MLA_ROPE)),
                                  wi[:, c_qd:]], axis=1).astype(BF16)
        wq = w_q_b[l].reshape(MLA_Q_LORA, N_HEADS, MLA_NOPE + MLA_ROPE)
        w_qb_p = jnp.concatenate([wq[:, :, :MLA_NOPE].reshape(MLA_Q_LORA, -1),
                                  wq[:, :, MLA_NOPE:].reshape(MLA_Q_LORA, -1)], axis=1).astype(BF16)
        wkv = w_kv_b[l].reshape(MLA_KV_LORA, N_HEADS, MLA_NOPE + MLA_V)
        w_kvb_p = jnp.concatenate([wkv[:, :, :MLA_NOPE].reshape(MLA_KV_LORA, -1),
                                   wkv[:, :, MLA_NOPE:].reshape(MLA_KV_LORA, -1)], axis=1).astype(BF16)
        gains = {
            "mix": g_mix_norm[l], "q_lat": g_q_lat[l], "kv_lat": g_kv_lat[l],
            "q_nope": jnp.tile(g_mla_q_nope[l], N_HEADS) * mla_scale,
            "q_pe": jnp.tile(g_mla_q_pe[l], N_HEADS) * mla_scale,
            "k_nope": jnp.tile(g_mla_k_nope[l], N_HEADS),
            "k_pe": jnp.tile(g_mla_k_pe[l], LANES // MLA_ROPE),
            "dil_q": jnp.tile(g_dil_q[l], N_HEADS) * dil_scale,
            "dil_k": jnp.tile(g_dil_k[l], N_HEADS)}
        gains = jnp.stack([jnp.pad(gains[n], (0, D - w)) for n, w in GAIN_ROWS])
        consts = (gains, w_in_p, w_qb_p, w_kvb_p, seg64, seg32, freq, expand)
        qn, qpe, kn, kpe, vm, qd, kd, vd = _proj(x, pos3, mod3, consts)

        two = ATT_PAIRS * LANES
        o_mla = _attn_call(_mla_kernel, "mla", (qn, qpe, kn, kpe, vm), (two, LANES, two, None, two),
                           2 * LANES, _mla_bias(), B, S)
        o_dil = _attn_call(_dil_kernel, "dil", (qd, kd, vd), (two, two, two), LANES, _dil_bias(), B, S)

        x = _ffn(x, o_mla, o_dil, mod3, g_ffn_norm[l][None, :], w_o[l].astype(BF16),
                 w_up[l].astype(BF16), w_conv[l], b_conv[l][None, :], w_down[l].astype(BF16))
    return x
```

```python
import numpy as np
import jax
import jax.numpy as jnp
from jax import lax
from jax.experimental import pallas as pl
from jax.experimental.pallas import tpu as pltpu

F32 = jnp.float32
BF16 = jnp.bfloat16

D_MODEL = 1024
N_HEADS = 8
MLA_NOPE = 64
MLA_ROPE = 32
MLA_V = 64
MLA_Q_LORA = 512
MLA_KV_LORA = 256
DIL_HEAD_DIM = 64
DIL_WIDTH = N_HEADS * DIL_HEAD_DIM
DIL_PATTERNS = ((128, 1), (512, 4), (2048, 16))
D_FF = 2816
ROPE_THETA = 10000.0
EPS = 1e-6
MASK_BIAS = -1e30
LOG2E = 1.4426950408889634

LANES = 128
MXU_DIM = 256
VMEM_LIMIT = 56 * 1024 * 1024
ADA_TN = 1024
PROJ_TM = 1024
ATT_TQ = 256
ATT_TK = 256
VT_ROWS = 64 + 16
ATT_PAIRS = 2
FFN_TM = 512
FFN_TF = 256
N_FT = D_FF // FFN_TF
DOWN_TILES = 2
STAGE_AT = N_FT - 4

C_QLAT = 0
C_KVLAT = C_QLAT + MLA_Q_LORA
C_KPE = C_KVLAT + MLA_KV_LORA
C_QD = C_KPE + LANES
C_KD = C_QD + DIL_WIDTH
C_VD = C_KD + DIL_WIDTH


def _dot(a, b):
    return jnp.dot(a, b, preferred_element_type=F32)


def _const_spec(shape):
    nd = len(shape)
    return pl.BlockSpec(shape, lambda *_: (0,) * nd, pipeline_mode=pl.Buffered(1))


def _split_bf16(v):
    hi = v.astype(BF16)
    lo = (v - hi.astype(F32)).astype(BF16)
    return hi, lo


def _ada_kernel(c_ref, w_ref, b_ref, o_ref):
    c = c_ref[...]
    a = c * (1.0 / (1.0 + jnp.exp(-c)))
    n = a.shape[0]
    o2 = _dot(jnp.concatenate(_split_bf16(a), axis=0), w_ref[...].astype(BF16))
    o_ref[...] = o2[:n] + o2[n:] + b_ref[...]


def _ada(c, w_ada, b_ada):
    B = c.shape[0]
    n = w_ada.shape[1]
    return pl.pallas_call(
        _ada_kernel,
        out_shape=jax.ShapeDtypeStruct((B, n), F32),
        grid=(n // ADA_TN,),
        in_specs=[pl.BlockSpec((B, D_MODEL), lambda j: (0, 0)),
                  pl.BlockSpec((D_MODEL, ADA_TN), lambda j: (0, j)),
                  pl.BlockSpec((1, ADA_TN), lambda j: (0, j))],
        out_specs=pl.BlockSpec((B, ADA_TN), lambda j: (0, j)),
        compiler_params=pltpu.CompilerParams(dimension_semantics=("arbitrary",),
                                             vmem_limit_bytes=VMEM_LIMIT),
        name="ada",
    )(c, w_ada, b_ada)


def _rms_rows(v, width):
    ms = jnp.sum(v * v, axis=-1, keepdims=True) * (1.0 / width)
    return v * lax.rsqrt(ms + EPS)


def _seg_rms(v, seg_ref, seg_width):
    outs = []
    for c0 in range(0, v.shape[1], MXU_DIM):
        blk = v[:, c0:c0 + MXU_DIM]
        ss = _dot((blk * blk).astype(BF16), seg_ref[...])
        outs.append(blk * lax.rsqrt(ss * (1.0 / seg_width) + EPS))
    return outs[0] if len(outs) == 1 else jnp.concatenate(outs, axis=1)


def _rope(v, cos, sin_signed, half):
    lane = lax.broadcasted_iota(jnp.int32, (v.shape[0], LANES), 1)
    first = lane % (2 * half) < half
    outs = []
    for c0 in range(0, v.shape[1], LANES):
        blk = v[:, c0:c0 + LANES]
        partner = jnp.where(first, pltpu.roll(blk, LANES - half, 1), pltpu.roll(blk, half, 1))
        outs.append(blk * cos + partner * sin_signed)
    return outs[0] if len(outs) == 1 else jnp.concatenate(outs, axis=1)


GAIN_ROWS = (("mix", D_MODEL), ("q_lat", MLA_Q_LORA), ("kv_lat", MLA_KV_LORA),
             ("q_nope", N_HEADS * MLA_NOPE), ("q_pe", N_HEADS * MLA_ROPE), ("k_nope", N_HEADS * MLA_NOPE),
             ("k_pe", LANES), ("dil_q", DIL_WIDTH), ("dil_k", DIL_WIDTH))


class _Rows:
    def __init__(self, ref):
        self._ref = ref

    def __getattr__(self, name):
        idx, width = next((i, w) for i, (n, w) in enumerate(GAIN_ROWS) if n == name)
        return self._ref[idx:idx + 1, :width]


def _proj_kernel(x_ref, pos_ref, mod_ref, gains_ref, win_ref, wqb_ref, wkvb_ref,
                 seg64_ref, seg32_ref, freq_ref, expand_ref,
                 qn_ref, qpe_ref, kn_ref, kpe_ref, vm_ref, qd_ref, kd_ref, vd_ref):
    g = _Rows(gains_ref)
    hd_half, pe_half = DIL_HEAD_DIM // 2, MLA_ROPE // 2
    x = x_ref[0]
    sh1 = mod_ref[0, 0:1, :]
    sc1 = mod_ref[0, 1:2, :]
    h = (_rms_rows(x, D_MODEL) * (g.mix * (1.0 + sc1)) + sh1).astype(BF16)

    q_lat = _dot(h, win_ref[:, C_QLAT:C_QLAT + MLA_Q_LORA])
    kv_lat = _dot(h, win_ref[:, C_KVLAT:C_KVLAT + MLA_KV_LORA])
    qd_raw = _dot(h, win_ref[:, C_QD:C_QD + DIL_WIDTH])

    pos = pos_ref[0].astype(F32)
    ang = freq_ref[...] * pos
    cs = jnp.concatenate([jnp.cos(ang), jnp.sin(ang)], axis=0).T
    cs_hi, cs_lo = _split_bf16(cs)
    tabs = _dot(cs_hi, expand_ref[...]) + _dot(cs_lo, expand_ref[...])
    cos_d, sin_d = tabs[:, 0:LANES], tabs[:, LANES:2 * LANES]
    cos_p, sin_p = tabs[:, 2 * LANES:3 * LANES], tabs[:, 3 * LANES:4 * LANES]

    q_in = (_rms_rows(q_lat, MLA_Q_LORA) * g.q_lat).astype(BF16)
    q = _dot(q_in, wqb_ref[...])
    kv_in = (_rms_rows(kv_lat, MLA_KV_LORA) * g.kv_lat).astype(BF16)
    kv = _dot(kv_in, wkvb_ref[...])
    kd_raw = _dot(h, win_ref[:, C_KD:C_KD + DIL_WIDTH])
    qd_ref[0] = _rope(_seg_rms(qd_raw, seg64_ref, DIL_HEAD_DIM) * g.dil_q, cos_d, sin_d, hd_half).astype(BF16)
    k_pe = _dot(h, win_ref[:, C_KPE:C_KPE + LANES])
    vd_ref[0] = _dot(h, win_ref[:, C_VD:C_VD + DIL_WIDTH]).astype(BF16)

    n_nope = N_HEADS * MLA_NOPE
    qn_ref[0] = (_seg_rms(q[:, :n_nope], seg64_ref, MLA_NOPE) * g.q_nope).astype(BF16)
    q_pe = _seg_rms(q[:, n_nope:], seg32_ref, MLA_ROPE) * g.q_pe
    qpe_ref[0] = _rope(q_pe, cos_p, sin_p, pe_half).astype(BF16)
    kn_ref[0] = (_seg_rms(kv[:, :n_nope], seg64_ref, MLA_NOPE) * g.k_nope).astype(BF16)
    vm_ref[0] = kv[:, n_nope:].astype(BF16)
    kd_ref[0] = _rope(_seg_rms(kd_raw, seg64_ref, DIL_HEAD_DIM) * g.dil_k, cos_d, sin_d, hd_half).astype(BF16)
    kpe_ref[0] = _rope(_rms_rows(k_pe, LANES) * g.k_pe, cos_p, sin_p, pe_half).astype(BF16)


def _proj(x, pos3, mod3, consts):
    B, S, _ = x.shape
    tm = PROJ_TM
    row = lambda w: pl.BlockSpec((1, tm, w), lambda b, i: (b, i, 0))
    out_widths = (512, 256, 512, LANES, 512, 512, 512, 512)
    in_specs = [row(D_MODEL),
                pl.BlockSpec((1, 1, tm), lambda b, i: (b, 0, i)),
                pl.BlockSpec((1, 6, D_MODEL), lambda b, i: (b, 0, 0))]
    in_specs += [_const_spec(a.shape) for a in consts]
    return pl.pallas_call(
        _proj_kernel,
        out_shape=tuple(jax.ShapeDtypeStruct((B, S, w), BF16) for w in out_widths),
        grid=(B, S // tm),
        in_specs=in_specs,
        out_specs=tuple(row(w) for w in out_widths),
        compiler_params=pltpu.CompilerParams(dimension_semantics=("arbitrary", "arbitrary"),
                                             vmem_limit_bytes=VMEM_LIMIT),
        name="proj",
    )(x, pos3, mod3, *consts)


def _row_iota(shape):
    return lax.broadcasted_iota(jnp.int32, shape, 0)


def _sublane_allreduce(v, op):
    for shift in (4, 2, 1):
        v = op(v, pltpu.roll(v, shift, 0))
    return v


def _transpose_bf16(v):
    return v.astype(F32).T.astype(BF16)


class _Stream:
    def __init__(self, load_qt2, load_k, load_v, store_o):
        self.load_qt2, self.load_k, self.load_v, self.store_o = load_qt2, load_k, load_v, store_o


def _pair_attention(streams, n_rows, bias_ref, near_width, vt_scr, qt_scr, s_scr, m_scr, l_scr, acc_scr):
    tq, tk = ATT_TQ, ATT_TK
    assert tq == tk
    n_q = n_rows // tq
    near = bias_ref.shape[0]
    hd = LANES // 2
    ids = range(len(streams))

    def put_queries(i):
        for s in ids:
            qt_scr[s] = streams[s].load_qt2(pl.ds(i * tq, tq))

    def put_scores(i, j, slot):
        far = i - j >= near
        k_rows = pl.ds(j * tk, tk)
        for s in ids:
            qt = qt_scr[s] if far else qt_scr[s, :near_width, :]
            s_scr[s, slot] = _dot(streams[s].load_k(k_rows, far), qt)

    def consume(j, slot, bias_idx):
        for s in ids:
            st = s_scr[s, slot]
            if bias_idx is not None:
                st = st + bias_ref[bias_idx]
            s3 = st.reshape(tk // 8, 8, 2 * tq)
            m_new = _sublane_allreduce(jnp.max(s3, axis=0), jnp.maximum)
            if j > 0:
                m_old = m_scr[s]
                m_new = jnp.maximum(m_old, m_new)
                alpha = jnp.exp2(m_old - m_new)
            p3 = jnp.exp2(s3 - m_new[None])
            m_scr[s] = m_new
            pt = p3.reshape(tk, 2 * tq).astype(BF16)
            vt = vt_scr[s, j]
            l_new = []
            for h in range(2):
                pv = _dot(vt[h * VT_ROWS:(h + 1) * VT_ROWS, :], pt[:, h * tq:(h + 1) * tq])
                l_new.append(pv[hd:hd + 8, :])
                pv = pv[:hd, :]
                if j > 0:
                    a3 = alpha[:, h * tq:(h + 1) * tq][None]
                    pv = (acc_scr[s, h].reshape(hd // 8, 8, tq) * a3).reshape(hd, tq) + pv
                acc_scr[s, h] = pv
            l_new = jnp.concatenate(l_new, axis=1)
            l_scr[s] = l_new if j == 0 else alpha * l_scr[s] + l_new

    def step(i, j, slot):
        put_scores(i, j + 1, 1 - slot)
        consume(j, slot, i - j if i - j < near else None)

    ones = jnp.ones((VT_ROWS - hd, tk), BF16)
    for s in ids:
        for j in range(n_rows // tk):
            vt = _transpose_bf16(streams[s].load_v(pl.ds(j * tk, tk)))
            vt_scr[s, j] = jnp.concatenate([vt[:hd], ones, vt[hd:], ones], axis=0)
    put_queries(0)
    put_scores(0, 0, 0)

    t = 0
    for i in range(n_q):
        for j in range(i):
            step(i, j, t % 2)
            t += 1
        i_next = min(i + 1, n_q - 1)
        put_queries(i_next)
        put_scores(i_next, 0, 1 - t % 2)
        consume(i, t % 2, 0)
        t += 1

        for s in ids:
            inv_l = 1.0 / l_scr[s]
            ot = jnp.concatenate(
                [(acc_scr[s, h].reshape(hd // 8, 8, tq) * inv_l[:, h * tq:(h + 1) * tq][None]).reshape(hd, tq)
                 for h in range(2)], axis=0)
            streams[s].store_o(pl.ds(i * tq, tq), ot.T)


def _lane_block(ref, s):
    cols = slice(s * LANES, (s + 1) * LANES)
    return lambda rows: ref[0, rows, cols]


def _mla_kernel(qn_ref, qpe_ref, kn_ref, kpe_ref, v_ref, bias_ref, o_ref, *scratch):
    feat = _row_iota((2 * LANES, ATT_TQ))
    pe_head = (feat % LANES) // MLA_ROPE
    is_pe = feat >= LANES
    n_streams = o_ref.shape[2] // LANES
    assert n_streams == 2

    def make(s):
        mask_a = (feat < LANES // 2) | (is_pe & (pe_head == 2 * s))
        mask_b = ((feat >= LANES // 2) & (feat < LANES)) | (is_pe & (pe_head == 2 * s + 1))
        qn, kn = _lane_block(qn_ref, s), _lane_block(kn_ref, s)

        def load_qt2(rows):
            qt = jnp.concatenate([qn(rows), qpe_ref[0, rows, :]], axis=1).astype(F32).T
            z = jnp.zeros_like(qt)
            return jnp.concatenate([jnp.where(mask_a, qt, z), jnp.where(mask_b, qt, z)],
                                   axis=1).astype(BF16)

        def load_k(rows, far):
            return jnp.concatenate([kn(rows), kpe_ref[0, rows, :]], axis=1)

        def store_o(rows, val):
            o_ref[0, rows, s * LANES:(s + 1) * LANES] = val.astype(o_ref.dtype)

        return _Stream(load_qt2, load_k, _lane_block(v_ref, s), store_o)

    _pair_attention([make(s) for s in range(n_streams)], o_ref.shape[1], bias_ref, 2 * LANES, *scratch)


def _dil_kernel(q_ref, k_ref, v_ref, qclass_ref, kmask_ref, bias_ref, o_ref, *scratch):
    feat = _row_iota((LANES, ATT_TQ))
    is_a = feat < DIL_HEAD_DIM

    def make(s):
        q, k = _lane_block(q_ref, s), _lane_block(k_ref, s)

        def load_qt2(rows):
            qt = q(rows).astype(F32).T
            z = jnp.zeros_like(qt)
            feats = jnp.concatenate([jnp.where(is_a, qt, z), jnp.where(is_a, z, qt)], axis=1)
            return jnp.concatenate([feats.astype(BF16), qclass_ref[...]], axis=0)

        def load_k(rows, far):
            return jnp.concatenate([k(rows), kmask_ref[...]], axis=1) if far else k(rows)

        def store_o(rows, val):
            o_ref[0, rows, s * LANES:(s + 1) * LANES] = val.astype(o_ref.dtype)

        return _Stream(load_qt2, load_k, _lane_block(v_ref, s), store_o)

    _pair_attention([make(s) for s in range(o_ref.shape[2] // LANES)], o_ref.shape[1], bias_ref,
                    LANES, *scratch)


def _attn_call(kernel, name, arrays, widths, qk_width, consts, B, S):
    n_s = ATT_PAIRS
    col = lambda w: pl.BlockSpec((1, S, w), lambda b, g: (b, 0, g))
    shared = pl.BlockSpec((1, S, LANES), lambda b, g: (b, 0, 0))
    in_specs = [shared if w is None else col(w) for w in widths] + [_const_spec(c.shape) for c in consts]
    return pl.pallas_call(
        kernel,
        out_shape=jax.ShapeDtypeStruct((B, S, N_HEADS * MLA_V), BF16),
        grid=(B, N_HEADS // 2 // n_s),
        in_specs=in_specs,
        out_specs=col(n_s * LANES),
        scratch_shapes=[pltpu.VMEM((n_s, S // ATT_TK, 2 * VT_ROWS, ATT_TK), BF16),
                        pltpu.VMEM((n_s, qk_width, 2 * ATT_TQ), BF16),
                        pltpu.VMEM((n_s, 2, ATT_TK, 2 * ATT_TQ), F32),
                        pltpu.VMEM((n_s, 8, 2 * ATT_TQ), F32),
                        pltpu.VMEM((n_s, 8, 2 * ATT_TQ), F32),
                        pltpu.VMEM((n_s, 2, LANES // 2, ATT_TQ), F32)],
        compiler_params=pltpu.CompilerParams(dimension_semantics=("arbitrary", "arbitrary"),
                                             vmem_limit_bytes=VMEM_LIMIT),
        name=name,
    )(*arrays, *consts)


def _two_heads(tile_qk):
    return np.concatenate([tile_qk.T, tile_qk.T], axis=1)


def _mla_bias():
    d = np.arange(ATT_TQ)[:, None] - np.arange(ATT_TK)[None, :]
    tile = np.where(d >= 0, 0.0, MASK_BIAS).astype(np.float32)
    return jnp.asarray(_two_heads(tile)[None])


def _dil_constants():
    (far_window, far_dil), = [(w, d) for w, d in DIL_PATTERNS if w == max(p[0] for p in DIL_PATTERNS)]
    assert ATT_TK % far_dil == 0 and far_dil <= LANES
    max_near = max(w for w, d in DIL_PATTERNS if w < far_window)
    n_near = -(-max_near // ATT_TK) + 1
    tiles = []
    for blk in range(n_near):
        delta = blk * ATT_TK + np.arange(ATT_TQ)[:, None] - np.arange(ATT_TK)[None, :]
        mult = np.zeros(delta.shape, np.int64)
        for window, dil in DIL_PATTERNS:
            mult += (delta >= 0) & (delta % dil == 0) & (delta <= window)
        tile = np.where(mult > 0, np.log2(np.maximum(mult, 1)), MASK_BIAS).astype(np.float32)
        tiles.append(_two_heads(tile))
    feat = np.arange(LANES)
    q_class = (feat[:, None] == np.arange(2 * ATT_TQ)[None, :] % far_dil).astype(np.float32)
    k_mask = np.where(feat[None, :] >= far_dil, 0.0,
                      np.where(np.arange(ATT_TK)[:, None] % far_dil == feat[None, :], 0.0, MASK_BIAS))
    return (jnp.asarray(q_class, dtype=BF16), jnp.asarray(k_mask.astype(np.float32), dtype=BF16),
            jnp.asarray(np.stack(tiles)))


def _ffn_kernel(x0_ref, om0_ref, od0_ref, mod0_ref, xn_ref, omn_ref, odn_ref, modn_ref, mod_ref,
                gffn_ref, wo_ref, wup_ref, wconv_ref, bconv_ref, wdown_ref, out_ref,
                carry_scr, acc_scr, h2_scr, y_scr, a_scr, x1n_scr, h2n_scr):
    half = wo_ref.shape[0] // 2

    def stage_tile(x_ref, om_ref, od_ref, m_ref):
        g1, sh2, sc2 = m_ref[0, 2:3, :], m_ref[0, 3:4, :], m_ref[0, 4:5, :]
        mix = _dot(om_ref[0], wo_ref[:half, :]) + _dot(od_ref[0], wo_ref[half:, :])
        x1 = x_ref[0] + g1 * mix
        x1n_scr[...] = x1
        h2n_scr[...] = (_rms_rows(x1, D_MODEL) * (gffn_ref[...] * (1.0 + sc2)) + sh2).astype(BF16)

    @pl.when((pl.program_id(0) == 0) & (pl.program_id(1) == 0))
    def _():
        stage_tile(x0_ref, om0_ref, od0_ref, mod0_ref)

    g2 = mod_ref[0, 5:6, :]
    h2_scr[...] = h2n_scr[...]
    out_ref[0] = x1n_scr[...]

    @pl.when(pl.program_id(1) == 0)
    def _():
        carry_scr[...] = jnp.zeros(carry_scr.shape, F32)

    def up(f):
        for half_idx, t in enumerate((f, f + N_FT)):
            y = _dot(h2_scr[...], wup_ref[:, t * FFN_TF:(t + 1) * FFN_TF])
            y_scr[f % 2, half_idx, 0:8, :] = carry_scr[t]
            y_scr[f % 2, half_idx, 8:, :] = y
            carry_scr[t] = y[FFN_TM - 8:, :]

    def conv(f, half_idx):
        t = f + half_idx * N_FT
        yb = y_scr.at[f % 2, half_idx]
        cols = slice(t * FFN_TF, (t + 1) * FFN_TF)
        w = wconv_ref[:, cols]
        return (w[2:3, :] * yb[8:8 + FFN_TM, :] + w[1:2, :] * yb[7:7 + FFN_TM, :]
                + w[0:1, :] * yb[6:6 + FFN_TM, :] + bconv_ref[:, cols])

    def down(f0, f1):
        d = _dot(a_scr[(f0 // DOWN_TILES) % 2, :, :(f1 - f0) * FFN_TF],
                 wdown_ref[f0 * FFN_TF:f1 * FFN_TF, :])
        if f0 == 0:
            acc_scr[...] = d
        else:
            acc_scr[...] += d

    up(0)
    pending = None
    for f in range(N_FT):
        if f + 1 < N_FT:
            up(f + 1)
        if pending is not None:
            down(*pending)
            pending = None
        if f == STAGE_AT:
            stage_tile(xn_ref, omn_ref, odn_ref, modn_ref)
        gate = conv(f, 0)
        val = conv(f, 1)
        k = f % DOWN_TILES
        a_scr[(f // DOWN_TILES) % 2, :, k * FFN_TF:(k + 1) * FFN_TF] = (
            gate * (1.0 / (1.0 + jnp.exp(-gate))) * val).astype(BF16)
        if k == DOWN_TILES - 1 or f == N_FT - 1:
            pending = (f - k, f + 1)
    down(*pending)
    out_ref[0] = out_ref[0] + g2 * acc_scr[...]


def _ffn(x, o_mla, o_dil, mod3, g_ffn, w_o, w_up, w_conv, b_conv, w_down):
    B, S, _ = x.shape
    tm = FFN_TM
    nt = S // tm

    def nxt(b, i):
        t = jnp.minimum(b * nt + i + 1, B * nt - 1)
        return t // nt, t % nt

    first = lambda w: pl.BlockSpec((1, tm, w), lambda b, i: (0, 0, 0), pipeline_mode=pl.Buffered(1))
    following = lambda w: pl.BlockSpec((1, tm, w), lambda b, i: (*nxt(b, i), 0))
    mod_spec = lambda fn: pl.BlockSpec((1, 6, D_MODEL), fn)
    wm, wd = o_mla.shape[2], o_dil.shape[2]
    consts = (g_ffn, w_o, w_up, w_conv, b_conv, w_down)
    return pl.pallas_call(
        _ffn_kernel,
        out_shape=jax.ShapeDtypeStruct(x.shape, F32),
        grid=(B, nt),
        in_specs=[first(D_MODEL), first(wm), first(wd), mod_spec(lambda b, i: (0, 0, 0)),
                  following(D_MODEL), following(wm), following(wd),
                  mod_spec(lambda b, i: (nxt(b, i)[0], 0, 0)), mod_spec(lambda b, i: (b, 0, 0))]
                 + [_const_spec(a.shape) for a in consts],
        out_specs=pl.BlockSpec((1, tm, D_MODEL), lambda b, i: (b, i, 0)),
        scratch_shapes=[pltpu.VMEM((2 * N_FT, 8, FFN_TF), F32),
                        pltpu.VMEM((tm, D_MODEL), F32),
                        pltpu.VMEM((tm, D_MODEL), BF16),
                        pltpu.VMEM((2, 2, tm + 8, FFN_TF), F32),
                        pltpu.VMEM((2, tm, DOWN_TILES * FFN_TF), BF16),
                        pltpu.VMEM((tm, D_MODEL), F32),
                        pltpu.VMEM((tm, D_MODEL), BF16)],
        compiler_params=pltpu.CompilerParams(dimension_semantics=("arbitrary", "arbitrary"),
                                             vmem_limit_bytes=VMEM_LIMIT),
        name="ffn",
    )(x, o_mla, o_dil, mod3, x, o_mla, o_dil, mod3, mod3, *consts)


def _segment_matrices():
    i = np.arange(MXU_DIM)
    seg64 = i // 64
    seg32 = i // MLA_ROPE
    same = lambda s: jnp.asarray((s[:, None] == s[None, :]).astype(np.float32), dtype=BF16)
    return same(seg64), same(seg32)


def _rope_constants():
    def inv_freq(d):
        half = d // 2
        return jnp.power(ROPE_THETA, -2.0 * jnp.arange(half, dtype=F32) / d)
    fd, fp = inv_freq(DIL_HEAD_DIM), inv_freq(MLA_ROPE)
    nd, npe = fd.shape[0], fp.shape[0]
    freq = jnp.concatenate([fd, fp, jnp.zeros((64 - nd - npe,), F32)])[:, None]
    lane = np.arange(LANES)
    sign = lambda half: np.where(lane % (2 * half) < half, -1.0, 1.0)
    e = np.zeros((2 * 64, 4 * LANES), np.float32)
    e[lane % nd, lane] = 1.0
    e[64 + lane % nd, LANES + lane] = sign(nd)
    e[nd + lane % npe, 2 * LANES + lane] = 1.0
    e[64 + nd + lane % npe, 3 * LANES + lane] = sign(npe)
    return freq, jnp.asarray(e, dtype=BF16)


def kernel(x, c, positions, w_ada, b_ada, g_mix_norm, w_in, g_q_lat, w_q_b, g_kv_lat, w_kv_b,
           g_mla_q_nope, g_mla_q_pe, g_mla_k_nope, g_mla_k_pe, g_dil_q, g_dil_k, w_o,
           g_ffn_norm, w_up, w_conv, b_conv, w_down):
    B, S, D = x.shape
    assert D == D_MODEL and S % PROJ_TM == 0 and S % ATT_TQ == 0 and S % FFN_TM == 0
    assert S <= max(w for w, _ in DIL_PATTERNS)
    seg64, seg32 = _segment_matrices()
    freq, expand = _rope_constants()
    mla_scale = (MLA_NOPE + MLA_ROPE) ** -0.5 * LOG2E
    dil_scale = DIL_HEAD_DIM ** -0.5 * LOG2E
    pos3 = positions.reshape(B, 1, S)

    for l in range(w_ada.shape[0]):
        mod3 = _ada(c, w_ada[l], b_ada[l][None, :]).reshape(B, 6, D)

        wi = w_in[l]
        c_kpe = MLA_Q_LORA + MLA_KV_LORA
        c_qd = c_kpe + MLA_ROPE
        w_in_p = jnp.concatenate([wi[:, :c_kpe], jnp.tile(wi[:, c_kpe:c_qd], (1, LANES // MLA_ROPE)),
                                  wi[:, c_qd:]], axis=1).astype(BF16)
        wq = w_q_b[l].reshape(MLA_Q_LORA, N_HEADS, MLA_NOPE + MLA_ROPE)
        w_qb_p = jnp.concatenate([wq[:, :, :MLA_NOPE].reshape(MLA_Q_LORA, -1),
                                  wq[:, :, MLA_NOPE:].reshape(MLA_Q_LORA, -1)], axis=1).astype(BF16)
        wkv = w_kv_b[l].reshape(MLA_KV_LORA, N_HEADS, MLA_NOPE + MLA_V)
        w_kvb_p = jnp.concatenate([wkv[:, :, :MLA_NOPE].reshape(MLA_KV_LORA, -1),
                                   wkv[:, :, MLA_NOPE:].reshape(MLA_KV_LORA, -1)], axis=1).astype(BF16)
        gains = {
            "mix": g_mix_norm[l], "q_lat": g_q_lat[l], "kv_lat": g_kv_lat[l],
            "q_nope": jnp.tile(g_mla_q_nope[l], N_HEADS) * mla_scale,
            "q_pe": jnp.tile(g_mla_q_pe[l], N_HEADS) * mla_scale,
            "k_nope": jnp.tile(g_mla_k_nope[l], N_HEADS),
            "k_pe": jnp.tile(g_mla_k_pe[l], LANES // MLA_ROPE),
            "dil_q": jnp.tile(g_dil_q[l], N_HEADS) * dil_scale,
            "dil_k": jnp.tile(g_dil_k[l], N_HEADS)}
        gains = jnp.stack([jnp.pad(gains[n], (0, D - w)) for n, w in GAIN_ROWS])
        consts = (gains, w_in_p, w_qb_p, w_kvb_p, seg64, seg32, freq, expand)
        qn, qpe, kn, kpe, vm, qd, kd, vd = _proj(x, pos3, mod3, consts)

        two = ATT_PAIRS * LANES
        o_mla = _attn_call(_mla_kernel, "mla", (qn, qpe, kn, kpe, vm), (two, LANES, two, None, two),
                           2 * LANES, (_mla_bias(),), B, S)
        o_dil = _attn_call(_dil_kernel, "dil", (qd, kd, vd), (two, two, two), 2 * LANES,
                           _dil_constants(), B, S)

        x = _ffn(x, o_mla, o_dil, mod3, g_ffn_norm[l][None, :], w_o[l].astype(BF16),
                 w_up[l].astype(BF16), w_conv[l], b_conv[l][None, :], w_down[l].astype(BF16))
    return x
```

```python
import numpy as np
import jax
import jax.numpy as jnp
from jax import lax
from jax.experimental import pallas as pl
from jax.experimental.pallas import tpu as pltpu

F32 = jnp.float32
BF16 = jnp.bfloat16

D_MODEL = 1024
N_HEADS = 8
MLA_NOPE = 64
MLA_ROPE = 32
MLA_V = 64
MLA_Q_LORA = 512
MLA_KV_LORA = 256
DIL_HEAD_DIM = 64
DIL_WIDTH = N_HEADS * DIL_HEAD_DIM
DIL_PATTERNS = ((128, 1), (512, 4), (2048, 16))
D_FF = 2816
ROPE_THETA = 10000.0
EPS = 1e-6
MASK_BIAS = -1e30
LOG2E = 1.4426950408889634

LANES = 128
MXU_DIM = 256
VMEM_LIMIT = 56 * 1024 * 1024
ADA_TN = 1024
PROJ_TM = 512
ATT_TQ = 256
ATT_TK = 256
VT_ROWS = 64 + 16
ATT_PAIRS = 2
FFN_TM = 512
FFN_TF = 256
N_FT = D_FF // FFN_TF
DOWN_TILES = 2
STAGE_AT = N_FT - 4

C_QLAT = 0
C_KVLAT = C_QLAT + MLA_Q_LORA
C_KPE = C_KVLAT + MLA_KV_LORA
C_QD = C_KPE + LANES
C_KD = C_QD + DIL_WIDTH
C_VD = C_KD + DIL_WIDTH


def _dot(a, b):
    return jnp.dot(a, b, preferred_element_type=F32)


def _const_spec(shape):
    nd = len(shape)
    return pl.BlockSpec(shape, lambda *_: (0,) * nd, pipeline_mode=pl.Buffered(1))


def _split_bf16(v):
    hi = v.astype(BF16)
    lo = (v - hi.astype(F32)).astype(BF16)
    return hi, lo


def _ada_kernel(c_ref, w_ref, b_ref, o_ref):
    c = c_ref[...]
    a = c * (1.0 / (1.0 + jnp.exp(-c)))
    n = a.shape[0]
    o2 = _dot(jnp.concatenate(_split_bf16(a), axis=0), w_ref[...].astype(BF16))
    o_ref[...] = o2[:n] + o2[n:] + b_ref[...]


def _ada(c, w_ada, b_ada):
    B = c.shape[0]
    n = w_ada.shape[1]
    return pl.pallas_call(
        _ada_kernel,
        out_shape=jax.ShapeDtypeStruct((B, n), F32),
        grid=(n // ADA_TN,),
        in_specs=[pl.BlockSpec((B, D_MODEL), lambda j: (0, 0)),
                  pl.BlockSpec((D_MODEL, ADA_TN), lambda j: (0, j)),
                  pl.BlockSpec((1, ADA_TN), lambda j: (0, j))],
        out_specs=pl.BlockSpec((B, ADA_TN), lambda j: (0, j)),
        compiler_params=pltpu.CompilerParams(dimension_semantics=("arbitrary",),
                                             vmem_limit_bytes=VMEM_LIMIT),
        name="ada",
    )(c, w_ada, b_ada)


def _rms_rows(v, width):
    ms = jnp.sum(v * v, axis=-1, keepdims=True) * (1.0 / width)
    return v * lax.rsqrt(ms + EPS)


def _seg_rms(v, seg_ref, seg_width):
    outs = []
    for c0 in range(0, v.shape[1], MXU_DIM):
        blk = v[:, c0:c0 + MXU_DIM]
        ss = _dot((blk * blk).astype(BF16), seg_ref[...])
        outs.append(blk * lax.rsqrt(ss * (1.0 / seg_width) + EPS))
    return outs[0] if len(outs) == 1 else jnp.concatenate(outs, axis=1)


def _rope(v, cos, sin_signed, half):
    lane = lax.broadcasted_iota(jnp.int32, (v.shape[0], LANES), 1)
    first = lane % (2 * half) < half
    outs = []
    for c0 in range(0, v.shape[1], LANES):
        blk = v[:, c0:c0 + LANES]
        partner = jnp.where(first, pltpu.roll(blk, LANES - half, 1), pltpu.roll(blk, half, 1))
        outs.append(blk * cos + partner * sin_signed)
    return outs[0] if len(outs) == 1 else jnp.concatenate(outs, axis=1)


GAIN_ROWS = (("mix", D_MODEL), ("q_lat", MLA_Q_LORA), ("kv_lat", MLA_KV_LORA),
             ("q_nope", N_HEADS * MLA_NOPE), ("q_pe", N_HEADS * MLA_ROPE), ("k_nope", N_HEADS * MLA_NOPE),
             ("k_pe", LANES), ("dil_q", DIL_WIDTH), ("dil_k", DIL_WIDTH))


class _Rows:
    def __init__(self, ref):
        self._ref = ref

    def __getattr__(self, name):
        idx, width = next((i, w) for i, (n, w) in enumerate(GAIN_ROWS) if n == name)
        return self._ref[idx:idx + 1, :width]


def _proj_kernel(x_ref, pos_ref, mod_ref, gains_ref, win_ref, wqb_ref, wkvb_ref,
                 seg64_ref, seg32_ref, freq_ref, expand_ref,
                 qn_ref, qpe_ref, kn_ref, kpe_ref, vm_ref, qd_ref, kd_ref, vd_ref):
    g = _Rows(gains_ref)
    hd_half, pe_half = DIL_HEAD_DIM // 2, MLA_ROPE // 2
    x = x_ref[0]
    sh1 = mod_ref[0, 0:1, :]
    sc1 = mod_ref[0, 1:2, :]
    h = (_rms_rows(x, D_MODEL) * (g.mix * (1.0 + sc1)) + sh1).astype(BF16)

    q_lat = _dot(h, win_ref[:, C_QLAT:C_QLAT + MLA_Q_LORA])
    kv_lat = _dot(h, win_ref[:, C_KVLAT:C_KVLAT + MLA_KV_LORA])
    qd_raw = _dot(h, win_ref[:, C_QD:C_QD + DIL_WIDTH])

    pos = pos_ref[0].astype(F32)
    ang = freq_ref[...] * pos
    cs = jnp.concatenate([jnp.cos(ang), jnp.sin(ang)], axis=0).T
    cs_hi, cs_lo = _split_bf16(cs)
    tabs = _dot(cs_hi, expand_ref[...]) + _dot(cs_lo, expand_ref[...])
    cos_d, sin_d = tabs[:, 0:LANES], tabs[:, LANES:2 * LANES]
    cos_p, sin_p = tabs[:, 2 * LANES:3 * LANES], tabs[:, 3 * LANES:4 * LANES]

    q_in = (_rms_rows(q_lat, MLA_Q_LORA) * g.q_lat).astype(BF16)
    q = _dot(q_in, wqb_ref[...])
    kv_in = (_rms_rows(kv_lat, MLA_KV_LORA) * g.kv_lat).astype(BF16)
    kv = _dot(kv_in, wkvb_ref[...])
    kd_raw = _dot(h, win_ref[:, C_KD:C_KD + DIL_WIDTH])
    qd_ref[0] = _rope(_seg_rms(qd_raw, seg64_ref, DIL_HEAD_DIM) * g.dil_q, cos_d, sin_d, hd_half).astype(BF16)
    k_pe = _dot(h, win_ref[:, C_KPE:C_KPE + LANES])
    vd_ref[0] = _dot(h, win_ref[:, C_VD:C_VD + DIL_WIDTH]).astype(BF16)

    n_nope = N_HEADS * MLA_NOPE
    qn_ref[0] = (_seg_rms(q[:, :n_nope], seg64_ref, MLA_NOPE) * g.q_nope).astype(BF16)
    q_pe = _seg_rms(q[:, n_nope:], seg32_ref, MLA_ROPE) * g.q_pe
    qpe_ref[0] = _rope(q_pe, cos_p, sin_p, pe_half).astype(BF16)
    kn_ref[0] = (_seg_rms(kv[:, :n_nope], seg64_ref, MLA_NOPE) * g.k_nope).astype(BF16)
    vm_ref[0] = kv[:, n_nope:].astype(BF16)
    kd_ref[0] = _rope(_seg_rms(kd_raw, seg64_ref, DIL_HEAD_DIM) * g.dil_k, cos_d, sin_d, hd_half).astype(BF16)
    kpe_ref[0] = _rope(_rms_rows(k_pe, LANES) * g.k_pe, cos_p, sin_p, pe_half).astype(BF16)


def _proj(x, pos3, mod3, consts):
    B, S, _ = x.shape
    tm = PROJ_TM
    row = lambda w: pl.BlockSpec((1, tm, w), lambda b, i: (b, i, 0))
    out_widths = (512, 256, 512, LANES, 512, 512, 512, 512)
    in_specs = [row(D_MODEL),
                pl.BlockSpec((1, 1, tm), lambda b, i: (b, 0, i)),
                pl.BlockSpec((1, 6, D_MODEL), lambda b, i: (b, 0, 0))]
    in_specs += [_const_spec(a.shape) for a in consts]
    return pl.pallas_call(
        _proj_kernel,
        out_shape=tuple(jax.ShapeDtypeStruct((B, S, w), BF16) for w in out_widths),
        grid=(B, S // tm),
        in_specs=in_specs,
        out_specs=tuple(row(w) for w in out_widths),
        compiler_params=pltpu.CompilerParams(dimension_semantics=("arbitrary", "arbitrary"),
                                             vmem_limit_bytes=VMEM_LIMIT),
        name="proj",
    )(x, pos3, mod3, *consts)


def _row_iota(shape):
    return lax.broadcasted_iota(jnp.int32, shape, 0)


def _sublane_allreduce(v, op):
    for shift in (4, 2, 1):
        v = op(v, pltpu.roll(v, shift, 0))
    return v


def _transpose_bf16(v):
    return v.astype(F32).T.astype(BF16)


class _Stream:
    def __init__(self, load_qt2, load_k, load_v, store_o):
        self.load_qt2, self.load_k, self.load_v, self.store_o = load_qt2, load_k, load_v, store_o


def _pair_attention(streams, n_rows, bias_ref, near_width, vt_scr, qt_scr, s_scr, m_scr, l_scr, acc_scr):
    tq, tk = ATT_TQ, ATT_TK
    assert tq == tk
    n_q = n_rows // tq
    near = bias_ref.shape[0]
    hd = LANES // 2
    ids = range(len(streams))

    def put_queries(i):
        for s in ids:
            qt_scr[s] = streams[s].load_qt2(pl.ds(i * tq, tq))

    def put_scores(i, j, slot):
        far = i - j >= near
        k_rows = pl.ds(j * tk, tk)
        for s in ids:
            qt = qt_scr[s] if far else qt_scr[s, :near_width, :]
            s_scr[s, slot] = _dot(streams[s].load_k(k_rows, far), qt)

    def consume(j, slot, bias_idx):
        for s in ids:
            st = s_scr[s, slot]
            if bias_idx is not None:
                st = st + bias_ref[bias_idx]
            s3 = st.reshape(tk // 8, 8, 2 * tq)
            m_new = _sublane_allreduce(jnp.max(s3, axis=0), jnp.maximum)
            if j > 0:
                m_old = m_scr[s]
                m_new = jnp.maximum(m_old, m_new)
                alpha = jnp.exp2(m_old - m_new)
            p3 = jnp.exp2(s3 - m_new[None])
            m_scr[s] = m_new
            pt = p3.reshape(tk, 2 * tq).astype(BF16)
            vt = vt_scr[s, j]
            l_new = []
            for h in range(2):
                pv = _dot(vt[h * VT_ROWS:(h + 1) * VT_ROWS, :], pt[:, h * tq:(h + 1) * tq])
                l_new.append(pv[hd:hd + 8, :])
                pv = pv[:hd, :]
                if j > 0:
                    a3 = alpha[:, h * tq:(h + 1) * tq][None]
                    pv = (acc_scr[s, h].reshape(hd // 8, 8, tq) * a3).reshape(hd, tq) + pv
                acc_scr[s, h] = pv
            l_new = jnp.concatenate(l_new, axis=1)
            l_scr[s] = l_new if j == 0 else alpha * l_scr[s] + l_new

    def step(i, j, slot):
        put_scores(i, j + 1, 1 - slot)
        consume(j, slot, i - j if i - j < near else None)

    ones = jnp.ones((VT_ROWS - hd, tk), BF16)
    for s in ids:
        for j in range(n_rows // tk):
            vt = _transpose_bf16(streams[s].load_v(pl.ds(j * tk, tk)))
            vt_scr[s, j] = jnp.concatenate([vt[:hd], ones, vt[hd:], ones], axis=0)
    put_queries(0)
    put_scores(0, 0, 0)

    t = 0
    for i in range(n_q):
        for j in range(i):
            step(i, j, t % 2)
            t += 1
        i_next = min(i + 1, n_q - 1)
        put_queries(i_next)
        put_scores(i_next, 0, 1 - t % 2)
        consume(i, t % 2, 0)
        t += 1

        for s in ids:
            inv_l = 1.0 / l_scr[s]
            ot = jnp.concatenate(
                [(acc_scr[s, h].reshape(hd // 8, 8, tq) * inv_l[:, h * tq:(h + 1) * tq][None]).reshape(hd, tq)
                 for h in range(2)], axis=0)
            streams[s].store_o(pl.ds(i * tq, tq), ot.T)


def _lane_block(ref, s):
    cols = slice(s * LANES, (s + 1) * LANES)
    return lambda rows: ref[0, rows, cols]


def _mla_kernel(qn_ref, qpe_ref, kn_ref, kpe_ref, v_ref, bias_ref, o_ref, *scratch):
    feat = _row_iota((2 * LANES, ATT_TQ))
    pe_head = (feat % LANES) // MLA_ROPE
    is_pe = feat >= LANES
    n_streams = o_ref.shape[2] // LANES
    assert n_streams == 2

    def make(s):
        mask_a = (feat < LANES // 2) | (is_pe & (pe_head == 2 * s))
        mask_b = ((feat >= LANES // 2) & (feat < LANES)) | (is_pe & (pe_head == 2 * s + 1))
        qn, kn = _lane_block(qn_ref, s), _lane_block(kn_ref, s)

        def load_qt2(rows):
            qt = jnp.concatenate([qn(rows), qpe_ref[0, rows, :]], axis=1).astype(F32).T
            z = jnp.zeros_like(qt)
            return jnp.concatenate([jnp.where(mask_a, qt, z), jnp.where(mask_b, qt, z)],
                                   axis=1).astype(BF16)

        def load_k(rows, far):
            return jnp.concatenate([kn(rows), kpe_ref[0, rows, :]], axis=1)

        def store_o(rows, val):
            o_ref[0, rows, s * LANES:(s + 1) * LANES] = val.astype(o_ref.dtype)

        return _Stream(load_qt2, load_k, _lane_block(v_ref, s), store_o)

    _pair_attention([make(s) for s in range(n_streams)], o_ref.shape[1], bias_ref, 2 * LANES, *scratch)


def _dil_kernel(q_ref, k_ref, v_ref, qclass_ref, kmask_ref, bias_ref, o_ref, *scratch):
    feat = _row_iota((LANES, ATT_TQ))
    is_a = feat < DIL_HEAD_DIM

    def make(s):
        q, k = _lane_block(q_ref, s), _lane_block(k_ref, s)

        def load_qt2(rows):
            qt = q(rows).astype(F32).T
            z = jnp.zeros_like(qt)
            feats = jnp.concatenate([jnp.where(is_a, qt, z), jnp.where(is_a, z, qt)], axis=1)
            return jnp.concatenate([feats.astype(BF16), qclass_ref[...]], axis=0)

        def load_k(rows, far):
            return jnp.concatenate([k(rows), kmask_ref[...]], axis=1) if far else k(rows)

        def store_o(rows, val):
            o_ref[0, rows, s * LANES:(s + 1) * LANES] = val.astype(o_ref.dtype)

        return _Stream(load_qt2, load_k, _lane_block(v_ref, s), store_o)

    _pair_attention([make(s) for s in range(o_ref.shape[2] // LANES)], o_ref.shape[1], bias_ref,
                    LANES, *scratch)


def _attn_call(kernel, name, arrays, widths, qk_width, consts, B, S):
    n_s = ATT_PAIRS
    col = lambda w: pl.BlockSpec((1, S, w), lambda b, g: (b, 0, g))
    shared = pl.BlockSpec((1, S, LANES), lambda b, g: (b, 0, 0))
    in_specs = [shared if w is None else col(w) for w in widths] + [_const_spec(c.shape) for c in consts]
    return pl.pallas_call(
        kernel,
        out_shape=jax.ShapeDtypeStruct((B, S, N_HEADS * MLA_V), BF16),
        grid=(B, N_HEADS // 2 // n_s),
        in_specs=in_specs,
        out_specs=col(n_s * LANES),
        scratch_shapes=[pltpu.VMEM((n_s, S // ATT_TK, 2 * VT_ROWS, ATT_TK), BF16),
                        pltpu.VMEM((n_s, qk_width, 2 * ATT_TQ), BF16),
                        pltpu.VMEM((n_s, 2, ATT_TK, 2 * ATT_TQ), F32),
                        pltpu.VMEM((n_s, 8, 2 * ATT_TQ), F32),
                        pltpu.VMEM((n_s, 8, 2 * ATT_TQ), F32),
                        pltpu.VMEM((n_s, 2, LANES // 2, ATT_TQ), F32)],
        compiler_params=pltpu.CompilerParams(dimension_semantics=("arbitrary", "arbitrary"),
                                             vmem_limit_bytes=VMEM_LIMIT),
        name=name,
    )(*arrays, *consts)


def _two_heads(tile_qk):
    return np.concatenate([tile_qk.T, tile_qk.T], axis=1)


def _mla_bias():
    d = np.arange(ATT_TQ)[:, None] - np.arange(ATT_TK)[None, :]
    tile = np.where(d >= 0, 0.0, MASK_BIAS).astype(np.float32)
    return jnp.asarray(_two_heads(tile)[None])


def _dil_constants():
    (far_window, far_dil), = [(w, d) for w, d in DIL_PATTERNS if w == max(p[0] for p in DIL_PATTERNS)]
    assert ATT_TK % far_dil == 0 and far_dil <= LANES
    max_near = max(w for w, d in DIL_PATTERNS if w < far_window)
    n_near = -(-max_near // ATT_TK) + 1
    tiles = []
    for blk in range(n_near):
        delta = blk * ATT_TK + np.arange(ATT_TQ)[:, None] - np.arange(ATT_TK)[None, :]
        mult = np.zeros(delta.shape, np.int64)
        for window, dil in DIL_PATTERNS:
            mult += (delta >= 0) & (delta % dil == 0) & (delta <= window)
        tile = np.where(mult > 0, np.log2(np.maximum(mult, 1)), MASK_BIAS).astype(np.float32)
        tiles.append(_two_heads(tile))
    feat = np.arange(LANES)
    q_class = (feat[:, None] == np.arange(2 * ATT_TQ)[None, :] % far_dil).astype(np.float32)
    k_mask = np.where(feat[None, :] >= far_dil, 0.0,
                      np.where(np.arange(ATT_TK)[:, None] % far_dil == feat[None, :], 0.0, MASK_BIAS))
    return (jnp.asarray(q_class, dtype=BF16), jnp.asarray(k_mask.astype(np.float32), dtype=BF16),
            jnp.asarray(np.stack(tiles)))


def _ffn_kernel(x0_ref, om0_ref, od0_ref, mod0_ref, xn_ref, omn_ref, odn_ref, modn_ref, mod_ref,
                gffn_ref, wo_ref, wup_ref, wconv_ref, bconv_ref, wdown_ref, out_ref,
                carry_scr, acc_scr, h2_scr, y_scr, a_scr, x1n_scr, h2n_scr):
    half = wo_ref.shape[0] // 2

    def stage_tile(x_ref, om_ref, od_ref, m_ref):
        g1, sh2, sc2 = m_ref[0, 2:3, :], m_ref[0, 3:4, :], m_ref[0, 4:5, :]
        mix = _dot(om_ref[0], wo_ref[:half, :]) + _dot(od_ref[0], wo_ref[half:, :])
        x1 = x_ref[0] + g1 * mix
        x1n_scr[...] = x1
        h2n_scr[...] = (_rms_rows(x1, D_MODEL) * (gffn_ref[...] * (1.0 + sc2)) + sh2).astype(BF16)

    @pl.when((pl.program_id(0) == 0) & (pl.program_id(1) == 0))
    def _():
        stage_tile(x0_ref, om0_ref, od0_ref, mod0_ref)

    g2 = mod_ref[0, 5:6, :]
    h2_scr[...] = h2n_scr[...]
    out_ref[0] = x1n_scr[...]

    @pl.when(pl.program_id(1) == 0)
    def _():
        carry_scr[...] = jnp.zeros(carry_scr.shape, F32)

    def up(f):
        for half_idx, t in enumerate((f, f + N_FT)):
            y = _dot(h2_scr[...], wup_ref[:, t * FFN_TF:(t + 1) * FFN_TF])
            y_scr[f % 2, half_idx, 0:8, :] = carry_scr[t]
            y_scr[f % 2, half_idx, 8:, :] = y
            carry_scr[t] = y[FFN_TM - 8:, :]

    def conv(f, half_idx):
        t = f + half_idx * N_FT
        yb = y_scr.at[f % 2, half_idx]
        cols = slice(t * FFN_TF, (t + 1) * FFN_TF)
        w = wconv_ref[:, cols]
        return (w[2:3, :] * yb[8:8 + FFN_TM, :] + w[1:2, :] * yb[7:7 + FFN_TM, :]
                + w[0:1, :] * yb[6:6 + FFN_TM, :] + bconv_ref[:, cols])

    def down(f0, f1):
        d = _dot(a_scr[(f0 // DOWN_TILES) % 2, :, :(f1 - f0) * FFN_TF],
                 wdown_ref[f0 * FFN_TF:f1 * FFN_TF, :])
        if f0 == 0:
            acc_scr[...] = d
        else:
            acc_scr[...] += d

    up(0)
    pending = None
    for f in range(N_FT):
        if f + 1 < N_FT:
            up(f + 1)
        if pending is not None:
            down(*pending)
            pending = None
        if f == STAGE_AT:
            stage_tile(xn_ref, omn_ref, odn_ref, modn_ref)
        gate = conv(f, 0)
        val = conv(f, 1)
        k = f % DOWN_TILES
        a_scr[(f // DOWN_TILES) % 2, :, k * FFN_TF:(k + 1) * FFN_TF] = (
            gate * (1.0 / (1.0 + jnp.exp(-gate))) * val).astype(BF16)
        if k == DOWN_TILES - 1 or f == N_FT - 1:
            pending = (f - k, f + 1)
    down(*pending)
    out_ref[0] = out_ref[0] + g2 * acc_scr[...]


def _ffn(x, o_mla, o_dil, mod3, g_ffn, w_o, w_up, w_conv, b_conv, w_down):
    B, S, _ = x.shape
    tm = FFN_TM
    nt = S // tm

    def nxt(b, i):
        t = jnp.minimum(b * nt + i + 1, B * nt - 1)
        return t // nt, t % nt

    first = lambda w: pl.BlockSpec((1, tm, w), lambda b, i: (0, 0, 0), pipeline_mode=pl.Buffered(1))
    following = lambda w: pl.BlockSpec((1, tm, w), lambda b, i: (*nxt(b, i), 0))
    mod_spec = lambda fn: pl.BlockSpec((1, 6, D_MODEL), fn)
    wm, wd = o_mla.shape[2], o_dil.shape[2]
    consts = (g_ffn, w_o, w_up, w_conv, b_conv, w_down)
    return pl.pallas_call(
        _ffn_kernel,
        out_shape=jax.ShapeDtypeStruct(x.shape, F32),
        grid=(B, nt),
        in_specs=[first(D_MODEL), first(wm), first(wd), mod_spec(lambda b, i: (0, 0, 0)),
                  following(D_MODEL), following(wm), following(wd),
                  mod_spec(lambda b, i: (nxt(b, i)[0], 0, 0)), mod_spec(lambda b, i: (b, 0, 0))]
                 + [_const_spec(a.shape) for a in consts],
        out_specs=pl.BlockSpec((1, tm, D_MODEL), lambda b, i: (b, i, 0)),
        scratch_shapes=[pltpu.VMEM((2 * N_FT, 8, FFN_TF), F32),
                        pltpu.VMEM((tm, D_MODEL), F32),
                        pltpu.VMEM((tm, D_MODEL), BF16),
                        pltpu.VMEM((2, 2, tm + 8, FFN_TF), F32),
                        pltpu.VMEM((2, tm, DOWN_TILES * FFN_TF), BF16),
                        pltpu.VMEM((tm, D_MODEL), F32),
                        pltpu.VMEM((tm, D_MODEL), BF16)],
        compiler_params=pltpu.CompilerParams(dimension_semantics=("arbitrary", "arbitrary"),
                                             vmem_limit_bytes=VMEM_LIMIT),
        name="ffn",
    )(x, o_mla, o_dil, mod3, x, o_mla, o_dil, mod3, mod3, *consts)


def _segment_matrices():
    i = np.arange(MXU_DIM)
    seg64 = i // 64
    seg32 = i // MLA_ROPE
    same = lambda s: jnp.asarray((s[:, None] == s[None, :]).astype(np.float32), dtype=BF16)
    return same(seg64), same(seg32)


def _rope_constants():
    def inv_freq(d):
        half = d // 2
        return jnp.power(ROPE_THETA, -2.0 * jnp.arange(half, dtype=F32) / d)
    fd, fp = inv_freq(DIL_HEAD_DIM), inv_freq(MLA_ROPE)
    nd, npe = fd.shape[0], fp.shape[0]
    freq = jnp.concatenate([fd, fp, jnp.zeros((64 - nd - npe,), F32)])[:, None]
    lane = np.arange(LANES)
    sign = lambda half: np.where(lane % (2 * half) < half, -1.0, 1.0)
    e = np.zeros((2 * 64, 4 * LANES), np.float32)
    e[lane % nd, lane] = 1.0
    e[64 + lane % nd, LANES + lane] = sign(nd)
    e[nd + lane % npe, 2 * LANES + lane] = 1.0
    e[64 + nd + lane % npe, 3 * LANES + lane] = sign(npe)
    return freq, jnp.asarray(e, dtype=BF16)


def kernel(x, c, positions, w_ada, b_ada, g_mix_norm, w_in, g_q_lat, w_q_b, g_kv_lat, w_kv_b,
           g_mla_q_nope, g_mla_q_pe, g_mla_k_nope, g_mla_k_pe, g_dil_q, g_dil_k, w_o,
           g_ffn_norm, w_up, w_conv, b_conv, w_down):
    B, S, D = x.shape
    assert D == D_MODEL and S % PROJ_TM == 0 and S % ATT_TQ == 0 and S % FFN_TM == 0
    assert S <= max(w for w, _ in DIL_PATTERNS)
    seg64, seg32 = _segment_matrices()
    freq, expand = _rope_constants()
    mla_scale = (MLA_NOPE + MLA_ROPE) ** -0.5 * LOG2E
    dil_scale = DIL_HEAD_DIM ** -0.5 * LOG2E
    pos3 = positions.reshape(B, 1, S)

    for l in range(w_ada.shape[0]):
        mod3 = _ada(c, w_ada[l], b_ada[l][None, :]).reshape(B, 6, D)

        wi = w_in[l]
        c_kpe = MLA_Q_LORA + MLA_KV_LORA
        c_qd = c_kpe + MLA_ROPE
        w_in_p = jnp.concatenate([wi[:, :c_kpe], jnp.tile(wi[:, c_kpe:c_qd], (1, LANES // MLA_ROPE)),
                                  wi[:, c_qd:]], axis=1).astype(BF16)
        wq = w_q_b[l].reshape(MLA_Q_LORA, N_HEADS, MLA_NOPE + MLA_ROPE)
        w_qb_p = jnp.concatenate([wq[:, :, :MLA_NOPE].reshape(MLA_Q_LORA, -1),
                                  wq[:, :, MLA_NOPE:].reshape(MLA_Q_LORA, -1)], axis=1).astype(BF16)
        wkv = w_kv_b[l].reshape(MLA_KV_LORA, N_HEADS, MLA_NOPE + MLA_V)
        w_kvb_p = jnp.concatenate([wkv[:, :, :MLA_NOPE].reshape(MLA_KV_LORA, -1),
                                   wkv[:, :, MLA_NOPE:].reshape(MLA_KV_LORA, -1)], axis=1).astype(BF16)
        gains = {
            "mix": g_mix_norm[l], "q_lat": g_q_lat[l], "kv_lat": g_kv_lat[l],
            "q_nope": jnp.tile(g_mla_q_nope[l], N_HEADS) * mla_scale,
            "q_pe": jnp.tile(g_mla_q_pe[l], N_HEADS) * mla_scale,
            "k_nope": jnp.tile(g_mla_k_nope[l], N_HEADS),
            "k_pe": jnp.tile(g_mla_k_pe[l], LANES // MLA_ROPE),
            "dil_q": jnp.tile(g_dil_q[l], N_HEADS) * dil_scale,
            "dil_k": jnp.tile(g_dil_k[l], N_HEADS)}
        gains = jnp.stack([jnp.pad(gains[n], (0, D - w)) for n, w in GAIN_ROWS])
        consts = (gains, w_in_p, w_qb_p, w_kvb_p, seg64, seg32, freq, expand)
        qn, qpe, kn, kpe, vm, qd, kd, vd = _proj(x, pos3, mod3, consts)

        two = ATT_PAIRS * LANES
        o_mla = _attn_call(_mla_kernel, "mla", (qn, qpe, kn, kpe, vm), (two, LANES, two, None, two),
                           2 * LANES, (_mla_bias(),), B, S)
        o_dil = _attn_call(_dil_kernel, "dil", (qd, kd, vd), (two, two, two), 2 * LANES,
                           _dil_constants(), B, S)

        x = _ffn(x, o_mla, o_dil, mod3, g_ffn_norm[l][None, :], w_o[l].astype(BF16),
                 w_up[l].astype(BF16), w_conv[l], b_conv[l][None, :], w_down[l].astype(BF16))
    return x
```

```python
import numpy as np
import jax
import jax.numpy as jnp
from jax import lax
from jax.experimental import pallas as pl
from jax.experimental.pallas import tpu as pltpu

F32 = jnp.float32
BF16 = jnp.bfloat16

D_MODEL = 1024
N_HEADS = 8
MLA_NOPE = 64
MLA_ROPE = 32
MLA_V = 64
MLA_Q_LORA = 512
MLA_KV_LORA = 256
DIL_HEAD_DIM = 64
DIL_WIDTH = N_HEADS * DIL_HEAD_DIM
DIL_PATTERNS = ((128, 1), (512, 4), (2048, 16))
D_FF = 2816
ROPE_THETA = 10000.0
EPS = 1e-6
MASK_BIAS = -1e30
LOG2E = 1.4426950408889634

LANES = 128
MXU_DIM = 256
VMEM_LIMIT = 56 * 1024 * 1024
ADA_TN = 1024
PROJ_TM = 256
ATT_TQ = 256
ATT_TK = 256
VT_ROWS = 64 + 16
ATT_PAIRS = 2
FFN_TM = 512
FFN_TF = 256
N_FT = D_FF // FFN_TF
DOWN_TILES = 2
STAGE_AT = N_FT - 4

C_QLAT = 0
C_KVLAT = C_QLAT + MLA_Q_LORA
C_KPE = C_KVLAT + MLA_KV_LORA
C_QD = C_KPE + LANES
C_KD = C_QD + DIL_WIDTH
C_VD = C_KD + DIL_WIDTH


def _dot(a, b):
    return jnp.dot(a, b, preferred_element_type=F32)


def _const_spec(shape):
    nd = len(shape)
    return pl.BlockSpec(shape, lambda *_: (0,) * nd, pipeline_mode=pl.Buffered(1))


def _split_bf16(v):
    hi = v.astype(BF16)
    lo = (v - hi.astype(F32)).astype(BF16)
    return hi, lo


def _ada_kernel(c_ref, w_ref, b_ref, o_ref):
    c = c_ref[...]
    a = c * (1.0 / (1.0 + jnp.exp(-c)))
    n = a.shape[0]
    o2 = _dot(jnp.concatenate(_split_bf16(a), axis=0), w_ref[...].astype(BF16))
    o_ref[...] = o2[:n] + o2[n:] + b_ref[...]


def _ada(c, w_ada, b_ada):
    B = c.shape[0]
    n = w_ada.shape[1]
    return pl.pallas_call(
        _ada_kernel,
        out_shape=jax.ShapeDtypeStruct((B, n), F32),
        grid=(n // ADA_TN,),
        in_specs=[pl.BlockSpec((B, D_MODEL), lambda j: (0, 0)),
                  pl.BlockSpec((D_MODEL, ADA_TN), lambda j: (0, j)),
                  pl.BlockSpec((1, ADA_TN), lambda j: (0, j))],
        out_specs=pl.BlockSpec((B, ADA_TN), lambda j: (0, j)),
        compiler_params=pltpu.CompilerParams(dimension_semantics=("arbitrary",),
                                             vmem_limit_bytes=VMEM_LIMIT),
        name="ada",
    )(c, w_ada, b_ada)


def _rms_rows(v, width):
    ms = jnp.sum(v * v, axis=-1, keepdims=True) * (1.0 / width)
    return v * lax.rsqrt(ms + EPS)


def _seg_rms(v, seg_ref, seg_width):
    outs = []
    for c0 in range(0, v.shape[1], MXU_DIM):
        blk = v[:, c0:c0 + MXU_DIM]
        ss = _dot((blk * blk).astype(BF16), seg_ref[...])
        outs.append(blk * lax.rsqrt(ss * (1.0 / seg_width) + EPS))
    return outs[0] if len(outs) == 1 else jnp.concatenate(outs, axis=1)


def _rope(v, cos, sin_signed, half):
    lane = lax.broadcasted_iota(jnp.int32, (v.shape[0], LANES), 1)
    first = lane % (2 * half) < half
    outs = []
    for c0 in range(0, v.shape[1], LANES):
        blk = v[:, c0:c0 + LANES]
        partner = jnp.where(first, pltpu.roll(blk, LANES - half, 1), pltpu.roll(blk, half, 1))
        outs.append(blk * cos + partner * sin_signed)
    return outs[0] if len(outs) == 1 else jnp.concatenate(outs, axis=1)


GAIN_ROWS = (("mix", D_MODEL), ("q_lat", MLA_Q_LORA), ("kv_lat", MLA_KV_LORA),
             ("q_nope", N_HEADS * MLA_NOPE), ("q_pe", N_HEADS * MLA_ROPE), ("k_nope", N_HEADS * MLA_NOPE),
             ("k_pe", LANES), ("dil_q", DIL_WIDTH), ("dil_k", DIL_WIDTH))


class _Rows:
    def __init__(self, ref):
        self._ref = ref

    def __getattr__(self, name):
        idx, width = next((i, w) for i, (n, w) in enumerate(GAIN_ROWS) if n == name)
        return self._ref[idx:idx + 1, :width]


def _proj_kernel(x_ref, pos_ref, mod_ref, gains_ref, win_ref, wqb_ref, wkvb_ref,
                 seg64_ref, seg32_ref, freq_ref, expand_ref,
                 qn_ref, qpe_ref, kn_ref, kpe_ref, vm_ref, qd_ref, kd_ref, vd_ref):
    g = _Rows(gains_ref)
    hd_half, pe_half = DIL_HEAD_DIM // 2, MLA_ROPE // 2
    x = x_ref[0]
    sh1 = mod_ref[0, 0:1, :]
    sc1 = mod_ref[0, 1:2, :]
    h = (_rms_rows(x, D_MODEL) * (g.mix * (1.0 + sc1)) + sh1).astype(BF16)

    q_lat = _dot(h, win_ref[:, C_QLAT:C_QLAT + MLA_Q_LORA])
    kv_lat = _dot(h, win_ref[:, C_KVLAT:C_KVLAT + MLA_KV_LORA])
    qd_raw = _dot(h, win_ref[:, C_QD:C_QD + DIL_WIDTH])

    pos = pos_ref[0].astype(F32)
    ang = freq_ref[...] * pos
    cs = jnp.concatenate([jnp.cos(ang), jnp.sin(ang)], axis=0).T
    cs_hi, cs_lo = _split_bf16(cs)
    tabs = _dot(cs_hi, expand_ref[...]) + _dot(cs_lo, expand_ref[...])
    cos_d, sin_d = tabs[:, 0:LANES], tabs[:, LANES:2 * LANES]
    cos_p, sin_p = tabs[:, 2 * LANES:3 * LANES], tabs[:, 3 * LANES:4 * LANES]

    q_in = (_rms_rows(q_lat, MLA_Q_LORA) * g.q_lat).astype(BF16)
    q = _dot(q_in, wqb_ref[...])
    kv_in = (_rms_rows(kv_lat, MLA_KV_LORA) * g.kv_lat).astype(BF16)
    kv = _dot(kv_in, wkvb_ref[...])
    kd_raw = _dot(h, win_ref[:, C_KD:C_KD + DIL_WIDTH])
    qd_ref[0] = _rope(_seg_rms(qd_raw, seg64_ref, DIL_HEAD_DIM) * g.dil_q, cos_d, sin_d, hd_half).astype(BF16)
    k_pe = _dot(h, win_ref[:, C_KPE:C_KPE + LANES])
    vd_ref[0] = _dot(h, win_ref[:, C_VD:C_VD + DIL_WIDTH]).astype(BF16)

    n_nope = N_HEADS * MLA_NOPE
    qn_ref[0] = (_seg_rms(q[:, :n_nope], seg64_ref, MLA_NOPE) * g.q_nope).astype(BF16)
    q_pe = _seg_rms(q[:, n_nope:], seg32_ref, MLA_ROPE) * g.q_pe
    qpe_ref[0] = _rope(q_pe, cos_p, sin_p, pe_half).astype(BF16)
    kn_ref[0] = (_seg_rms(kv[:, :n_nope], seg64_ref, MLA_NOPE) * g.k_nope).astype(BF16)
    vm_ref[0] = kv[:, n_nope:].astype(BF16)
    kd_ref[0] = _rope(_seg_rms(kd_raw, seg64_ref, DIL_HEAD_DIM) * g.dil_k, cos_d, sin_d, hd_half).astype(BF16)
    kpe_ref[0] = _rope(_rms_rows(k_pe, LANES) * g.k_pe, cos_p, sin_p, pe_half).astype(BF16)


def _proj(x, pos3, mod3, consts):
    B, S, _ = x.shape
    tm = PROJ_TM
    row = lambda w: pl.BlockSpec((1, tm, w), lambda b, i: (b, i, 0))
    out_widths = (512, 256, 512, LANES, 512, 512, 512, 512)
    in_specs = [row(D_MODEL),
                pl.BlockSpec((1, 1, tm), lambda b, i: (b, 0, i)),
                pl.BlockSpec((1, 6, D_MODEL), lambda b, i: (b, 0, 0))]
    in_specs += [_const_spec(a.shape) for a in consts]
    return pl.pallas_call(
        _proj_kernel,
        out_shape=tuple(jax.ShapeDtypeStruct((B, S, w), BF16) for w in out_widths),
        grid=(B, S // tm),
        in_specs=in_specs,
        out_specs=tuple(row(w) for w in out_widths),
        compiler_params=pltpu.CompilerParams(dimension_semantics=("arbitrary", "arbitrary"),
                                             vmem_limit_bytes=VMEM_LIMIT),
        name="proj",
    )(x, pos3, mod3, *consts)


def _row_iota(shape):
    return lax.broadcasted_iota(jnp.int32, shape, 0)


def _sublane_allreduce(v, op):
    for shift in (4, 2, 1):
        v = op(v, pltpu.roll(v, shift, 0))
    return v


def _transpose_bf16(v):
    return v.astype(F32).T.astype(BF16)


class _Stream:
    def __init__(self, load_qt2, load_k, load_v, store_o):
        self.load_qt2, self.load_k, self.load_v, self.store_o = load_qt2, load_k, load_v, store_o


def _pair_attention(streams, n_rows, bias_ref, near_width, vt_scr, qt_scr, s_scr, m_scr, l_scr, acc_scr):
    tq, tk = ATT_TQ, ATT_TK
    assert tq == tk
    n_q = n_rows // tq
    near = bias_ref.shape[0]
    hd = LANES // 2
    ids = range(len(streams))

    def put_queries(i):
        for s in ids:
            qt_scr[s] = streams[s].load_qt2(pl.ds(i * tq, tq))

    def put_scores(i, j, slot):
        far = i - j >= near
        k_rows = pl.ds(j * tk, tk)
        for s in ids:
            qt = qt_scr[s] if far else qt_scr[s, :near_width, :]
            s_scr[s, slot] = _dot(streams[s].load_k(k_rows, far), qt)

    def consume(j, slot, bias_idx):
        for s in ids:
            st = s_scr[s, slot]
            if bias_idx is not None:
                st = st + bias_ref[bias_idx]
            s3 = st.reshape(tk // 8, 8, 2 * tq)
            m_new = _sublane_allreduce(jnp.max(s3, axis=0), jnp.maximum)
            if j > 0:
                m_old = m_scr[s]
                m_new = jnp.maximum(m_old, m_new)
                alpha = jnp.exp2(m_old - m_new)
            p3 = jnp.exp2(s3 - m_new[None])
            m_scr[s] = m_new
            pt = p3.reshape(tk, 2 * tq).astype(BF16)
            vt = vt_scr[s, j]
            l_new = []
            for h in range(2):
                pv = _dot(vt[h * VT_ROWS:(h + 1) * VT_ROWS, :], pt[:, h * tq:(h + 1) * tq])
                l_new.append(pv[hd:hd + 8, :])
                pv = pv[:hd, :]
                if j > 0:
                    a3 = alpha[:, h * tq:(h + 1) * tq][None]
                    pv = (acc_scr[s, h].reshape(hd // 8, 8, tq) * a3).reshape(hd, tq) + pv
                acc_scr[s, h] = pv
            l_new = jnp.concatenate(l_new, axis=1)
            l_scr[s] = l_new if j == 0 else alpha * l_scr[s] + l_new

    def step(i, j, slot):
        put_scores(i, j + 1, 1 - slot)
        consume(j, slot, i - j if i - j < near else None)

    ones = jnp.ones((VT_ROWS - hd, tk), BF16)
    for s in ids:
        for j in range(n_rows // tk):
            vt = _transpose_bf16(streams[s].load_v(pl.ds(j * tk, tk)))
            vt_scr[s, j] = jnp.concatenate([vt[:hd], ones, vt[hd:], ones], axis=0)
    put_queries(0)
    put_scores(0, 0, 0)

    t = 0
    for i in range(n_q):
        for j in range(i):
            step(i, j, t % 2)
            t += 1
        i_next = min(i + 1, n_q - 1)
        put_queries(i_next)
        put_scores(i_next, 0, 1 - t % 2)
        consume(i, t % 2, 0)
        t += 1

        for s in ids:
            inv_l = 1.0 / l_scr[s]
            ot = jnp.concatenate(
                [(acc_scr[s, h].reshape(hd // 8, 8, tq) * inv_l[:, h * tq:(h + 1) * tq][None]).reshape(hd, tq)
                 for h in range(2)], axis=0)
            streams[s].store_o(pl.ds(i * tq, tq), ot.T)


def _lane_block(ref, s):
    cols = slice(s * LANES, (s + 1) * LANES)
    return lambda rows: ref[0, rows, cols]


def _mla_kernel(qn_ref, qpe_ref, kn_ref, kpe_ref, v_ref, bias_ref, o_ref, *scratch):
    feat = _row_iota((2 * LANES, ATT_TQ))
    pe_head = (feat % LANES) // MLA_ROPE
    is_pe = feat >= LANES
    n_streams = o_ref.shape[2] // LANES
    assert n_streams == 2

    def make(s):
        mask_a = (feat < LANES // 2) | (is_pe & (pe_head == 2 * s))
        mask_b = ((feat >= LANES // 2) & (feat < LANES)) | (is_pe & (pe_head == 2 * s + 1))
        qn, kn = _lane_block(qn_ref, s), _lane_block(kn_ref, s)

        def load_qt2(rows):
            qt = jnp.concatenate([qn(rows), qpe_ref[0, rows, :]], axis=1).astype(F32).T
            z = jnp.zeros_like(qt)
            return jnp.concatenate([jnp.where(mask_a, qt, z), jnp.where(mask_b, qt, z)],
                                   axis=1).astype(BF16)

        def load_k(rows, far):
            return jnp.concatenate([kn(rows), kpe_ref[0, rows, :]], axis=1)

        def store_o(rows, val):
            o_ref[0, rows, s * LANES:(s + 1) * LANES] = val.astype(o_ref.dtype)

        return _Stream(load_qt2, load_k, _lane_block(v_ref, s), store_o)

    _pair_attention([make(s) for s in range(n_streams)], o_ref.shape[1], bias_ref, 2 * LANES, *scratch)


def _dil_kernel(q_ref, k_ref, v_ref, qclass_ref, kmask_ref, bias_ref, o_ref, *scratch):
    feat = _row_iota((LANES, ATT_TQ))
    is_a = feat < DIL_HEAD_DIM

    def make(s):
        q, k = _lane_block(q_ref, s), _lane_block(k_ref, s)

        def load_qt2(rows):
            qt = q(rows).astype(F32).T
            z = jnp.zeros_like(qt)
            feats = jnp.concatenate([jnp.where(is_a, qt, z), jnp.where(is_a, z, qt)], axis=1)
            return jnp.concatenate([feats.astype(BF16), qclass_ref[...]], axis=0)

        def load_k(rows, far):
            return jnp.concatenate([k(rows), kmask_ref[...]], axis=1) if far else k(rows)

        def store_o(rows, val):
            o_ref[0, rows, s * LANES:(s + 1) * LANES] = val.astype(o_ref.dtype)

        return _Stream(load_qt2, load_k, _lane_block(v_ref, s), store_o)

    _pair_attention([make(s) for s in range(o_ref.shape[2] // LANES)], o_ref.shape[1], bias_ref,
                    LANES, *scratch)


def _attn_call(kernel, name, arrays, widths, qk_width, consts, B, S):
    n_s = ATT_PAIRS
    col = lambda w: pl.BlockSpec((1, S, w), lambda b, g: (b, 0, g))
    shared = pl.BlockSpec((1, S, LANES), lambda b, g: (b, 0, 0))
    in_specs = [shared if w is None else col(w) for w in widths] + [_const_spec(c.shape) for c in consts]
    return pl.pallas_call(
        kernel,
        out_shape=jax.ShapeDtypeStruct((B, S, N_HEADS * MLA_V), BF16),
        grid=(B, N_HEADS // 2 // n_s),
        in_specs=in_specs,
        out_specs=col(n_s * LANES),
        scratch_shapes=[pltpu.VMEM((n_s, S // ATT_TK, 2 * VT_ROWS, ATT_TK), BF16),
                        pltpu.VMEM((n_s, qk_width, 2 * ATT_TQ), BF16),
                        pltpu.VMEM((n_s, 2, ATT_TK, 2 * ATT_TQ), F32),
                        pltpu.VMEM((n_s, 8, 2 * ATT_TQ), F32),
                        pltpu.VMEM((n_s, 8, 2 * ATT_TQ), F32),
                        pltpu.VMEM((n_s, 2, LANES // 2, ATT_TQ), F32)],
        compiler_params=pltpu.CompilerParams(dimension_semantics=("arbitrary", "arbitrary"),
                                             vmem_limit_bytes=VMEM_LIMIT),
        name=name,
    )(*arrays, *consts)


def _two_heads(tile_qk):
    return np.concatenate([tile_qk.T, tile_qk.T], axis=1)


def _mla_bias():
    d = np.arange(ATT_TQ)[:, None] - np.arange(ATT_TK)[None, :]
    tile = np.where(d >= 0, 0.0, MASK_BIAS).astype(np.float32)
    return jnp.asarray(_two_heads(tile)[None])


def _dil_constants():
    (far_window, far_dil), = [(w, d) for w, d in DIL_PATTERNS if w == max(p[0] for p in DIL_PATTERNS)]
    assert ATT_TK % far_dil == 0 and far_dil <= LANES
    max_near = max(w for w, d in DIL_PATTERNS if w < far_window)
    n_near = -(-max_near // ATT_TK) + 1
    tiles = []
    for blk in range(n_near):
        delta = blk * ATT_TK + np.arange(ATT_TQ)[:, None] - np.arange(ATT_TK)[None, :]
        mult = np.zeros(delta.shape, np.int64)
        for window, dil in DIL_PATTERNS:
            mult += (delta >= 0) & (delta % dil == 0) & (delta <= window)
        tile = np.where(mult > 0, np.log2(np.maximum(mult, 1)), MASK_BIAS).astype(np.float32)
        tiles.append(_two_heads(tile))
    feat = np.arange(LANES)
    q_class = (feat[:, None] == np.arange(2 * ATT_TQ)[None, :] % far_dil).astype(np.float32)
    k_mask = np.where(feat[None, :] >= far_dil, 0.0,
                      np.where(np.arange(ATT_TK)[:, None] % far_dil == feat[None, :], 0.0, MASK_BIAS))
    return (jnp.asarray(q_class, dtype=BF16), jnp.asarray(k_mask.astype(np.float32), dtype=BF16),
            jnp.asarray(np.stack(tiles)))


def _ffn_kernel(x0_ref, om0_ref, od0_ref, mod0_ref, xn_ref, omn_ref, odn_ref, modn_ref, mod_ref,
                gffn_ref, wo_ref, wup_ref, wconv_ref, bconv_ref, wdown_ref, out_ref,
                carry_scr, acc_scr, h2_scr, y_scr, a_scr, x1n_scr, h2n_scr):
    half = wo_ref.shape[0] // 2

    def stage_tile(x_ref, om_ref, od_ref, m_ref):
        g1, sh2, sc2 = m_ref[0, 2:3, :], m_ref[0, 3:4, :], m_ref[0, 4:5, :]
        mix = _dot(om_ref[0], wo_ref[:half, :]) + _dot(od_ref[0], wo_ref[half:, :])
        x1 = x_ref[0] + g1 * mix
        x1n_scr[...] = x1
        h2n_scr[...] = (_rms_rows(x1, D_MODEL) * (gffn_ref[...] * (1.0 + sc2)) + sh2).astype(BF16)

    @pl.when((pl.program_id(0) == 0) & (pl.program_id(1) == 0))
    def _():
        stage_tile(x0_ref, om0_ref, od0_ref, mod0_ref)

    g2 = mod_ref[0, 5:6, :]
    h2_scr[...] = h2n_scr[...]
    out_ref[0] = x1n_scr[...]

    @pl.when(pl.program_id(1) == 0)
    def _():
        carry_scr[...] = jnp.zeros(carry_scr.shape, F32)

    def up(f):
        for half_idx, t in enumerate((f, f + N_FT)):
            y = _dot(h2_scr[...], wup_ref[:, t * FFN_TF:(t + 1) * FFN_TF])
            y_scr[f % 2, half_idx, 0:8, :] = carry_scr[t]
            y_scr[f % 2, half_idx, 8:, :] = y
            carry_scr[t] = y[FFN_TM - 8:, :]

    def conv(f, half_idx):
        t = f + half_idx * N_FT
        yb = y_scr.at[f % 2, half_idx]
        cols = slice(t * FFN_TF, (t + 1) * FFN_TF)
        w = wconv_ref[:, cols]
        return (w[2:3, :] * yb[8:8 + FFN_TM, :] + w[1:2, :] * yb[7:7 + FFN_TM, :]
                + w[0:1, :] * yb[6:6 + FFN_TM, :] + bconv_ref[:, cols])

    def down(f0, f1):
        d = _dot(a_scr[(f0 // DOWN_TILES) % 2, :, :(f1 - f0) * FFN_TF],
                 wdown_ref[f0 * FFN_TF:f1 * FFN_TF, :])
        if f0 == 0:
            acc_scr[...] = d
        else:
            acc_scr[...] += d

    up(0)
    pending = None
    for f in range(N_FT):
        if f + 1 < N_FT:
            up(f + 1)
        if pending is not None:
            down(*pending)
            pending = None
        if f == STAGE_AT:
            stage_tile(xn_ref, omn_ref, odn_ref, modn_ref)
        gate = conv(f, 0)
        val = conv(f, 1)
        k = f % DOWN_TILES
        a_scr[(f // DOWN_TILES) % 2, :, k * FFN_TF:(k + 1) * FFN_TF] = (
            gate * (1.0 / (1.0 + jnp.exp(-gate))) * val).astype(BF16)
        if k == DOWN_TILES - 1 or f == N_FT - 1:
            pending = (f - k, f + 1)
    down(*pending)
    out_ref[0] = out_ref[0] + g2 * acc_scr[...]


def _ffn(x, o_mla, o_dil, mod3, g_ffn, w_o, w_up, w_conv, b_conv, w_down):
    B, S, _ = x.shape
    tm = FFN_TM
    nt = S // tm

    def nxt(b, i):
        t = jnp.minimum(b * nt + i + 1, B * nt - 1)
        return t // nt, t % nt

    first = lambda w: pl.BlockSpec((1, tm, w), lambda b, i: (0, 0, 0), pipeline_mode=pl.Buffered(1))
    following = lambda w: pl.BlockSpec((1, tm, w), lambda b, i: (*nxt(b, i), 0))
    mod_spec = lambda fn: pl.BlockSpec((1, 6, D_MODEL), fn)
    wm, wd = o_mla.shape[2], o_dil.shape[2]
    consts = (g_ffn, w_o, w_up, w_conv, b_conv, w_down)
    return pl.pallas_call(
        _ffn_kernel,
        out_shape=jax.ShapeDtypeStruct(x.shape, F32),
        grid=(B, nt),
        in_specs=[first(D_MODEL), first(wm), first(wd), mod_spec(lambda b, i: (0, 0, 0)),
                  following(D_MODEL), following(wm), following(wd),
                  mod_spec(lambda b, i: (nxt(b, i)[0], 0, 0)), mod_spec(lambda b, i: (b, 0, 0))]
                 + [_const_spec(a.shape) for a in consts],
        out_specs=pl.BlockSpec((1, tm, D_MODEL), lambda b, i: (b, i, 0)),
        scratch_shapes=[pltpu.VMEM((2 * N_FT, 8, FFN_TF), F32),
                        pltpu.VMEM((tm, D_MODEL), F32),
                        pltpu.VMEM((tm, D_MODEL), BF16),
                        pltpu.VMEM((2, 2, tm + 8, FFN_TF), F32),
                        pltpu.VMEM((2, tm, DOWN_TILES * FFN_TF), BF16),
                        pltpu.VMEM((tm, D_MODEL), F32),
                        pltpu.VMEM((tm, D_MODEL), BF16)],
        compiler_params=pltpu.CompilerParams(dimension_semantics=("arbitrary", "arbitrary"),
                                             vmem_limit_bytes=VMEM_LIMIT),
        name="ffn",
    )(x, o_mla, o_dil, mod3, x, o_mla, o_dil, mod3, mod3, *consts)


def _segment_matrices():
    i = np.arange(MXU_DIM)
    seg64 = i // 64
    seg32 = i // MLA_ROPE
    same = lambda s: jnp.asarray((s[:, None] == s[None, :]).astype(np.float32), dtype=BF16)
    return same(seg64), same(seg32)


def _rope_constants():
    def inv_freq(d):
        half = d // 2
        return jnp.power(ROPE_THETA, -2.0 * jnp.arange(half, dtype=F32) / d)
    fd, fp = inv_freq(DIL_HEAD_DIM), inv_freq(MLA_ROPE)
    nd, npe = fd.shape[0], fp.shape[0]
    freq = jnp.concatenate([fd, fp, jnp.zeros((64 - nd - npe,), F32)])[:, None]
    lane = np.arange(LANES)
    sign = lambda half: np.where(lane % (2 * half) < half, -1.0, 1.0)
    e = np.zeros((2 * 64, 4 * LANES), np.float32)
    e[lane % nd, lane] = 1.0
    e[64 + lane % nd, LANES + lane] = sign(nd)
    e[nd + lane % npe, 2 * LANES + lane] = 1.0
    e[64 + nd + lane % npe, 3 * LANES + lane] = sign(npe)
    return freq, jnp.asarray(e, dtype=BF16)


def kernel(x, c, positions, w_ada, b_ada, g_mix_norm, w_in, g_q_lat, w_q_b, g_kv_lat, w_kv_b,
           g_mla_q_nope, g_mla_q_pe, g_mla_k_nope, g_mla_k_pe, g_dil_q, g_dil_k, w_o,
           g_ffn_norm, w_up, w_conv, b_conv, w_down):
    B, S, D = x.shape
    assert D == D_MODEL and S % PROJ_TM == 0 and S % ATT_TQ == 0 and S % FFN_TM == 0
    assert S <= max(w for w, _ in DIL_PATTERNS)
    seg64, seg32 = _segment_matrices()
    freq, expand = _rope_constants()
    mla_scale = (MLA_NOPE + MLA_ROPE) ** -0.5 * LOG2E
    dil_scale = DIL_HEAD_DIM ** -0.5 * LOG2E
    pos3 = positions.reshape(B, 1, S)

    for l in range(w_ada.shape[0]):
        mod3 = _ada(c, w_ada[l], b_ada[l][None, :]).reshape(B, 6, D)

        wi = w_in[l]
        c_kpe = MLA_Q_LORA + MLA_KV_LORA
        c_qd = c_kpe + MLA_ROPE
        w_in_p = jnp.concatenate([wi[:, :c_kpe], jnp.tile(wi[:, c_kpe:c_qd], (1, LANES // MLA_ROPE)),
                                  wi[:, c_qd:]], axis=1).astype(BF16)
        wq = w_q_b[l].reshape(MLA_Q_LORA, N_HEADS, MLA_NOPE + MLA_ROPE)
        w_qb_p = jnp.concatenate([wq[:, :, :MLA_NOPE].reshape(MLA_Q_LORA, -1),
                                  wq[:, :, MLA_NOPE:].reshape(MLA_Q_LORA, -1)], axis=1).astype(BF16)
        wkv = w_kv_b[l].reshape(MLA_KV_LORA, N_HEADS, MLA_NOPE + MLA_V)
        w_kvb_p = jnp.concatenate([wkv[:, :, :MLA_NOPE].reshape(MLA_KV_LORA, -1),
                                   wkv[:, :, MLA_NOPE:].reshape(MLA_KV_LORA, -1)], axis=1).astype(BF16)
        gains = {
            "mix": g_mix_norm[l], "q_lat": g_q_lat[l], "kv_lat": g_kv_lat[l],
            "q_nope": jnp.tile(g_mla_q_nope[l], N_HEADS) * mla_scale,
            "q_pe": jnp.tile(g_mla_q_pe[l], N_HEADS) * mla_scale,
            "k_nope": jnp.tile(g_mla_k_nope[l], N_HEADS),
            "k_pe": jnp.tile(g_mla_k_pe[l], LANES // MLA_ROPE),
            "dil_q": jnp.tile(g_dil_q[l], N_HEADS) * dil_scale,
            "dil_k": jnp.tile(g_dil_k[l], N_HEADS)}
        gains = jnp.stack([jnp.pad(gains[n], (0, D - w)) for n, w in GAIN_ROWS])
        consts = (gains, w_in_p, w_qb_p, w_kvb_p, seg64, seg32, freq, expand)
        qn, qpe, kn, kpe, vm, qd, kd, vd = _proj(x, pos3, mod3, consts)

        two = ATT_PAIRS * LANES
        o_mla = _attn_call(_mla_kernel, "mla", (qn, qpe, kn, kpe, vm), (two, LANES, two, None, two),
                           2 * LANES, (_mla_bias(),), B, S)
        o_dil = _attn_call(_dil_kernel, "dil", (qd, kd, vd), (two, two, two), 2 * LANES,
                           _dil_constants(), B, S)

        x = _ffn(x, o_mla, o_dil, mod3, g_ffn_norm[l][None, :], w_o[l].astype(BF16),
                 w_up[l].astype(BF16), w_conv[l], b_conv[l][None, :], w_down[l].astype(BF16))
    return x
```
